```python
import jax
import jax.numpy as jnp
from jax import lax
import numpy as np

D_MODEL = 2048
BATCH = 4
SEQ = 4096
DEPTH = 1

MLA_HEADS = 8
MLA_Q_RANK = 512
MLA_KV_RANK = 512
MLA_NOPE = 128
MLA_ROPE = 64
MLA_V = 128
MLA_QK = MLA_NOPE + MLA_ROPE
Q_BLOCK = 128
RET_HEADS = 8
RET_DK = D_MODEL // RET_HEADS
RET_DV = D_MODEL // RET_HEADS
RET_CHUNK = 128
N_EXPERTS = 32
TOP_K = 4
D_EXPERT = D_MODEL
SWIGLU_LIMIT = 7.0
SWIGLU_ALPHA = 1.702
MOE_BLOCK = 128
ROPE_THETA = 10000.0
RMS_EPS = 1e-6
GN_EPS = 1e-6
N_MOD = 6

IN_SPLITS = (MLA_Q_RANK, MLA_KV_RANK + MLA_ROPE,
             RET_HEADS * RET_DK, RET_HEADS * RET_DK, RET_HEADS * RET_DV, RET_HEADS * RET_DV,
             D_MODEL, D_MODEL)
D_IN = sum(IN_SPLITS)

kernel_name = 'hybrid_mla_retention_moe_encoder'


def split_cols(t, sizes):
    out, start = [], 0
    for s in sizes:
        out.append(t[..., start:start + s])
        start += s
    return out


def rms_norm(x, gain):
    xf = x.astype(jnp.float32)
    y = xf * lax.rsqrt(jnp.mean(xf * xf, axis=-1, keepdims=True) + RMS_EPS)
    return (y * gain.astype(jnp.float32)).astype(x.dtype)


def rotary(x, positions):
    half = x.shape[-1] // 2
    inv = 1.0 / (ROPE_THETA ** (jnp.arange(half, dtype=jnp.float32) / half))
    ang = positions.astype(jnp.float32)[:, None, :, None] * inv
    cos, sin = jnp.cos(ang), jnp.sin(ang)
    xf = x.astype(jnp.float32)
    x1, x2 = xf[..., :half], xf[..., half:]
    return jnp.concatenate([x1 * cos - x2 * sin, x1 * sin + x2 * cos], axis=-1).astype(x.dtype)


def mla_branch(q_lat, kv_lat, positions, q_a_norm, w_q_b, kv_a_norm, w_kv_b, w_mla_o):
    B, S, _ = q_lat.shape
    H = MLA_HEADS
    q = (rms_norm(q_lat, q_a_norm) @ w_q_b).reshape(B, S, H, MLA_QK).transpose(0, 2, 1, 3)
    q = jnp.concatenate([q[..., :MLA_NOPE], rotary(q[..., MLA_NOPE:], positions)], axis=-1)
    q = q * (MLA_QK ** -0.5)
    c_kv, k_rope = kv_lat[..., :MLA_KV_RANK], kv_lat[..., MLA_KV_RANK:]
    k_rope = rotary(k_rope[:, None], positions)
    kv = (rms_norm(c_kv, kv_a_norm) @ w_kv_b).reshape(B, S, H, MLA_NOPE + MLA_V).transpose(0, 2, 1, 3)
    k = jnp.concatenate([kv[..., :MLA_NOPE], jnp.broadcast_to(k_rope, (B, H, S, MLA_ROPE))], axis=-1)
    v = kv[..., MLA_NOPE:]
    nb = S // Q_BLOCK
    q_blocks = q.reshape(B, H, nb, Q_BLOCK, MLA_QK).transpose(2, 0, 1, 3, 4)

    def attend(qb):
        s = jnp.einsum('bhqd,bhkd->bhqk', qb, k).astype(jnp.float32)
        p = jax.nn.softmax(s, axis=-1).astype(v.dtype)
        return jnp.einsum('bhqk,bhkd->bhqd', p, v)

    o = lax.map(attend, q_blocks)
    o = o.transpose(1, 0, 3, 2, 4).reshape(B, S, H * MLA_V)
    return o @ w_mla_o


def retention_direction(q, k, v, gamma, include_diag):
    B, H, N, C, _ = q.shape
    dv = v.shape[-1]
    idx = jnp.arange(C, dtype=jnp.float32)
    log_g = jnp.log(gamma)[:, None]
    diff = idx[:, None] - idx[None, :]
    mask = (diff >= 0) if include_diag else (diff > 0)
    decay = jnp.where(mask, jnp.exp(log_g[..., None] * jnp.where(mask, diff, 0.0)), 0.0)
    scores = jnp.einsum('bhncd,bhnmd->bhncm', q, k) * decay[:, None]
    inner = jnp.einsum('bhncm,bhnmd->bhncd', scores, v)
    xi = jnp.exp(log_g * (idx + 1.0))
    zeta = jnp.exp(log_g * (C - 1.0 - idx))
    chunk_kv = jnp.einsum('bhnmk,bhnmv->bhnkv', k * zeta[:, None, :, None], v)
    g_chunk = jnp.exp(log_g[:, 0] * C)[None, :, None, None]

    def step(state, kv_i):
        return state * g_chunk + kv_i, state

    init = jnp.zeros((B, H, q.shape[-1], dv), jnp.float32)
    _, prev = lax.scan(step, init, chunk_kv.transpose(2, 0, 1, 3, 4))
    prev = prev.transpose(1, 2, 0, 3, 4)
    cross = jnp.einsum('bhnck,bhnkv->bhncv', q * xi[:, None, :, None], prev)
    return inner + cross


def retention_branch(rq, rk, rv, rg, positions, decay_fwd, decay_bwd, ret_gn, w_ret_o):
    B, S, _ = rq.shape
    H = RET_HEADS
    N = S // RET_CHUNK

    def heads(t, d):
        return t.reshape(B, S, H, d).transpose(0, 2, 1, 3)

    def chunks(t):
        return t.reshape(B, H, N, RET_CHUNK, t.shape[-1])

    def rev(t):
        return jnp.flip(t, axis=2)

    q = rotary(heads(rq, RET_DK), positions).astype(jnp.float32)
    k = rotary(heads(rk, RET_DK), positions).astype(jnp.float32) * (RET_DK ** -0.5)
    v = heads(rv, RET_DV).astype(jnp.float32)
    g_f = jax.nn.sigmoid(decay_fwd.astype(jnp.float32))
    g_b = jax.nn.sigmoid(decay_bwd.astype(jnp.float32))
    fwd = retention_direction(chunks(q), chunks(k), chunks(v), g_f, True)
    bwd = retention_direction(chunks(rev(q)), chunks(rev(k)), chunks(rev(v)), g_b, False)
    o = fwd.reshape(B, H, S, RET_DV) + rev(bwd.reshape(B, H, S, RET_DV))
    mu = jnp.mean(o, axis=-1, keepdims=True)
    var = jnp.mean(jnp.square(o - mu), axis=-1, keepdims=True)
    o = ((o - mu) * lax.rsqrt(var + GN_EPS)).transpose(0, 2, 1, 3).reshape(B, S, H * RET_DV)
    o = jax.nn.silu(rg.astype(jnp.float32)) * (o * ret_gn.astype(jnp.float32))
    return o.astype(rq.dtype) @ w_ret_o


def moe_ffn(h, w_router, b_router, w_gate_up, b_gate_up, w_down, b_down):
    B, S, D = h.shape
    T = B * S
    A = T * TOP_K
    xt = h.reshape(T, D)
    logits = (xt @ w_router + b_router).astype(jnp.float32)
    top_vals, top_idx = lax.top_k(logits, TOP_K)
    weights = jax.nn.softmax(top_vals, axis=-1)
    flat_e = top_idx.reshape(A).astype(jnp.int32)
    flat_t = jnp.arange(A, dtype=jnp.int32) // TOP_K
    flat_w = weights.reshape(A)
    order = jnp.argsort(flat_e)
    e_sorted = flat_e[order]
    counts = jnp.zeros((N_EXPERTS,), jnp.int32).at[flat_e].add(1)
    padded = ((counts + MOE_BLOCK - 1) // MOE_BLOCK) * MOE_BLOCK
    start = jnp.cumsum(counts) - counts
    pend = jnp.cumsum(padded)
    pstart = pend - padded
    dest = pstart[e_sorted] + (jnp.arange(A, dtype=jnp.int32) - start[e_sorted])
    n_blocks = (A + N_EXPERTS * (MOE_BLOCK - 1) + MOE_BLOCK - 1) // MOE_BLOCK
    P = n_blocks * MOE_BLOCK
    row_tok = jnp.zeros((P,), jnp.int32).at[dest].set(flat_t[order])
    row_w = jnp.zeros((P,), jnp.float32).at[dest].set(flat_w[order])
    block_e = jnp.clip(jnp.searchsorted(pend, jnp.arange(n_blocks, dtype=jnp.int32) * MOE_BLOCK,
                                        side='right'), 0, N_EXPERTS - 1)
    xs = xt[row_tok].reshape(n_blocks, MOE_BLOCK, D)

    def expert_block(args):
        xb, e = args
        gu = xb @ w_gate_up[e] + b_gate_up[e]
        gate = jnp.minimum(gu[..., :D_EXPERT], SWIGLU_LIMIT)
        up = jnp.clip(gu[..., D_EXPERT:], -SWIGLU_LIMIT, SWIGLU_LIMIT)
        glu = gate * jax.nn.sigmoid(SWIGLU_ALPHA * gate)
        return ((up + 1.0) * glu) @ w_down[e] + b_down[e]

    ys = lax.map(expert_block, (xs, block_e)).reshape(P, D)
    out = jnp.zeros((T, D), ys.dtype).at[row_tok].add(ys * row_w[:, None].astype(ys.dtype))
    return out.reshape(B, S, D)


def setup_inputs(seed: int = 0) -> dict:
    key = jax.random.key(seed)
    ks = jax.random.split(key, 32)
    L = DEPTH
    f32 = jnp.float32

    def nrm(k, shape, scale):
        return jax.random.normal(k, (L,) + shape, f32) * scale

    def gain(k, n):
        return 1.0 + 0.02 * jax.random.normal(k, (L, n), f32)

    base_logit = jnp.log(2.0 ** (5.0 + jnp.arange(RET_HEADS, dtype=f32)) - 1.0)
    x = jax.random.normal(ks[0], (BATCH, SEQ, D_MODEL), f32)
    c = jax.random.normal(ks[1], (BATCH, D_MODEL), f32)
    positions = (jnp.arange(SEQ, dtype=jnp.int32)[None, :]
                 + jax.random.randint(ks[2], (BATCH, 1), 0, 1024, dtype=jnp.int32))
    return {
        'x': x,
        'c': c,
        'positions': positions,
        'w_ada': nrm(ks[3], (D_MODEL, N_MOD * D_MODEL), 0.5 * D_MODEL ** -0.5),
        'b_ada': nrm(ks[4], (N_MOD * D_MODEL,), 0.01),
        'g_pre_mix': gain(ks[5], D_MODEL),
        'g_post_mix': gain(ks[6], D_MODEL),
        'g_pre_ffn': gain(ks[7], D_MODEL),
        'g_post_ffn': gain(ks[8], D_MODEL),
        'w_in': nrm(ks[9], (D_MODEL, D_IN), D_MODEL ** -0.5),
        'q_a_norm': gain(ks[10], MLA_Q_RANK),
        'w_q_b': nrm(ks[11], (MLA_Q_RANK, MLA_HEADS * MLA_QK), MLA_Q_RANK ** -0.5),
        'kv_a_norm': gain(ks[12], MLA_KV_RANK),
        'w_kv_b': nrm(ks[13], (MLA_KV_RANK, MLA_HEADS * (MLA_NOPE + MLA_V)), MLA_KV_RANK ** -0.5),
        'w_mla_o': nrm(ks[14], (MLA_HEADS * MLA_V, D_MODEL), (MLA_HEADS * MLA_V) ** -0.5),
        'ret_decay_fwd': base_logit[None, :] + nrm(ks[15], (RET_HEADS,), 0.01),
        'ret_decay_bwd': base_logit[None, :] + nrm(ks[16], (RET_HEADS,), 0.01),
        'ret_gn': gain(ks[17], RET_HEADS * RET_DV),
        'w_ret_o': nrm(ks[18], (RET_HEADS * RET_DV, D_MODEL), (RET_HEADS * RET_DV) ** -0.5),
        'w_out': nrm(ks[19], (D_MODEL, D_MODEL), D_MODEL ** -0.5),
        'w_router': nrm(ks[20], (D_MODEL, N_EXPERTS), D_MODEL ** -0.5),
        'b_router': nrm(ks[21], (N_EXPERTS,), 0.01),
        'w_gate_up': nrm(ks[22], (N_EXPERTS, D_MODEL, 2 * D_EXPERT), D_MODEL ** -0.5),
        'b_gate_up': nrm(ks[23], (N_EXPERTS, 2 * D_EXPERT), 0.01),
        'w_down': nrm(ks[24], (N_EXPERTS, D_EXPERT, D_MODEL), D_EXPERT ** -0.5),
        'b_down': nrm(ks[25], (N_EXPERTS, D_MODEL), 0.01),
    }


def reference(x, c, positions, w_ada, b_ada, g_pre_mix, g_post_mix, g_pre_ffn, g_post_ffn,
              w_in, q_a_norm, w_q_b, kv_a_norm, w_kv_b, w_mla_o, ret_decay_fwd, ret_decay_bwd,
              ret_gn, w_ret_o, w_out, w_router, b_router, w_gate_up, b_gate_up, w_down, b_down):
    for l in range(DEPTH):
        ada = (c @ w_ada[l] + b_ada[l])[:, None, :].astype(x.dtype)
        shift_m, scale_m, gate_m, shift_f, scale_f, gate_f = jnp.split(ada, N_MOD, axis=-1)
        h = rms_norm(x, g_pre_mix[l]) * (1.0 + scale_m) + shift_m
        q_lat, kv_lat, rq, rk, rv, rg, gate_a, gate_b = split_cols(h @ w_in[l], IN_SPLITS)
        y_a = mla_branch(q_lat, kv_lat, positions, q_a_norm[l], w_q_b[l], kv_a_norm[l],
                         w_kv_b[l], w_mla_o[l])
        y_b = retention_branch(rq, rk, rv, rg, positions, ret_decay_fwd[l], ret_decay_bwd[l],
                               ret_gn[l], w_ret_o[l])
        mix = (jax.nn.sigmoid(gate_a) * y_a + jax.nn.sigmoid(gate_b) * y_b) @ w_out[l]
        x = x + gate_m * rms_norm(mix, g_post_mix[l])
        h = rms_norm(x, g_pre_ffn[l]) * (1.0 + scale_f) + shift_f
        f = moe_ffn(h, w_router[l], b_router[l], w_gate_up[l], b_gate_up[l], w_down[l], b_down[l])
        x = x + gate_f * rms_norm(f, g_post_ffn[l])
    return x
```

```python
import functools

import numpy as np
import jax
import jax.numpy as jnp
from jax import lax
from jax.experimental import pallas as pl
from jax.experimental.pallas import tpu as pltpu

F32 = jnp.float32
BF16 = jnp.bfloat16

MLA_HEADS = 8
MLA_NOPE = 128
MLA_ROPE = 64
MLA_V = 128
MLA_QK = MLA_NOPE + MLA_ROPE
RET_HEADS = 8
N_EXPERTS = 32
TOP_K = 4
SWIGLU_LIMIT = 7.0
SWIGLU_ALPHA = 1.702
ROPE_THETA = 10000.0
RMS_EPS = 1e-6
GN_EPS = 1e-6
N_MOD = 6

LANES = 128
VMEM_LIMIT = 56 * 1024 * 1024


def _cparams(*sem):
    return pltpu.CompilerParams(dimension_semantics=sem, vmem_limit_bytes=VMEM_LIMIT)


def _rms(x, gain):
    return x * lax.rsqrt(jnp.mean(x * x, axis=-1, keepdims=True) + RMS_EPS) * gain


def _ada_kernel(c_ref, w_ref, b_ref, o_ref):
    o_ref[...] = jnp.dot(c_ref[...], w_ref[...], preferred_element_type=F32,
                         precision=lax.Precision.HIGHEST) + b_ref[...]


def _ada(c_pad, w_ada, b_ada, tn=1024):
    m, d = c_pad.shape
    n = w_ada.shape[1]
    return pl.pallas_call(
        _ada_kernel,
        out_shape=jax.ShapeDtypeStruct((m, n), F32),
        grid=(n // tn,),
        in_specs=[pl.BlockSpec((m, d), lambda j: (0, 0)),
                  pl.BlockSpec((d, tn), lambda j: (0, j)),
                  pl.BlockSpec((1, tn), lambda j: (0, j))],
        out_specs=pl.BlockSpec((m, tn), lambda j: (0, j)),
        compiler_params=_cparams("arbitrary"),
        name="ada",
    )(c_pad, w_ada, b_ada)


def _inproj_kernel(x_ref, g_ref, sc_ref, sh_ref, w_ref, wkr_ref, o_ref, okr_ref, h_ref):
    @pl.when(pl.program_id(1) == 0)
    def _():
        h = _rms(x_ref[...], g_ref[...]) * (1.0 + sc_ref[...]) + sh_ref[...]
        hb = h.astype(BF16)
        h_ref[...] = hb
        okr_ref[...] = jnp.dot(hb, wkr_ref[...], preferred_element_type=F32)

    o_ref[...] = jnp.dot(h_ref[...], w_ref[...], preferred_element_type=F32).astype(o_ref.dtype)


def _in_proj(x2, g, scale, shift, w_main, w_kr, seq, tm, tn):
    t, d = x2.shape
    n = w_main.shape[1]
    per_b = seq // tm
    return pl.pallas_call(
        _inproj_kernel,
        out_shape=(jax.ShapeDtypeStruct((t, n), BF16), jax.ShapeDtypeStruct((t, LANES), F32)),
        grid=(t // tm, n // tn),
        in_specs=[pl.BlockSpec((tm, d), lambda i, j: (i, 0)),
                  pl.BlockSpec((1, d), lambda i, j: (0, 0)),
                  pl.BlockSpec((None, 1, d), lambda i, j: (i // per_b, 0, 0)),
                  pl.BlockSpec((None, 1, d), lambda i, j: (i // per_b, 0, 0)),
                  pl.BlockSpec((d, tn), lambda i, j: (0, j)),
                  pl.BlockSpec((d, LANES), lambda i, j: (0, 0))],
        out_specs=(pl.BlockSpec((tm, tn), lambda i, j: (i, j)),
                   pl.BlockSpec((tm, LANES), lambda i, j: (i, 0))),
        scratch_shapes=[pltpu.VMEM((tm, d), BF16)],
        compiler_params=_cparams("arbitrary", "arbitrary"),
        name="in_proj",
    )(x2, g, scale, shift, w_main, w_kr)


def _rope_kernel(pos_ref, invr_ref, invm_ref, cr_ref, sr_ref, cm_ref, sm_ref):
    pos = pos_ref[...].astype(F32)
    ang_r = pos * invr_ref[...]
    cr_ref[...] = jnp.cos(ang_r)
    sr_ref[...] = jnp.sin(ang_r)
    ang_m = pos * invm_ref[...]
    lane = lax.broadcasted_iota(jnp.int32, ang_m.shape, 1)
    half = MLA_ROPE // 2
    cm_ref[...] = jnp.where(lane < MLA_ROPE, jnp.cos(ang_m), 0.0)
    s = jnp.sin(ang_m)
    sm_ref[...] = jnp.where(lane < half, -s, jnp.where(lane < MLA_ROPE, s, 0.0))


def _rope_tables(pos_col, tm):
    t = pos_col.shape[0]
    half_r = LANES
    inv_r = (1.0 / (np.float32(ROPE_THETA) ** (np.arange(half_r, dtype=np.float32) / np.float32(half_r)))
             ).astype(np.float32)
    half_m = MLA_ROPE // 2
    inv_m32 = (1.0 / (np.float32(ROPE_THETA) ** (np.arange(half_m, dtype=np.float32) / np.float32(half_m)))
               ).astype(np.float32)
    inv_m = np.zeros((LANES,), np.float32)
    inv_m[:half_m] = inv_m32
    inv_m[half_m:2 * half_m] = inv_m32
    tab = jax.ShapeDtypeStruct((t, LANES), F32)
    row = pl.BlockSpec((tm, LANES), lambda i: (i, 0))
    return pl.pallas_call(
        _rope_kernel,
        out_shape=(tab, tab, tab, tab),
        grid=(t // tm,),
        in_specs=[pl.BlockSpec((tm, 1), lambda i: (i, 0)),
                  pl.BlockSpec((1, LANES), lambda i: (0, 0)),
                  pl.BlockSpec((1, LANES), lambda i: (0, 0))],
        out_specs=(row, row, row, row),
        compiler_params=_cparams("arbitrary"),
        name="rope_tables",
    )(pos_col, jnp.asarray(inv_r)[None, :], jnp.asarray(inv_m)[None, :])


def _rot64(x, c, s):
    return x * c + (pltpu.roll(x, 32, 1) + pltpu.roll(x, 96, 1)) * s


def _mla_prep_kernel(ql_ref, ckv_ref, kr_ref, cm_ref, sm_ref, gq_ref, gkv_ref,
                     wqn_ref, wqr_ref, wk_ref, wv_ref, q_ref, k_ref, v_ref):
    c = cm_ref[...]
    s = sm_ref[...]
    qn = _rms(ql_ref[...].astype(F32), gq_ref[...]).astype(BF16)
    q_nope = jnp.dot(qn, wqn_ref[...], preferred_element_type=F32)
    q_rope = jnp.dot(qn, wqr_ref[...], preferred_element_type=F32)
    cn = _rms(ckv_ref[...].astype(F32), gkv_ref[...]).astype(BF16)
    k_nope = jnp.dot(cn, wk_ref[...], preferred_element_type=F32)
    v_ref[...] = jnp.dot(cn, wv_ref[...], preferred_element_type=F32).astype(BF16)
    k_rot = _rot64(kr_ref[...], c, s).astype(BF16)
    scale = MLA_QK ** -0.5
    for h in range(MLA_HEADS):
        lo, hi = h * LANES, (h + 1) * LANES
        q_ref[:, 2 * lo:2 * lo + LANES] = (q_nope[:, lo:hi] * scale).astype(BF16)
        q_ref[:, 2 * lo + LANES:2 * hi] = (_rot64(q_rope[:, lo:hi], c, s) * scale).astype(BF16)
        k_ref[:, 2 * lo:2 * lo + LANES] = k_nope[:, lo:hi].astype(BF16)
        k_ref[:, 2 * lo + LANES:2 * hi] = k_rot


def _mla_prep(proj, kr_raw, cm, sm, gq, gkv, wqn, wqr, wk, wv, ql_blk, ckv_blk, tm):
    t = proj.shape[0]
    rank = wqn.shape[0]
    hd = MLA_HEADS * LANES
    full = lambda shape: pl.BlockSpec(shape, lambda i: (0, 0))
    row = lambda w: pl.BlockSpec((tm, w), lambda i: (i, 0))
    return pl.pallas_call(
        _mla_prep_kernel,
        out_shape=(jax.ShapeDtypeStruct((t, 2 * hd), BF16),
                   jax.ShapeDtypeStruct((t, 2 * hd), BF16),
                   jax.ShapeDtypeStruct((t, hd), BF16)),
        grid=(t // tm,),
        in_specs=[pl.BlockSpec((tm, rank), lambda i: (i, ql_blk)),
                  pl.BlockSpec((tm, rank), lambda i: (i, ckv_blk)),
                  row(LANES), row(LANES), row(LANES),
                  full((1, rank)), full((1, rank)),
                  full((rank, hd)), full((rank, hd)), full((rank, hd)), full((rank, hd))],
        out_specs=(row(2 * hd), row(2 * hd), row(hd)),
        compiler_params=_cparams("arbitrary"),
        name="mla_prep",
    )(proj, proj, kr_raw, cm, sm, gq, gkv, wqn, wqr, wk, wv)


def _attn_kernel(q_ref, k_ref, v_ref, o_ref, m_ref, l_ref, acc_ref, *, tk):
    q = q_ref[...]
    n_kv = k_ref.shape[0] // tk
    m_ref[...] = jnp.full(m_ref.shape, -jnp.inf, F32)
    l_ref[...] = jnp.zeros(l_ref.shape, F32)
    acc_ref[...] = jnp.zeros(acc_ref.shape, F32)

    def body(j, carry):
        start = pl.multiple_of(j * tk, tk)
        kj = k_ref[pl.ds(start, tk), :]
        vj = v_ref[pl.ds(start, tk), :]
        s = lax.dot_general(q, kj, (((1,), (1,)), ((), ())), preferred_element_type=F32)
        m_prev = m_ref[...]
        m_next = jnp.maximum(m_prev, jnp.max(s, axis=-1, keepdims=True))
        p = jnp.exp(s - m_next[:, :1])
        alpha = jnp.exp(m_prev - m_next)
        l_ref[...] = alpha * l_ref[...] + jnp.sum(p, axis=-1, keepdims=True)
        acc_ref[...] = alpha * acc_ref[...] + jnp.dot(p.astype(BF16), vj, preferred_element_type=F32)
        m_ref[...] = m_next
        return carry

    lax.fori_loop(0, n_kv, body, 0)
    o_ref[...] = (acc_ref[...] / l_ref[...]).astype(o_ref.dtype)


def _attention(q, k, v, batch, seq, tq, tk):
    t = q.shape[0]
    nq = seq // tq
    return pl.pallas_call(
        functools.partial(_attn_kernel, tk=tk),
        out_shape=jax.ShapeDtypeStruct((t, MLA_HEADS * MLA_V), BF16),
        grid=(batch, MLA_HEADS, nq),
        in_specs=[pl.BlockSpec((tq, 2 * LANES), lambda b, h, i: (b * nq + i, h)),
                  pl.BlockSpec((seq, 2 * LANES), lambda b, h, i: (b, h)),
                  pl.BlockSpec((seq, MLA_V), lambda b, h, i: (b, h))],
        out_specs=pl.BlockSpec((tq, MLA_V), lambda b, h, i: (b * nq + i, h)),
        scratch_shapes=[pltpu.VMEM((tq, LANES), F32), pltpu.VMEM((tq, LANES), F32),
                        pltpu.VMEM((tq, MLA_V), F32)],
        compiler_params=_cparams("arbitrary", "arbitrary", "arbitrary"),
        name="mla_attention",
    )(q, k, v)


def _ret_kernel(rq_ref, rk_ref, rv_ref, rg_ref, cos_ref, sin_ref, df_ref, db_ref, gn_ref,
                o_ref, krot_ref, sf_ref, st_ref, dmat_ref, xif_ref, zf_ref, xib_ref, zb_ref, *, ch):
    seq, dk = rq_ref.shape
    half = dk // 2
    n_ch = seq // ch
    lgf = jnp.log(jax.nn.sigmoid(df_ref[...]))[:1, :1]
    lgb = jnp.log(jax.nn.sigmoid(db_ref[...]))[:1, :1]

    ri = lax.broadcasted_iota(jnp.int32, (ch, ch), 0)
    ci = lax.broadcasted_iota(jnp.int32, (ch, ch), 1)
    diff = (ri - ci).astype(F32)
    dmat_ref[...] = jnp.where(diff >= 0, jnp.exp(lgf * jnp.maximum(diff, 0.0)),
                              jnp.exp(lgb * jnp.maximum(-diff, 0.0)))
    pos = lax.broadcasted_iota(jnp.int32, (ch, dk), 0).astype(F32)
    xif_ref[...] = jnp.exp(lgf * (pos + 1.0))
    zf_ref[...] = jnp.exp(lgf * (ch - 1.0 - pos))
    xib_ref[...] = jnp.exp(lgb * (ch - pos))
    zb_ref[...] = jnp.exp(lgb * pos)
    gcf = jnp.exp(lgf * float(ch))
    gcb = jnp.exp(lgb * float(ch))
    k_scale = dk ** -0.5

    def rot(x, c, s):
        x1, x2 = x[:, :half], x[:, half:]
        return jnp.concatenate([x1 * c - x2 * s, x1 * s + x2 * c], axis=-1)

    def kv_outer(kz, v):
        return lax.dot_general(kz, v, (((0,), (0,)), ((), ())), preferred_element_type=F32)

    st_ref[...] = jnp.zeros(st_ref.shape, F32)

    def fwd(n, carry):
        r0 = pl.multiple_of(n * ch, ch)
        rows = pl.ds(r0, ch)
        k = rot(rk_ref[rows, :].astype(F32), cos_ref[rows, :], sin_ref[rows, :]) * k_scale
        krot_ref[rows, :] = k.astype(BF16)
        sf_ref[n] = st_ref[...].astype(BF16)
        kv = kv_outer((k * zf_ref[...]).astype(BF16), rv_ref[rows, :])
        st_ref[...] = st_ref[...] * gcf + kv
        return carry

    lax.fori_loop(0, n_ch, fwd, 0)

    st_ref[...] = jnp.zeros(st_ref.shape, F32)

    def bwd(i, carry):
        n = n_ch - 1 - i
        r0 = pl.multiple_of(n * ch, ch)
        rows = pl.ds(r0, ch)
        q = rot(rq_ref[rows, :].astype(F32), cos_ref[rows, :], sin_ref[rows, :])
        kb = krot_ref[rows, :]
        v = rv_ref[rows, :]
        s = lax.dot_general(q.astype(BF16), kb, (((1,), (1,)), ((), ())), preferred_element_type=F32)
        o = jnp.dot((s * dmat_ref[...]).astype(BF16), v, preferred_element_type=F32)
        o = o + jnp.dot((q * xif_ref[...]).astype(BF16), sf_ref[n], preferred_element_type=F32)
        o = o + jnp.dot((q * xib_ref[...]).astype(BF16), st_ref[...].astype(BF16),
                        preferred_element_type=F32)
        mu = jnp.mean(o, axis=-1, keepdims=True)
        oc = o - mu
        var = jnp.mean(oc * oc, axis=-1, keepdims=True)
        on = oc * lax.rsqrt(var + GN_EPS)
        g = rg_ref[rows, :].astype(F32)
        o_ref[rows, :] = (g * jax.nn.sigmoid(g) * (on * gn_ref[...])).astype(o_ref.dtype)
        kv = kv_outer((kb.astype(F32) * zb_ref[...]).astype(BF16), v)
        st_ref[...] = st_ref[...] * gcb + kv
        return carry

    lax.fori_loop(0, n_ch, bwd, 0)


def _retention(proj, cos_r, sin_r, dec_f, dec_b, gn, batch, seq, blk, ch):
    t = proj.shape[0]
    dk = 2 * LANES
    n_ch = seq // ch
    col = lambda base: pl.BlockSpec((seq, dk), lambda b, h: (b, base + h))
    tab = pl.BlockSpec((seq, LANES), lambda b, h: (b, 0))
    dec = pl.BlockSpec((None, 8, LANES), lambda b, h: (h, 0, 0))
    return pl.pallas_call(
        functools.partial(_ret_kernel, ch=ch),
        out_shape=jax.ShapeDtypeStruct((t, RET_HEADS * dk), BF16),
        grid=(batch, RET_HEADS),
        in_specs=[col(blk["rq"]), col(blk["rk"]), col(blk["rv"]), col(blk["rg"]),
                  tab, tab, dec, dec,
                  pl.BlockSpec((1, dk), lambda b, h: (0, h))],
        out_specs=pl.BlockSpec((seq, dk), lambda b, h: (b, h)),
        scratch_shapes=[pltpu.VMEM((seq, dk), BF16),
                        pltpu.VMEM((n_ch, dk, dk), BF16),
                        pltpu.VMEM((dk, dk), F32),
                        pltpu.VMEM((ch, ch), F32),
                        pltpu.VMEM((ch, dk), F32), pltpu.VMEM((ch, dk), F32),
                        pltpu.VMEM((ch, dk), F32), pltpu.VMEM((ch, dk), F32)],
        compiler_params=_cparams("arbitrary", "arbitrary"),
        name="retention",
    )(proj, proj, proj, proj, cos_r, sin_r, dec_f, dec_b, gn)


def _mixgate_kernel(oa_ref, ob_ref, ga_ref, gb_ref, wa_ref, wb_ref, m_ref):
    ya = jnp.dot(oa_ref[...], wa_ref[...], preferred_element_type=F32)
    yb = jnp.dot(ob_ref[...], wb_ref[...], preferred_element_type=F32)
    m = jax.nn.sigmoid(ga_ref[...].astype(F32)) * ya + jax.nn.sigmoid(gb_ref[...].astype(F32)) * yb
    m_ref[...] = m.astype(m_ref.dtype)


def _mix_gate(o_mla, o_ret, proj, wa, wb, ga_blk, gb_blk, tm):
    t = o_mla.shape[0]
    d = wa.shape[1]
    return pl.pallas_call(
        _mixgate_kernel,
        out_shape=jax.ShapeDtypeStruct((t, d), BF16),
        grid=(t // tm,),
        in_specs=[pl.BlockSpec((tm, o_mla.shape[1]), lambda i: (i, 0)),
                  pl.BlockSpec((tm, o_ret.shape[1]), lambda i: (i, 0)),
                  pl.BlockSpec((tm, d), lambda i: (i, ga_blk)),
                  pl.BlockSpec((tm, d), lambda i: (i, gb_blk)),
                  pl.BlockSpec(wa.shape, lambda i: (0, 0)),
                  pl.BlockSpec(wb.shape, lambda i: (0, 0))],
        out_specs=pl.BlockSpec((tm, d), lambda i: (i, 0)),
        compiler_params=_cparams("arbitrary"),
        name="mix_gate",
    )(o_mla, o_ret, proj, proj, wa, wb)


def _mixout_kernel(m_ref, x_ref, wo_ref, gpost_ref, gate_ref, gpre_ref, sc_ref, sh_ref,
                   wr_ref, br_ref, x1_ref, h2_ref, lg_ref):
    mix = jnp.dot(m_ref[...], wo_ref[...], preferred_element_type=F32)
    x1 = x_ref[...] + gate_ref[...] * _rms(mix, gpost_ref[...])
    x1_ref[...] = x1
    h2 = _rms(x1, gpre_ref[...]) * (1.0 + sc_ref[...]) + sh_ref[...]
    h2_ref[...] = h2
    lg_ref[...] = jnp.dot(h2, wr_ref[...], preferred_element_type=F32,
                          precision=lax.Precision.HIGHEST) + br_ref[...]


def _mix_out(m, x2, wo, gpost, gate_m, gpre, scale_f, shift_f, wr, br, seq, tm):
    t, d = x2.shape
    per_b = seq // tm
    row = pl.BlockSpec((tm, d), lambda i: (i, 0))
    vec = pl.BlockSpec((1, d), lambda i: (0, 0))
    mod = pl.BlockSpec((None, 1, d), lambda i: (i // per_b, 0, 0))
    return pl.pallas_call(
        _mixout_kernel,
        out_shape=(jax.ShapeDtypeStruct((t, d), F32), jax.ShapeDtypeStruct((t, d), F32),
                   jax.ShapeDtypeStruct((t, LANES), F32)),
        grid=(t // tm,),
        in_specs=[row, row, pl.BlockSpec((d, d), lambda i: (0, 0)), vec, mod, vec, mod, mod,
                  pl.BlockSpec((d, LANES), lambda i: (0, 0)),
                  pl.BlockSpec((1, LANES), lambda i: (0, 0))],
        out_specs=(row, row, pl.BlockSpec((tm, LANES), lambda i: (i, 0))),
        compiler_params=_cparams("arbitrary"),
        name="mix_out",
    )(m, x2, wo, gpost, gate_m, gpre, scale_f, shift_f, wr, br)


def _route_kernel(lg_ref, o_ref, cnt_ref, carry_ref):
    @pl.when(pl.program_id(0) == 0)
    def _():
        carry_ref[...] = jnp.zeros(carry_ref.shape, F32)

    tm = lg_ref.shape[0]
    lane = lax.broadcasted_iota(jnp.int32, (tm, LANES), 1).astype(F32)
    l = jnp.where(lane < N_EXPERTS, lg_ref[...], -jnp.inf)
    vals, idxs = [], []
    for _ in range(TOP_K):
        mx = jnp.max(l, axis=-1, keepdims=True)
        ik = jnp.min(jnp.where(l == mx, lane, float(LANES)), axis=-1, keepdims=True)
        vals.append(mx)
        idxs.append(ik)
        l = jnp.where(lane == ik, -jnp.inf, l)
    es = [jnp.exp(v - vals[0]) for v in vals]
    den = es[0] + es[1] + es[2] + es[3]
    onehot = jnp.zeros((tm, LANES), F32)
    for ik in idxs:
        onehot = onehot + jnp.where(lane == ik, 1.0, 0.0)
    ri = lax.broadcasted_iota(jnp.int32, (tm, tm), 0)
    ci = lax.broadcasted_iota(jnp.int32, (tm, tm), 1)
    tri = jnp.where(ci < ri, 1.0, 0.0).astype(BF16)
    cum = jnp.dot(tri, onehot.astype(BF16), preferred_element_type=F32) + carry_ref[0:1, :]
    packed = jnp.zeros((tm, LANES), F32)
    for k in range(TOP_K):
        rank = jnp.sum(jnp.where(lane == idxs[k], cum, 0.0), axis=-1, keepdims=True)
        packed = packed + jnp.where(lane == float(k), idxs[k], 0.0)
        packed = packed + jnp.where(lane == float(TOP_K + k), rank, 0.0)
        packed = packed + jnp.where(lane == float(2 * TOP_K + k), es[k] / den, 0.0)
    o_ref[...] = packed
    carry_ref[...] = carry_ref[...] + jnp.sum(onehot, axis=0, keepdims=True)
    cnt_ref[...] = carry_ref[...]


def _route(logits, tm):
    t = logits.shape[0]
    return pl.pallas_call(
        _route_kernel,
        out_shape=(jax.ShapeDtypeStruct((t, LANES), F32), jax.ShapeDtypeStruct((8, LANES), F32)),
        grid=(t // tm,),
        in_specs=[pl.BlockSpec((tm, LANES), lambda i: (i, 0))],
        out_specs=(pl.BlockSpec((tm, LANES), lambda i: (i, 0)),
                   pl.BlockSpec((8, LANES), lambda i: (0, 0))),
        scratch_shapes=[pltpu.VMEM((8, LANES), F32)],
        compiler_params=_cparams("arbitrary"),
        name="route",
    )(logits)


def _dispatch_kernel(dest_ref, h_ref, xs_in_ref, xs_ref, sem):
    del xs_in_ref
    tm = h_ref.shape[0]
    base = pl.program_id(0) * (tm * TOP_K)

    def row_copy(t, d):
        return pltpu.make_async_copy(h_ref.at[pl.ds(t, 1), :], xs_ref.at[pl.ds(d, 1), :], sem)

    def issue(t, carry):
        for k in range(TOP_K):
            row_copy(t, dest_ref[base + t * TOP_K + k]).start()
        return carry

    lax.fori_loop(0, tm, issue, 0)

    def drain(t, carry):
        for k in range(TOP_K):
            row_copy(0, 0).wait()
        return carry

    lax.fori_loop(0, tm, drain, 0)


def _dispatch(dest, h2, xs_init, tm):
    t, d = h2.shape
    return pl.pallas_call(
        _dispatch_kernel,
        out_shape=jax.ShapeDtypeStruct(xs_init.shape, xs_init.dtype),
        grid_spec=pltpu.PrefetchScalarGridSpec(
            num_scalar_prefetch=1,
            grid=(t // tm,),
            in_specs=[pl.BlockSpec((tm, d), lambda i, dest: (i, 0)),
                      pl.BlockSpec(memory_space=pl.ANY)],
            out_specs=pl.BlockSpec(memory_space=pl.ANY),
            scratch_shapes=[pltpu.SemaphoreType.DMA],
        ),
        input_output_aliases={2: 0},
        compiler_params=_cparams("arbitrary"),
        name="dispatch",
    )(dest, h2, xs_init)


def _expert_kernel(be_ref, nu_ref, x_ref, wg_ref, wu_ref, bg_ref, bu_ref, wd_ref, bd_ref,
                   o_ref, xb_ref, acc_ref):
    del be_ref
    f = pl.program_id(1)
    nf = pl.num_programs(1)

    @pl.when(pl.program_id(0) < nu_ref[0])
    def _():
        @pl.when(f == 0)
        def _():
            xb_ref[...] = x_ref[...].astype(BF16)

        xb = xb_ref[...]
        g = jnp.dot(xb, wg_ref[...], preferred_element_type=F32) + bg_ref[...]
        u = jnp.dot(xb, wu_ref[...], preferred_element_type=F32) + bu_ref[...]
        g = jnp.minimum(g, SWIGLU_LIMIT)
        u = jnp.clip(u, -SWIGLU_LIMIT, SWIGLU_LIMIT)
        a = ((u + 1.0) * (g * jax.nn.sigmoid(SWIGLU_ALPHA * g))).astype(BF16)
        d = jnp.dot(a, wd_ref[...], preferred_element_type=F32)

        @pl.when(f == 0)
        def _():
            acc_ref[...] = d + bd_ref[...]

        @pl.when(f > 0)
        def _():
            acc_ref[...] += d

        @pl.when(f == nf - 1)
        def _():
            o_ref[...] = acc_ref[...]

    @pl.when(jnp.logical_and(pl.program_id(0) >= nu_ref[0], f == nf - 1))
    def _():
        o_ref[...] = jnp.zeros(o_ref.shape, o_ref.dtype)


def _experts(block_e, n_used, xs, w_gu, b_gu, w_dn, b_dn, tm, tf):
    p, d = xs.shape
    n_e, _, f2 = w_gu.shape
    fdim = f2 // 2
    nf = fdim // tf
    n_blocks = p // tm

    def live(i, nu):
        return i < nu[0]

    def x_map(i, f, be, nu):
        return (jnp.where(live(i, nu), i, nu[0] - 1), 0)

    def f_eff(i, f, nu):
        return jnp.where(live(i, nu), f, nf - 1)

    return pl.pallas_call(
        _expert_kernel,
        out_shape=jax.ShapeDtypeStruct((p, d), F32),
        grid_spec=pltpu.PrefetchScalarGridSpec(
            num_scalar_prefetch=2,
            grid=(n_blocks, nf),
            in_specs=[pl.BlockSpec((tm, d), x_map),
                      pl.BlockSpec((None, d, tf), lambda i, f, be, nu: (be[i], 0, f_eff(i, f, nu))),
                      pl.BlockSpec((None, d, tf), lambda i, f, be, nu: (be[i], 0, nf + f_eff(i, f, nu))),
                      pl.BlockSpec((None, 1, tf), lambda i, f, be, nu: (be[i], 0, f_eff(i, f, nu))),
                      pl.BlockSpec((None, 1, tf), lambda i, f, be, nu: (be[i], 0, nf + f_eff(i, f, nu))),
                      pl.BlockSpec((None, tf, d), lambda i, f, be, nu: (be[i], f_eff(i, f, nu), 0)),
                      pl.BlockSpec((None, 1, d), lambda i, f, be, nu: (be[i], 0, 0))],
            out_specs=pl.BlockSpec((tm, d), lambda i, f, be, nu: (i, 0)),
            scratch_shapes=[pltpu.VMEM((tm, d), BF16), pltpu.VMEM((tm, d), F32)],
        ),
        compiler_params=_cparams("arbitrary", "arbitrary"),
        name="experts",
    )(block_e, n_used, xs, w_gu, w_gu, b_gu, b_gu, w_dn, b_dn)


def _combine_kernel(dest_ref, ys_ref, pk_ref, x1_ref, gpost_ref, gate_ref, o_ref, buf_ref, sem):
    tm = x1_ref.shape[0]
    base = pl.program_id(0) * (tm * TOP_K)

    def row_copy(t, k, d):
        return pltpu.make_async_copy(ys_ref.at[pl.ds(d, 1), :], buf_ref.at[k, pl.ds(t, 1), :], sem)

    def issue(t, carry):
        for k in range(TOP_K):
            row_copy(t, k, dest_ref[base + t * TOP_K + k]).start()
        return carry

    lax.fori_loop(0, tm, issue, 0)

    def drain(t, carry):
        for k in range(TOP_K):
            row_copy(0, k, 0).wait()
        return carry

    lax.fori_loop(0, tm, drain, 0)

    pk = pk_ref[...]
    f = jnp.zeros(x1_ref.shape, F32)
    for k in range(TOP_K):
        w = pk[:, 2 * TOP_K + k:2 * TOP_K + k + 1]
        f = f + buf_ref[k] * w
    o_ref[...] = x1_ref[...] + gate_ref[...] * _rms(f, gpost_ref[...])


def _combine(dest, ys, packed, x1, gpost, gate_f, seq, tm):
    t, d = x1.shape
    per_b = seq // tm
    return pl.pallas_call(
        _combine_kernel,
        out_shape=jax.ShapeDtypeStruct((t, d), F32),
        grid_spec=pltpu.PrefetchScalarGridSpec(
            num_scalar_prefetch=1,
            grid=(t // tm,),
            in_specs=[pl.BlockSpec(memory_space=pl.ANY),
                      pl.BlockSpec((tm, LANES), lambda i, dest: (i, 0)),
                      pl.BlockSpec((tm, d), lambda i, dest: (i, 0)),
                      pl.BlockSpec((1, d), lambda i, dest: (0, 0)),
                      pl.BlockSpec((None, 1, d), lambda i, dest: (i // per_b, 0, 0))],
            out_specs=pl.BlockSpec((tm, d), lambda i, dest: (i, 0)),
            scratch_shapes=[pltpu.VMEM((TOP_K, tm, d), F32), pltpu.SemaphoreType.DMA],
        ),
        compiler_params=_cparams("arbitrary"),
        name="combine",
    )(dest, ys, packed, x1, gpost, gate_f)


def _pad_heads(w, n_heads, width):
    k = w.shape[0]
    w = w.reshape(k, n_heads, width)
    return jnp.pad(w, ((0, 0), (0, 0), (0, LANES - width))).reshape(k, n_heads * LANES)


def _layer(x, c, positions, w_ada, b_ada, g_pre_mix, g_post_mix, g_pre_ffn, g_post_ffn,
           w_in, q_a_norm, w_q_b, kv_a_norm, w_kv_b, w_mla_o, ret_decay_fwd, ret_decay_bwd,
           ret_gn, w_ret_o, w_out, w_router, b_router, w_gate_up, b_gate_up, w_down, b_down):
    batch, seq, d = x.shape
    t = batch * seq
    q_rank = q_a_norm.shape[0]
    kv_rank = kv_a_norm.shape[0]
    hd = RET_HEADS * 2 * LANES
    x2 = x.reshape(t, d)

    c_pad = jnp.pad(c, ((0, 8 - batch), (0, 0)))
    ada = _ada(c_pad, w_ada, b_ada[None, :])[:batch]
    shift_m, scale_m, gate_m, shift_f, scale_f, gate_f = [
        ada[:, i * d:(i + 1) * d].reshape(batch, 1, d) for i in range(N_MOD)]

    o_q, o_kv = 0, q_rank
    o_r = q_rank + kv_rank + MLA_ROPE
    w_wide = w_in[:, o_r:]
    w_main = jnp.concatenate([w_wide, w_in[:, o_q:o_q + q_rank], w_in[:, o_kv:o_kv + kv_rank]],
                             axis=1).astype(BF16)
    w_kr = jnp.pad(w_in[:, o_kv + kv_rank:o_r], ((0, 0), (0, LANES - MLA_ROPE))).astype(BF16)
    n_wide = w_wide.shape[1]
    proj, kr_raw = _in_proj(x2, g_pre_mix[None, :], scale_m, shift_m, w_main, w_kr, seq,
                            tm=min(1024, seq), tn=1024)
    blk_w = 2 * LANES
    blk = {"rq": 0, "rk": hd // blk_w, "rv": 2 * hd // blk_w, "rg": 3 * hd // blk_w}
    ga_blk, gb_blk = (4 * hd) // d, (4 * hd + d) // d
    ql_blk, ckv_blk = n_wide // q_rank, (n_wide + q_rank) // kv_rank

    cos_r, sin_r, cm, sm = _rope_tables(positions.reshape(t, 1), tm=min(512, seq))

    wq = w_q_b.reshape(q_rank, MLA_HEADS, MLA_QK)
    wqn = wq[:, :, :MLA_NOPE].reshape(q_rank, MLA_HEADS * MLA_NOPE).astype(BF16)
    wqr = _pad_heads(wq[:, :, MLA_NOPE:].reshape(q_rank, MLA_HEADS * MLA_ROPE), MLA_HEADS,
                     MLA_ROPE).astype(BF16)
    wkv = w_kv_b.reshape(kv_rank, MLA_HEADS, MLA_NOPE + MLA_V)
    wk = wkv[:, :, :MLA_NOPE].reshape(kv_rank, MLA_HEADS * MLA_NOPE).astype(BF16)
    wv = wkv[:, :, MLA_NOPE:].reshape(kv_rank, MLA_HEADS * MLA_V).astype(BF16)
    q, k, v = _mla_prep(proj, kr_raw, cm, sm, q_a_norm[None, :], kv_a_norm[None, :],
                        wqn, wqr, wk, wv, ql_blk, ckv_blk, tm=min(512, seq))
    o_mla = _attention(q, k, v, batch, seq, tq=min(512, seq), tk=min(512, seq))

    dec_f = jnp.broadcast_to(ret_decay_fwd.astype(F32)[:, None, None], (RET_HEADS, 8, LANES))
    dec_b = jnp.broadcast_to(ret_decay_bwd.astype(F32)[:, None, None], (RET_HEADS, 8, LANES))
    o_ret = _retention(proj, cos_r, sin_r, dec_f, dec_b, ret_gn[None, :], batch, seq, blk,
                       ch=min(256, seq))

    m = _mix_gate(o_mla, o_ret, proj, w_mla_o.astype(BF16), w_ret_o.astype(BF16), ga_blk, gb_blk,
                  tm=min(512, seq))
    wr = jnp.pad(w_router, ((0, 0), (0, LANES - N_EXPERTS)))
    br = jnp.pad(b_router, (0, LANES - N_EXPERTS))[None, :]
    x1, h2, logits = _mix_out(m, x2, w_out.astype(BF16), g_post_mix[None, :], gate_m,
                              g_pre_ffn[None, :], scale_f, shift_f, wr, br, seq, tm=min(256, seq))

    packed, cnt = _route(logits, tm=min(256, seq))
    e_idx = packed[:, 0:TOP_K].astype(jnp.int32)
    rank = packed[:, TOP_K:2 * TOP_K].astype(jnp.int32)
    counts = cnt[0, :N_EXPERTS].astype(jnp.int32)
    a = t * TOP_K
    tm_e = min(512, a // N_EXPERTS)
    n_blocks = (a + N_EXPERTS * (tm_e - 1) + tm_e - 1) // tm_e
    padded = ((counts + tm_e - 1) // tm_e) * tm_e
    pend = jnp.cumsum(padded)
    pstart = pend - padded
    dest = (pstart[e_idx] + rank).reshape(a).astype(jnp.int32)
    n_used = (pend[-1] // tm_e).astype(jnp.int32).reshape(1)
    block_e = jnp.clip(jnp.searchsorted(pend, jnp.arange(n_blocks, dtype=jnp.int32) * tm_e,
                                        side="right"), 0, N_EXPERTS - 1).astype(jnp.int32)

    xs = _dispatch(dest, h2, jnp.zeros((n_blocks * tm_e, d), F32), tm=min(256, seq))
    fdim = w_down.shape[1]
    ys = _experts(block_e, n_used, xs, w_gate_up.astype(BF16), b_gate_up[:, None, :],
                  w_down.astype(BF16), b_down[:, None, :], tm=tm_e, tf=min(512, fdim))
    out = _combine(dest, ys, packed, x1, g_post_ffn[None, :], gate_f, seq, tm=min(256, seq))
    return out.reshape(batch, seq, d)


def kernel(x, c, positions, w_ada, b_ada, g_pre_mix, g_post_mix, g_pre_ffn, g_post_ffn, w_in,
           q_a_norm, w_q_b, kv_a_norm, w_kv_b, w_mla_o, ret_decay_fwd, ret_decay_bwd, ret_gn,
           w_ret_o, w_out, w_router, b_router, w_gate_up, b_gate_up, w_down, b_down):
    params = (w_ada, b_ada, g_pre_mix, g_post_mix, g_pre_ffn, g_post_ffn, w_in, q_a_norm, w_q_b,
              kv_a_norm, w_kv_b, w_mla_o, ret_decay_fwd, ret_decay_bwd, ret_gn, w_ret_o, w_out,
              w_router, b_router, w_gate_up, b_gate_up, w_down, b_down)
    for l in range(w_ada.shape[0]):
        x = _layer(x, c, positions, *[p[l] for p in params])
    return x
```

```python
import functools

import numpy as np
import jax
import jax.numpy as jnp
from jax import lax
from jax.experimental import pallas as pl
from jax.experimental.pallas import tpu as pltpu

F32 = jnp.float32
BF16 = jnp.bfloat16

MLA_HEADS = 8
MLA_NOPE = 128
MLA_ROPE = 64
MLA_V = 128
MLA_QK = MLA_NOPE + MLA_ROPE
RET_HEADS = 8
N_EXPERTS = 32
TOP_K = 4
SWIGLU_LIMIT = 7.0
SWIGLU_ALPHA = 1.702
ROPE_THETA = 10000.0
RMS_EPS = 1e-6
GN_EPS = 1e-6
LOG2E = 1.4426950408889634
N_MOD = 6

LANES = 128
VMEM_LIMIT = 56 * 1024 * 1024


def _cparams(*sem):
    return pltpu.CompilerParams(dimension_semantics=sem, vmem_limit_bytes=VMEM_LIMIT)


def _rms(x, gain):
    return x * lax.rsqrt(jnp.mean(x * x, axis=-1, keepdims=True) + RMS_EPS) * gain


def _ada_kernel(c_ref, w_ref, b_ref, o_ref):
    o_ref[...] = jnp.dot(c_ref[...], w_ref[...], preferred_element_type=F32,
                         precision=lax.Precision.HIGHEST) + b_ref[...]


def _ada(c_pad, w_ada, b_ada, tn=1024):
    m, d = c_pad.shape
    n = w_ada.shape[1]
    return pl.pallas_call(
        _ada_kernel,
        out_shape=jax.ShapeDtypeStruct((m, n), F32),
        grid=(n // tn,),
        in_specs=[pl.BlockSpec((m, d), lambda j: (0, 0)),
                  pl.BlockSpec((d, tn), lambda j: (0, j)),
                  pl.BlockSpec((1, tn), lambda j: (0, j))],
        out_specs=pl.BlockSpec((m, tn), lambda j: (0, j)),
        compiler_params=_cparams("arbitrary"),
        name="ada",
    )(c_pad, w_ada, b_ada)


def _inproj_kernel(x_ref, g_ref, sc_ref, sh_ref, w_ref, wkr_ref, o_ref, okr_ref, h_ref):
    @pl.when(pl.program_id(1) == 0)
    def _():
        h = _rms(x_ref[...], g_ref[...]) * (1.0 + sc_ref[...]) + sh_ref[...]
        hb = h.astype(BF16)
        h_ref[...] = hb
        okr_ref[...] = jnp.dot(hb, wkr_ref[...], preferred_element_type=F32)

    o_ref[...] = jnp.dot(h_ref[...], w_ref[...], preferred_element_type=F32).astype(o_ref.dtype)


def _in_proj(x2, g, scale, shift, w_main, w_kr, seq, tm, tn):
    t, d = x2.shape
    n = w_main.shape[1]
    per_b = seq // tm
    return pl.pallas_call(
        _inproj_kernel,
        out_shape=(jax.ShapeDtypeStruct((t, n), BF16), jax.ShapeDtypeStruct((t, LANES), F32)),
        grid=(t // tm, n // tn),
        in_specs=[pl.BlockSpec((tm, d), lambda i, j: (i, 0)),
                  pl.BlockSpec((1, d), lambda i, j: (0, 0)),
                  pl.BlockSpec((None, 1, d), lambda i, j: (i // per_b, 0, 0)),
                  pl.BlockSpec((None, 1, d), lambda i, j: (i // per_b, 0, 0)),
                  pl.BlockSpec((d, tn), lambda i, j: (0, j)),
                  pl.BlockSpec((d, LANES), lambda i, j: (0, 0))],
        out_specs=(pl.BlockSpec((tm, tn), lambda i, j: (i, j)),
                   pl.BlockSpec((tm, LANES), lambda i, j: (i, 0))),
        scratch_shapes=[pltpu.VMEM((tm, d), BF16)],
        compiler_params=_cparams("arbitrary", "arbitrary"),
        name="in_proj",
    )(x2, g, scale, shift, w_main, w_kr)


def _rope_kernel(pos_ref, invr_ref, invm_ref, cr_ref, sr_ref, cm_ref, sm_ref):
    pos = pos_ref[...].astype(F32)
    ang_r = pos * invr_ref[...]
    cr_ref[...] = jnp.cos(ang_r)
    sr_ref[...] = jnp.sin(ang_r)
    ang_m = pos * invm_ref[...]
    lane = lax.broadcasted_iota(jnp.int32, ang_m.shape, 1)
    half = MLA_ROPE // 2
    cm_ref[...] = jnp.where(lane < MLA_ROPE, jnp.cos(ang_m), 0.0)
    s = jnp.sin(ang_m)
    sm_ref[...] = jnp.where(lane < half, -s, jnp.where(lane < MLA_ROPE, s, 0.0))


def _rope_tables(pos_col, tm):
    t = pos_col.shape[0]
    half_r = LANES
    inv_r = (1.0 / (np.float32(ROPE_THETA) ** (np.arange(half_r, dtype=np.float32) / np.float32(half_r)))
             ).astype(np.float32)
    half_m = MLA_ROPE // 2
    inv_m32 = (1.0 / (np.float32(ROPE_THETA) ** (np.arange(half_m, dtype=np.float32) / np.float32(half_m)))
               ).astype(np.float32)
    inv_m = np.zeros((LANES,), np.float32)
    inv_m[:half_m] = inv_m32
    inv_m[half_m:2 * half_m] = inv_m32
    tab = jax.ShapeDtypeStruct((t, LANES), F32)
    row = pl.BlockSpec((tm, LANES), lambda i: (i, 0))
    return pl.pallas_call(
        _rope_kernel,
        out_shape=(tab, tab, tab, tab),
        grid=(t // tm,),
        in_specs=[pl.BlockSpec((tm, 1), lambda i: (i, 0)),
                  pl.BlockSpec((1, LANES), lambda i: (0, 0)),
                  pl.BlockSpec((1, LANES), lambda i: (0, 0))],
        out_specs=(row, row, row, row),
        compiler_params=_cparams("arbitrary"),
        name="rope_tables",
    )(pos_col, jnp.asarray(inv_r)[None, :], jnp.asarray(inv_m)[None, :])


def _rot64(x, c, s):
    return x * c + (pltpu.roll(x, 32, 1) + pltpu.roll(x, 96, 1)) * s


def _mla_prep_kernel(ql_ref, ckv_ref, kr_ref, cm_ref, sm_ref, gq_ref, gkv_ref,
                     wqn_ref, wqr_ref, wk_ref, wv_ref, q_ref, k_ref, v_ref):
    c = cm_ref[...]
    s = sm_ref[...]
    qn = _rms(ql_ref[...].astype(F32), gq_ref[...]).astype(BF16)
    q_nope = jnp.dot(qn, wqn_ref[...], preferred_element_type=F32)
    q_rope = jnp.dot(qn, wqr_ref[...], preferred_element_type=F32)
    cn = _rms(ckv_ref[...].astype(F32), gkv_ref[...]).astype(BF16)
    k_nope = jnp.dot(cn, wk_ref[...], preferred_element_type=F32)
    v = jnp.dot(cn, wv_ref[...], preferred_element_type=F32)
    k_rot_t = _rot64(kr_ref[...], c, s).T.astype(BF16)
    lane = lax.broadcasted_iota(jnp.int32, c.shape, 1)
    ones_col = jnp.where(lane == 0, 1.0, 0.0).astype(BF16)
    scale = MLA_QK ** -0.5 * LOG2E
    for h in range(MLA_HEADS):
        lo, hi = h * LANES, (h + 1) * LANES
        q_ref[:, 2 * lo:2 * lo + LANES] = (q_nope[:, lo:hi] * scale).astype(BF16)
        q_ref[:, 2 * lo + LANES:2 * hi] = (_rot64(q_rope[:, lo:hi], c, s) * scale).astype(BF16)
        k_ref[2 * lo:2 * lo + LANES, :] = k_nope[:, lo:hi].T.astype(BF16)
        k_ref[2 * lo + LANES:2 * hi, :] = k_rot_t
        v_ref[:, 2 * lo:2 * lo + LANES] = v[:, lo:hi].astype(BF16)
        v_ref[:, 2 * lo + LANES:2 * hi] = ones_col


def _mla_prep(proj, kr_raw, cm, sm, gq, gkv, wqn, wqr, wk, wv, ql_blk, ckv_blk, seq, tm):
    t = proj.shape[0]
    rank = wqn.shape[0]
    hd = MLA_HEADS * LANES
    per_b = seq // tm
    full = lambda shape: pl.BlockSpec(shape, lambda i: (0, 0))
    row = lambda w: pl.BlockSpec((tm, w), lambda i: (i, 0))
    return pl.pallas_call(
        _mla_prep_kernel,
        out_shape=(jax.ShapeDtypeStruct((t, 2 * hd), BF16),
                   jax.ShapeDtypeStruct(((t // seq) * 2 * hd, seq), BF16),
                   jax.ShapeDtypeStruct((t, 2 * hd), BF16)),
        grid=(t // tm,),
        in_specs=[pl.BlockSpec((tm, rank), lambda i: (i, ql_blk)),
                  pl.BlockSpec((tm, rank), lambda i: (i, ckv_blk)),
                  row(LANES), row(LANES), row(LANES),
                  full((1, rank)), full((1, rank)),
                  full((rank, hd)), full((rank, hd)), full((rank, hd)), full((rank, hd))],
        out_specs=(row(2 * hd),
                   pl.BlockSpec((2 * hd, tm), lambda i: (i // per_b, i % per_b)),
                   row(2 * hd)),
        compiler_params=_cparams("arbitrary"),
        name="mla_prep",
    )(proj, proj, kr_raw, cm, sm, gq, gkv, wqn, wqr, wk, wv)


def _attn_kernel(q_ref, kt_ref, v_ref, o_ref, s_ref, *, rq, tk):
    tq = q_ref.shape[0]
    seq = kt_ref.shape[1]
    for r0 in range(0, tq, rq):
        q = q_ref[r0:r0 + rq, :]
        m_lane = None
        for c0 in range(0, seq, tk):
            s = jnp.dot(q, kt_ref[:, c0:c0 + tk], preferred_element_type=F32)
            s_ref[r0:r0 + rq, c0:c0 + tk] = s
            for l0 in range(0, tk, LANES):
                blk = s[:, l0:l0 + LANES]
                m_lane = blk if m_lane is None else jnp.maximum(m_lane, blk)
        m = jnp.max(m_lane, axis=-1, keepdims=True)
        acc = jnp.zeros((rq, 2 * MLA_V), F32)
        for c0 in range(0, seq, tk):
            p = jnp.exp2(s_ref[r0:r0 + rq, c0:c0 + tk] - m).astype(BF16)
            acc = acc + jnp.dot(p, v_ref[c0:c0 + tk, :], preferred_element_type=F32)
        o_ref[r0:r0 + rq, :] = (acc[:, :MLA_V] / acc[:, MLA_V:MLA_V + 1]).astype(o_ref.dtype)


def _attention(q, kt, v, batch, seq, tq, rq, tk):
    t = q.shape[0]
    nq = seq // tq
    return pl.pallas_call(
        functools.partial(_attn_kernel, rq=rq, tk=tk),
        out_shape=jax.ShapeDtypeStruct((t, MLA_HEADS * MLA_V), BF16),
        grid=(batch, MLA_HEADS, nq),
        in_specs=[pl.BlockSpec((tq, 2 * LANES), lambda b, h, i: (b * nq + i, h)),
                  pl.BlockSpec((2 * LANES, seq), lambda b, h, i: (b * MLA_HEADS + h, 0)),
                  pl.BlockSpec((seq, 2 * MLA_V), lambda b, h, i: (b, h))],
        out_specs=pl.BlockSpec((tq, MLA_V), lambda b, h, i: (b * nq + i, h)),
        scratch_shapes=[pltpu.VMEM((tq, seq), F32)],
        compiler_params=_cparams("arbitrary", "arbitrary", "arbitrary"),
        name="mla_attention",
    )(q, kt, v)


def _ret_kernel(rq_ref, rk_ref, rv_ref, rg_ref, cos_ref, sin_ref, df_ref, db_ref, gn_ref,
                o_ref, krot_ref, sf_ref, st_ref, dmat_ref, xif_ref, zf_ref, xib_ref, zb_ref, *, ch):
    seq, dk = rq_ref.shape
    half = dk // 2
    n_ch = seq // ch
    lgf = jnp.log(jax.nn.sigmoid(df_ref[...]))[:1, :1]
    lgb = jnp.log(jax.nn.sigmoid(db_ref[...]))[:1, :1]

    ri = lax.broadcasted_iota(jnp.int32, (ch, ch), 0)
    ci = lax.broadcasted_iota(jnp.int32, (ch, ch), 1)
    diff = (ri - ci).astype(F32)
    dmat_ref[...] = jnp.where(diff >= 0, jnp.exp(lgf * jnp.maximum(diff, 0.0)),
                              jnp.exp(lgb * jnp.maximum(-diff, 0.0)))
    pos = lax.broadcasted_iota(jnp.int32, (ch, dk), 0).astype(F32)
    xif_ref[...] = jnp.exp(lgf * (pos + 1.0))
    zf_ref[...] = jnp.exp(lgf * (ch - 1.0 - pos))
    xib_ref[...] = jnp.exp(lgb * (ch - pos))
    zb_ref[...] = jnp.exp(lgb * pos)
    gcf = jnp.exp(lgf * float(ch))
    gcb = jnp.exp(lgb * float(ch))
    k_scale = dk ** -0.5

    def rot(x, c, s):
        x1, x2 = x[:, :half], x[:, half:]
        return jnp.concatenate([x1 * c - x2 * s, x1 * s + x2 * c], axis=-1)

    def kv_outer(kz, v):
        return lax.dot_general(kz, v, (((0,), (0,)), ((), ())), preferred_element_type=F32)

    st_ref[...] = jnp.zeros(st_ref.shape, F32)

    def fwd(n, carry):
        r0 = pl.multiple_of(n * ch, ch)
        rows = pl.ds(r0, ch)
        k = rot(rk_ref[rows, :].astype(F32), cos_ref[rows, :], sin_ref[rows, :]) * k_scale
        krot_ref[rows, :] = k.astype(BF16)
        sf_ref[n] = st_ref[...].astype(BF16)
        kv = kv_outer((k * zf_ref[...]).astype(BF16), rv_ref[rows, :])
        st_ref[...] = st_ref[...] * gcf + kv
        return carry

    lax.fori_loop(0, n_ch, fwd, 0)

    st_ref[...] = jnp.zeros(st_ref.shape, F32)

    def bwd(i, carry):
        n = n_ch - 1 - i
        r0 = pl.multiple_of(n * ch, ch)
        rows = pl.ds(r0, ch)
        q = rot(rq_ref[rows, :].astype(F32), cos_ref[rows, :], sin_ref[rows, :])
        kb = krot_ref[rows, :]
        v = rv_ref[rows, :]
        s = lax.dot_general(q.astype(BF16), kb, (((1,), (1,)), ((), ())), preferred_element_type=F32)
        o = jnp.dot((s * dmat_ref[...]).astype(BF16), v, preferred_element_type=F32)
        o = o + jnp.dot((q * xif_ref[...]).astype(BF16), sf_ref[n], preferred_element_type=F32)
        o = o + jnp.dot((q * xib_ref[...]).astype(BF16), st_ref[...].astype(BF16),
                        preferred_element_type=F32)
        mu = jnp.mean(o, axis=-1, keepdims=True)
        oc = o - mu
        var = jnp.mean(oc * oc, axis=-1, keepdims=True)
        on = oc * lax.rsqrt(var + GN_EPS)
        g = rg_ref[rows, :].astype(F32)
        o_ref[rows, :] = (g * jax.nn.sigmoid(g) * (on * gn_ref[...])).astype(o_ref.dtype)
        kv = kv_outer((kb.astype(F32) * zb_ref[...]).astype(BF16), v)
        st_ref[...] = st_ref[...] * gcb + kv
        return carry

    lax.fori_loop(0, n_ch, bwd, 0)


def _retention(proj, cos_r, sin_r, dec_f, dec_b, gn, batch, seq, blk, ch):
    t = proj.shape[0]
    dk = 2 * LANES
    n_ch = seq // ch
    col = lambda base: pl.BlockSpec((seq, dk), lambda b, h: (b, base + h))
    tab = pl.BlockSpec((seq, LANES), lambda b, h: (b, 0))
    dec = pl.BlockSpec((None, 8, LANES), lambda b, h: (h, 0, 0))
    return pl.pallas_call(
        functools.partial(_ret_kernel, ch=ch),
        out_shape=jax.ShapeDtypeStruct((t, RET_HEADS * dk), BF16),
        grid=(batch, RET_HEADS),
        in_specs=[col(blk["rq"]), col(blk["rk"]), col(blk["rv"]), col(blk["rg"]),
                  tab, tab, dec, dec,
                  pl.BlockSpec((1, dk), lambda b, h: (0, h))],
        out_specs=pl.BlockSpec((seq, dk), lambda b, h: (b, h)),
        scratch_shapes=[pltpu.VMEM((seq, dk), BF16),
                        pltpu.VMEM((n_ch, dk, dk), BF16),
                        pltpu.VMEM((dk, dk), F32),
                        pltpu.VMEM((ch, ch), F32),
                        pltpu.VMEM((ch, dk), F32), pltpu.VMEM((ch, dk), F32),
                        pltpu.VMEM((ch, dk), F32), pltpu.VMEM((ch, dk), F32)],
        compiler_params=_cparams("arbitrary", "arbitrary"),
        name="retention",
    )(proj, proj, proj, proj, cos_r, sin_r, dec_f, dec_b, gn)


def _mixgate_kernel(oa_ref, ob_ref, ga_ref, gb_ref, wa_ref, wb_ref, m_ref):
    ya = jnp.dot(oa_ref[...], wa_ref[...], preferred_element_type=F32)
    yb = jnp.dot(ob_ref[...], wb_ref[...], preferred_element_type=F32)
    m = jax.nn.sigmoid(ga_ref[...].astype(F32)) * ya + jax.nn.sigmoid(gb_ref[...].astype(F32)) * yb
    m_ref[...] = m.astype(m_ref.dtype)


def _mix_gate(o_mla, o_ret, proj, wa, wb, ga_blk, gb_blk, tm):
    t = o_mla.shape[0]
    d = wa.shape[1]
    return pl.pallas_call(
        _mixgate_kernel,
        out_shape=jax.ShapeDtypeStruct((t, d), BF16),
        grid=(t // tm,),
        in_specs=[pl.BlockSpec((tm, o_mla.shape[1]), lambda i: (i, 0)),
                  pl.BlockSpec((tm, o_ret.shape[1]), lambda i: (i, 0)),
                  pl.BlockSpec((tm, d), lambda i: (i, ga_blk)),
                  pl.BlockSpec((tm, d), lambda i: (i, gb_blk)),
                  pl.BlockSpec(wa.shape, lambda i: (0, 0)),
                  pl.BlockSpec(wb.shape, lambda i: (0, 0))],
        out_specs=pl.BlockSpec((tm, d), lambda i: (i, 0)),
        compiler_params=_cparams("arbitrary"),
        name="mix_gate",
    )(o_mla, o_ret, proj, proj, wa, wb)


def _mixout_kernel(m_ref, x_ref, wo_ref, gpost_ref, gate_ref, gpre_ref, sc_ref, sh_ref,
                   wr_ref, br_ref, x1_ref, h2_ref, lg_ref):
    mix = jnp.dot(m_ref[...], wo_ref[...], preferred_element_type=F32)
    x1 = x_ref[...] + gate_ref[...] * _rms(mix, gpost_ref[...])
    x1_ref[...] = x1
    h2 = _rms(x1, gpre_ref[...]) * (1.0 + sc_ref[...]) + sh_ref[...]
    h2_ref[...] = h2
    h_hi = h2.astype(BF16)
    h_lo = (h2 - h_hi.astype(F32)).astype(BF16)
    wr = wr_ref[...]
    hh = jnp.dot(h_hi, wr, preferred_element_type=F32)
    lh = jnp.dot(h_lo, wr[:, :LANES], preferred_element_type=F32)
    lg_ref[...] = hh[:, :LANES] + (hh[:, LANES:] + lh) + br_ref[...]


def _mix_out(m, x2, wo, gpost, gate_m, gpre, scale_f, shift_f, wr, br, seq, tm):
    t, d = x2.shape
    per_b = seq // tm
    row = pl.BlockSpec((tm, d), lambda i: (i, 0))
    vec = pl.BlockSpec((1, d), lambda i: (0, 0))
    mod = pl.BlockSpec((None, 1, d), lambda i: (i // per_b, 0, 0))
    return pl.pallas_call(
        _mixout_kernel,
        out_shape=(jax.ShapeDtypeStruct((t, d), F32), jax.ShapeDtypeStruct((t, d), F32),
                   jax.ShapeDtypeStruct((t, LANES), F32)),
        grid=(t // tm,),
        in_specs=[row, row, pl.BlockSpec((d, d), lambda i: (0, 0)), vec, mod, vec, mod, mod,
                  pl.BlockSpec((d, 2 * LANES), lambda i: (0, 0)),
                  pl.BlockSpec((1, LANES), lambda i: (0, 0))],
        out_specs=(row, row, pl.BlockSpec((tm, LANES), lambda i: (i, 0))),
        compiler_params=_cparams("arbitrary"),
        name="mix_out",
    )(m, x2, wo, gpost, gate_m, gpre, scale_f, shift_f, wr, br)


def _route_kernel(lg_ref, o_ref, cnt_ref, carry_ref):
    @pl.when(pl.program_id(0) == 0)
    def _():
        carry_ref[...] = jnp.zeros(carry_ref.shape, F32)

    tm = lg_ref.shape[0]
    lane = lax.broadcasted_iota(jnp.int32, (tm, LANES), 1).astype(F32)
    l = jnp.where(lane < N_EXPERTS, lg_ref[...], -jnp.inf)
    vals, idxs = [], []
    for _ in range(TOP_K):
        mx = jnp.max(l, axis=-1, keepdims=True)
        ik = jnp.min(jnp.where(l == mx, lane, float(LANES)), axis=-1, keepdims=True)
        vals.append(mx)
        idxs.append(ik)
        l = jnp.where(lane == ik, -jnp.inf, l)
    es = [jnp.exp(v - vals[0]) for v in vals]
    den = es[0] + es[1] + es[2] + es[3]
    onehot = jnp.zeros((tm, LANES), F32)
    for ik in idxs:
        onehot = onehot + jnp.where(lane == ik, 1.0, 0.0)
    ri = lax.broadcasted_iota(jnp.int32, (tm, tm), 0)
    ci = lax.broadcasted_iota(jnp.int32, (tm, tm), 1)
    tri = jnp.where(ci < ri, 1.0, 0.0).astype(BF16)
    cum = jnp.dot(tri, onehot.astype(BF16), preferred_element_type=F32) + carry_ref[0:1, :]
    packed = jnp.zeros((tm, LANES), F32)
    for k in range(TOP_K):
        rank = jnp.sum(jnp.where(lane == idxs[k], cum, 0.0), axis=-1, keepdims=True)
        packed = packed + jnp.where(lane == float(k), idxs[k], 0.0)
        packed = packed + jnp.where(lane == float(TOP_K + k), rank, 0.0)
        packed = packed + jnp.where(lane == float(2 * TOP_K + k), es[k] / den, 0.0)
    o_ref[...] = packed
    carry_ref[...] = carry_ref[...] + jnp.sum(onehot, axis=0, keepdims=True)
    cnt_ref[...] = carry_ref[...]


def _route(logits, tm):
    t = logits.shape[0]
    return pl.pallas_call(
        _route_kernel,
        out_shape=(jax.ShapeDtypeStruct((t, LANES), F32), jax.ShapeDtypeStruct((8, LANES), F32)),
        grid=(t // tm,),
        in_specs=[pl.BlockSpec((tm, LANES), lambda i: (i, 0))],
        out_specs=(pl.BlockSpec((tm, LANES), lambda i: (i, 0)),
                   pl.BlockSpec((8, LANES), lambda i: (0, 0))),
        scratch_shapes=[pltpu.VMEM((8, LANES), F32)],
        compiler_params=_cparams("arbitrary"),
        name="route",
    )(logits)


def _dispatch_kernel(dest_ref, h_ref, xs_in_ref, xs_ref, sem):
    del xs_in_ref
    tm = h_ref.shape[0]
    base = pl.program_id(0) * (tm * TOP_K)

    def row_copy(t, d):
        return pltpu.make_async_copy(h_ref.at[pl.ds(t, 1), :], xs_ref.at[pl.ds(d, 1), :], sem)

    def issue(t, carry):
        for k in range(TOP_K):
            row_copy(t, dest_ref[base + t * TOP_K + k]).start(priority=k % 2)
        return carry

    lax.fori_loop(0, tm, issue, 0)

    def drain(t, carry):
        for k in range(TOP_K):
            row_copy(0, 0).wait()
        return carry

    lax.fori_loop(0, tm, drain, 0)


def _dispatch(dest, h2, xs_init, tm):
    t, d = h2.shape
    return pl.pallas_call(
        _dispatch_kernel,
        out_shape=jax.ShapeDtypeStruct(xs_init.shape, xs_init.dtype),
        grid_spec=pltpu.PrefetchScalarGridSpec(
            num_scalar_prefetch=1,
            grid=(t // tm,),
            in_specs=[pl.BlockSpec((tm, d), lambda i, dest: (i, 0)),
                      pl.BlockSpec(memory_space=pl.ANY)],
            out_specs=pl.BlockSpec(memory_space=pl.ANY),
            scratch_shapes=[pltpu.SemaphoreType.DMA],
        ),
        input_output_aliases={2: 0},
        compiler_params=_cparams("arbitrary"),
        name="dispatch",
    )(dest, h2, xs_init)


def _expert_kernel(be_ref, nu_ref, x_ref, wg_ref, wu_ref, bg_ref, bu_ref, wd_ref, bd_ref,
                   o_ref, xb_ref):
    del be_ref

    @pl.when(pl.program_id(1) == 0)
    def _():
        xb_ref[...] = x_ref[...].astype(BF16)
        o_ref[...] = jnp.broadcast_to(bd_ref[...], o_ref.shape)

    @pl.when(pl.program_id(0) < nu_ref[0])
    def _():
        xb = xb_ref[...]
        g = jnp.dot(xb, wg_ref[...], preferred_element_type=F32) + bg_ref[...]
        u = jnp.dot(xb, wu_ref[...], preferred_element_type=F32) + bu_ref[...]
        g = jnp.minimum(g, SWIGLU_LIMIT)
        u = jnp.clip(u, -SWIGLU_LIMIT, SWIGLU_LIMIT)
        a = ((u + 1.0) * (g * jax.nn.sigmoid(SWIGLU_ALPHA * g))).astype(BF16)
        o_ref[...] += jnp.dot(a, wd_ref[...], preferred_element_type=F32)


def _experts(block_e, n_used, xs, w_gu, b_gu, w_dn, b_dn, tm, tf):
    p, d = xs.shape
    n_e, _, f2 = w_gu.shape
    fdim = f2 // 2
    nf = fdim // tf
    n_blocks = p // tm

    def live(i, nu):
        return i < nu[0]

    def x_map(i, f, be, nu):
        return (jnp.where(live(i, nu), i, nu[0] - 1), 0)

    def f_eff(i, f, nu):
        return jnp.where(live(i, nu), f, nf - 1)

    return pl.pallas_call(
        _expert_kernel,
        out_shape=jax.ShapeDtypeStruct((p, d), F32),
        grid_spec=pltpu.PrefetchScalarGridSpec(
            num_scalar_prefetch=2,
            grid=(n_blocks, nf),
            in_specs=[pl.BlockSpec((tm, d), x_map),
                      pl.BlockSpec((None, d, tf), lambda i, f, be, nu: (be[i], 0, f_eff(i, f, nu))),
                      pl.BlockSpec((None, d, tf), lambda i, f, be, nu: (be[i], 0, nf + f_eff(i, f, nu))),
                      pl.BlockSpec((None, 1, tf), lambda i, f, be, nu: (be[i], 0, f_eff(i, f, nu))),
                      pl.BlockSpec((None, 1, tf), lambda i, f, be, nu: (be[i], 0, nf + f_eff(i, f, nu))),
                      pl.BlockSpec((None, tf, d), lambda i, f, be, nu: (be[i], f_eff(i, f, nu), 0)),
                      pl.BlockSpec((None, 1, d), lambda i, f, be, nu: (be[i], 0, 0))],
            out_specs=pl.BlockSpec((tm, d), lambda i, f, be, nu: (i, 0)),
            scratch_shapes=[pltpu.VMEM((tm, d), BF16)],
        ),
        compiler_params=_cparams("arbitrary", "arbitrary"),
        name="experts",
    )(block_e, n_used, xs, w_gu, w_gu, b_gu, b_gu, w_dn, b_dn)


def _combine_kernel(dest_ref, ys_ref, pk_ref, x1_ref, gpost_ref, gate_ref, o_ref, buf_ref, sem):
    tm = x1_ref.shape[0]
    base = pl.program_id(0) * (tm * TOP_K)

    def row_copy(t, k, d):
        return pltpu.make_async_copy(ys_ref.at[pl.ds(d, 1), :], buf_ref.at[k, pl.ds(t, 1), :], sem)

    def issue(t, carry):
        for k in range(TOP_K):
            row_copy(t, k, dest_ref[base + t * TOP_K + k]).start(priority=k % 2)
        return carry

    lax.fori_loop(0, tm, issue, 0)

    def drain(t, carry):
        for k in range(TOP_K):
            row_copy(0, k, 0).wait()
        return carry

    lax.fori_loop(0, tm, drain, 0)

    pk = pk_ref[...]
    f = jnp.zeros(x1_ref.shape, F32)
    for k in range(TOP_K):
        w = pk[:, 2 * TOP_K + k:2 * TOP_K + k + 1]
        f = f + buf_ref[k] * w
    o_ref[...] = x1_ref[...] + gate_ref[...] * _rms(f, gpost_ref[...])


def _combine(dest, ys, packed, x1, gpost, gate_f, seq, tm):
    t, d = x1.shape
    per_b = seq // tm
    return pl.pallas_call(
        _combine_kernel,
        out_shape=jax.ShapeDtypeStruct((t, d), F32),
        grid_spec=pltpu.PrefetchScalarGridSpec(
            num_scalar_prefetch=1,
            grid=(t // tm,),
            in_specs=[pl.BlockSpec(memory_space=pl.ANY),
                      pl.BlockSpec((tm, LANES), lambda i, dest: (i, 0)),
                      pl.BlockSpec((tm, d), lambda i, dest: (i, 0)),
                      pl.BlockSpec((1, d), lambda i, dest: (0, 0)),
                      pl.BlockSpec((None, 1, d), lambda i, dest: (i // per_b, 0, 0))],
            out_specs=pl.BlockSpec((tm, d), lambda i, dest: (i, 0)),
            scratch_shapes=[pltpu.VMEM((TOP_K, tm, d), F32), pltpu.SemaphoreType.DMA],
        ),
        compiler_params=_cparams("arbitrary"),
        name="combine",
    )(dest, ys, packed, x1, gpost, gate_f)


def _pad_heads(w, n_heads, width):
    k = w.shape[0]
    w = w.reshape(k, n_heads, width)
    return jnp.pad(w, ((0, 0), (0, 0), (0, LANES - width))).reshape(k, n_heads * LANES)


def _layer(x, c, positions, w_ada, b_ada, g_pre_mix, g_post_mix, g_pre_ffn, g_post_ffn,
           w_in, q_a_norm, w_q_b, kv_a_norm, w_kv_b, w_mla_o, ret_decay_fwd, ret_decay_bwd,
           ret_gn, w_ret_o, w_out, w_router, b_router, w_gate_up, b_gate_up, w_down, b_down):
    batch, seq, d = x.shape
    t = batch * seq
    q_rank = q_a_norm.shape[0]
    kv_rank = kv_a_norm.shape[0]
    hd = RET_HEADS * 2 * LANES
    x2 = x.reshape(t, d)

    c_pad = jnp.pad(c, ((0, 8 - batch), (0, 0)))
    ada = _ada(c_pad, w_ada, b_ada[None, :])[:batch]
    shift_m, scale_m, gate_m, shift_f, scale_f, gate_f = [
        ada[:, i * d:(i + 1) * d].reshape(batch, 1, d) for i in range(N_MOD)]

    o_q, o_kv = 0, q_rank
    o_r = q_rank + kv_rank + MLA_ROPE
    w_wide = w_in[:, o_r:]
    w_main = jnp.concatenate([w_wide, w_in[:, o_q:o_q + q_rank], w_in[:, o_kv:o_kv + kv_rank]],
                             axis=1).astype(BF16)
    w_kr = jnp.pad(w_in[:, o_kv + kv_rank:o_r], ((0, 0), (0, LANES - MLA_ROPE))).astype(BF16)
    n_wide = w_wide.shape[1]
    proj, kr_raw = _in_proj(x2, g_pre_mix[None, :], scale_m, shift_m, w_main, w_kr, seq,
                            tm=min(1024, seq), tn=1024)
    blk_w = 2 * LANES
    blk = {"rq": 0, "rk": hd // blk_w, "rv": 2 * hd // blk_w, "rg": 3 * hd // blk_w}
    ga_blk, gb_blk = (4 * hd) // d, (4 * hd + d) // d
    ql_blk, ckv_blk = n_wide // q_rank, (n_wide + q_rank) // kv_rank

    cos_r, sin_r, cm, sm = _rope_tables(positions.reshape(t, 1), tm=min(512, seq))

    wq = w_q_b.reshape(q_rank, MLA_HEADS, MLA_QK)
    wqn = wq[:, :, :MLA_NOPE].reshape(q_rank, MLA_HEADS * MLA_NOPE).astype(BF16)
    wqr = _pad_heads(wq[:, :, MLA_NOPE:].reshape(q_rank, MLA_HEADS * MLA_ROPE), MLA_HEADS,
                     MLA_ROPE).astype(BF16)
    wkv = w_kv_b.reshape(kv_rank, MLA_HEADS, MLA_NOPE + MLA_V)
    wk = wkv[:, :, :MLA_NOPE].reshape(kv_rank, MLA_HEADS * MLA_NOPE).astype(BF16)
    wv = wkv[:, :, MLA_NOPE:].reshape(kv_rank, MLA_HEADS * MLA_V).astype(BF16)
    q, k, v = _mla_prep(proj, kr_raw, cm, sm, q_a_norm[None, :], kv_a_norm[None, :],
                        wqn, wqr, wk, wv, ql_blk, ckv_blk, seq, tm=min(512, seq))
    o_mla = _attention(q, k, v, batch, seq, tq=min(1024, seq), rq=128, tk=min(512, seq))

    dec_f = jnp.broadcast_to(ret_decay_fwd.astype(F32)[:, None, None], (RET_HEADS, 8, LANES))
    dec_b = jnp.broadcast_to(ret_decay_bwd.astype(F32)[:, None, None], (RET_HEADS, 8, LANES))
    o_ret = _retention(proj, cos_r, sin_r, dec_f, dec_b, ret_gn[None, :], batch, seq, blk,
                       ch=min(256, seq))

    m = _mix_gate(o_mla, o_ret, proj, w_mla_o.astype(BF16), w_ret_o.astype(BF16), ga_blk, gb_blk,
                  tm=min(512, seq))
    wr32 = jnp.pad(w_router, ((0, 0), (0, LANES - N_EXPERTS)))
    wr_hi = wr32.astype(BF16)
    wr = jnp.concatenate([wr_hi, (wr32 - wr_hi.astype(F32)).astype(BF16)], axis=1)
    br = jnp.pad(b_router, (0, LANES - N_EXPERTS))[None, :]
    x1, h2, logits = _mix_out(m, x2, w_out.astype(BF16), g_post_mix[None, :], gate_m,
                              g_pre_ffn[None, :], scale_f, shift_f, wr, br, seq, tm=min(256, seq))

    packed, cnt = _route(logits, tm=min(256, seq))
    e_idx = packed[:, 0:TOP_K].astype(jnp.int32)
    rank = packed[:, TOP_K:2 * TOP_K].astype(jnp.int32)
    counts = cnt[0, :N_EXPERTS].astype(jnp.int32)
    a = t * TOP_K
    tm_e = min(512, a // N_EXPERTS)
    n_blocks = (a + N_EXPERTS * (tm_e - 1) + tm_e - 1) // tm_e
    padded = ((counts + tm_e - 1) // tm_e) * tm_e
    pend = jnp.cumsum(padded)
    pstart = pend - padded
    dest = (pstart[e_idx] + rank).reshape(a).astype(jnp.int32)
    n_used = (pend[-1] // tm_e).astype(jnp.int32).reshape(1)
    blk_start = jnp.arange(n_blocks, dtype=jnp.int32) * tm_e
    block_e = jnp.minimum(jnp.sum((pend[None, :] <= blk_start[:, None]).astype(jnp.int32), axis=1),
                          N_EXPERTS - 1)

    xs = _dispatch(dest, h2, jnp.zeros((n_blocks * tm_e, d), F32), tm=min(256, seq))
    fdim = w_down.shape[1]
    ys = _experts(block_e, n_used, xs, w_gate_up.astype(BF16), b_gate_up[:, None, :],
                  w_down.astype(BF16), b_down[:, None, :], tm=tm_e, tf=min(512, fdim))
    out = _combine(dest, ys, packed, x1, g_post_ffn[None, :], gate_f, seq, tm=min(256, seq))
    return out.reshape(batch, seq, d)


def kernel(x, c, positions, w_ada, b_ada, g_pre_mix, g_post_mix, g_pre_ffn, g_post_ffn, w_in,
           q_a_norm, w_q_b, kv_a_norm, w_kv_b, w_mla_o, ret_decay_fwd, ret_decay_bwd, ret_gn,
           w_ret_o, w_out, w_router, b_router, w_gate_up, b_gate_up, w_down, b_down):
    params = (w_ada, b_ada, g_pre_mix, g_post_mix, g_pre_ffn, g_post_ffn, w_in, q_a_norm, w_q_b,
              kv_a_norm, w_kv_b, w_mla_o, ret_decay_fwd, ret_decay_bwd, ret_gn, w_ret_o, w_out,
              w_router, b_router, w_gate_up, b_gate_up, w_down, b_down)
    for l in range(w_ada.shape[0]):
        x = _layer(x, c, positions, *[p[l] for p in params])
    return x
```

```python
import functools

import numpy as np
import jax
import jax.numpy as jnp
from jax import lax
from jax.experimental import pallas as pl
from jax.experimental.pallas import tpu as pltpu

F32 = jnp.float32
BF16 = jnp.bfloat16

MLA_HEADS = 8
MLA_NOPE = 128
MLA_ROPE = 64
MLA_V = 128
MLA_QK = MLA_NOPE + MLA_ROPE
RET_HEADS = 8
N_EXPERTS = 32
TOP_K = 4
SWIGLU_LIMIT = 7.0
SWIGLU_ALPHA = 1.702
ROPE_THETA = 10000.0
RMS_EPS = 1e-6
GN_EPS = 1e-6
LOG2E = 1.4426950408889634
N_MOD = 6

LANES = 128
SUBLANES = 8
VMEM_LIMIT = 56 * 1024 * 1024
GROUP_SUB = 8


def _cparams(*sem):
    return pltpu.CompilerParams(dimension_semantics=sem, vmem_limit_bytes=VMEM_LIMIT)


def _rms(x, gain):
    return x * lax.rsqrt(jnp.mean(x * x, axis=-1, keepdims=True) + RMS_EPS) * gain


def _ada_kernel(c_ref, w_ref, b_ref, o_ref):
    o_ref[...] = jnp.dot(c_ref[...], w_ref[...], preferred_element_type=F32,
                         precision=lax.Precision.HIGHEST) + b_ref[...]


def _ada(c_pad, w_ada, b_ada, tn=1024):
    m, d = c_pad.shape
    n = w_ada.shape[1]
    return pl.pallas_call(
        _ada_kernel,
        out_shape=jax.ShapeDtypeStruct((m, n), F32),
        grid=(n // tn,),
        in_specs=[pl.BlockSpec((m, d), lambda j: (0, 0)),
                  pl.BlockSpec((d, tn), lambda j: (0, j)),
                  pl.BlockSpec((1, tn), lambda j: (0, j))],
        out_specs=pl.BlockSpec((m, tn), lambda j: (0, j)),
        compiler_params=_cparams("arbitrary"),
        name="ada",
    )(c_pad, w_ada, b_ada)


def _inproj_kernel(x_ref, g_ref, sc_ref, sh_ref, w_ref, wkr_ref, o_ref, okr_ref, h_ref):
    @pl.when(pl.program_id(1) == 0)
    def _():
        h = _rms(x_ref[...], g_ref[...]) * (1.0 + sc_ref[...]) + sh_ref[...]
        hb = h.astype(BF16)
        h_ref[...] = hb
        okr_ref[...] = jnp.dot(hb, wkr_ref[...], preferred_element_type=F32)

    o_ref[...] = jnp.dot(h_ref[...], w_ref[...], preferred_element_type=F32).astype(o_ref.dtype)


def _in_proj(x2, g, scale, shift, w_main, w_kr, seq, tm, tn):
    t, d = x2.shape
    n = w_main.shape[1]
    per_b = seq // tm
    return pl.pallas_call(
        _inproj_kernel,
        out_shape=(jax.ShapeDtypeStruct((t, n), BF16), jax.ShapeDtypeStruct((t, LANES), F32)),
        grid=(t // tm, n // tn),
        in_specs=[pl.BlockSpec((tm, d), lambda i, j: (i, 0)),
                  pl.BlockSpec((1, d), lambda i, j: (0, 0)),
                  pl.BlockSpec((None, 1, d), lambda i, j: (i // per_b, 0, 0)),
                  pl.BlockSpec((None, 1, d), lambda i, j: (i // per_b, 0, 0)),
                  pl.BlockSpec((d, tn), lambda i, j: (0, j)),
                  pl.BlockSpec((d, LANES), lambda i, j: (0, 0))],
        out_specs=(pl.BlockSpec((tm, tn), lambda i, j: (i, j)),
                   pl.BlockSpec((tm, LANES), lambda i, j: (i, 0))),
        scratch_shapes=[pltpu.VMEM((tm, d), BF16)],
        compiler_params=_cparams("arbitrary", "arbitrary"),
        name="in_proj",
    )(x2, g, scale, shift, w_main, w_kr)


def _rope_kernel(pos_ref, invr_ref, invm_ref, cr_ref, sr_ref, cm_ref, sm_ref):
    pos = pos_ref[...].astype(F32)
    ang_r = pos * invr_ref[...]
    cr_ref[...] = jnp.cos(ang_r)
    sr_ref[...] = jnp.sin(ang_r)
    ang_m = pos * invm_ref[...]
    lane = lax.broadcasted_iota(jnp.int32, ang_m.shape, 1)
    half = MLA_ROPE // 2
    cm_ref[...] = jnp.where(lane < MLA_ROPE, jnp.cos(ang_m), 0.0)
    s = jnp.sin(ang_m)
    sm_ref[...] = jnp.where(lane < half, -s, jnp.where(lane < MLA_ROPE, s, 0.0))


def _rope_tables(pos_col, tm):
    t = pos_col.shape[0]
    half_r = LANES
    inv_r = (1.0 / (np.float32(ROPE_THETA) ** (np.arange(half_r, dtype=np.float32) / np.float32(half_r)))
             ).astype(np.float32)
    half_m = MLA_ROPE // 2
    inv_m32 = (1.0 / (np.float32(ROPE_THETA) ** (np.arange(half_m, dtype=np.float32) / np.float32(half_m)))
               ).astype(np.float32)
    inv_m = np.zeros((LANES,), np.float32)
    inv_m[:half_m] = inv_m32
    inv_m[half_m:2 * half_m] = inv_m32
    tab = jax.ShapeDtypeStruct((t, LANES), F32)
    row = pl.BlockSpec((tm, LANES), lambda i: (i, 0))
    return pl.pallas_call(
        _rope_kernel,
        out_shape=(tab, tab, tab, tab),
        grid=(t // tm,),
        in_specs=[pl.BlockSpec((tm, 1), lambda i: (i, 0)),
                  pl.BlockSpec((1, LANES), lambda i: (0, 0)),
                  pl.BlockSpec((1, LANES), lambda i: (0, 0))],
        out_specs=(row, row, row, row),
        compiler_params=_cparams("arbitrary"),
        name="rope_tables",
    )(pos_col, jnp.asarray(inv_r)[None, :], jnp.asarray(inv_m)[None, :])


def _rot64(x, c, s):
    return x * c + (pltpu.roll(x, 32, 1) + pltpu.roll(x, 96, 1)) * s


def _mla_prep_kernel(ql_ref, ckv_ref, kr_ref, cm_ref, sm_ref, gq_ref, gkv_ref,
                     wqn_ref, wqr_ref, wk_ref, wv_ref, q_ref, k_ref, v_ref):
    c = cm_ref[...]
    s = sm_ref[...]
    qn = _rms(ql_ref[...].astype(F32), gq_ref[...]).astype(BF16)
    q_nope = jnp.dot(qn, wqn_ref[...], preferred_element_type=F32)
    q_rope = jnp.dot(qn, wqr_ref[...], preferred_element_type=F32)
    cn = _rms(ckv_ref[...].astype(F32), gkv_ref[...]).astype(BF16)
    k_nope = jnp.dot(cn, wk_ref[...], preferred_element_type=F32)
    v = jnp.dot(cn, wv_ref[...], preferred_element_type=F32)
    k_rot_t = _rot64(kr_ref[...], c, s).T.astype(BF16)
    lane = lax.broadcasted_iota(jnp.int32, c.shape, 1)
    ones_col = jnp.where(lane == 0, 1.0, 0.0).astype(BF16)
    scale = MLA_QK ** -0.5 * LOG2E
    for h in range(MLA_HEADS):
        lo, hi = h * LANES, (h + 1) * LANES
        q_ref[:, 2 * lo:2 * lo + LANES] = (q_nope[:, lo:hi] * scale).astype(BF16)
        q_ref[:, 2 * lo + LANES:2 * hi] = (_rot64(q_rope[:, lo:hi], c, s) * scale).astype(BF16)
        k_ref[2 * lo:2 * lo + LANES, :] = k_nope[:, lo:hi].T.astype(BF16)
        k_ref[2 * lo + LANES:2 * hi, :] = k_rot_t
        v_ref[:, 2 * lo:2 * lo + LANES] = v[:, lo:hi].astype(BF16)
        v_ref[:, 2 * lo + LANES:2 * hi] = ones_col


def _mla_prep(proj, kr_raw, cm, sm, gq, gkv, wqn, wqr, wk, wv, ql_blk, ckv_blk, seq, tm):
    t = proj.shape[0]
    rank = wqn.shape[0]
    hd = MLA_HEADS * LANES
    per_b = seq // tm
    full = lambda shape: pl.BlockSpec(shape, lambda i: (0, 0))
    row = lambda w: pl.BlockSpec((tm, w), lambda i: (i, 0))
    return pl.pallas_call(
        _mla_prep_kernel,
        out_shape=(jax.ShapeDtypeStruct((t, 2 * hd), BF16),
                   jax.ShapeDtypeStruct(((t // seq) * 2 * hd, seq), BF16),
                   jax.ShapeDtypeStruct((t, 2 * hd), BF16)),
        grid=(t // tm,),
        in_specs=[pl.BlockSpec((tm, rank), lambda i: (i, ql_blk)),
                  pl.BlockSpec((tm, rank), lambda i: (i, ckv_blk)),
                  row(LANES), row(LANES), row(LANES),
                  full((1, rank)), full((1, rank)),
                  full((rank, hd)), full((rank, hd)), full((rank, hd)), full((rank, hd))],
        out_specs=(row(2 * hd),
                   pl.BlockSpec((2 * hd, tm), lambda i: (i // per_b, i % per_b)),
                   row(2 * hd)),
        compiler_params=_cparams("arbitrary"),
        name="mla_prep",
    )(proj, proj, kr_raw, cm, sm, gq, gkv, wqn, wqr, wk, wv)


def _attn_kernel(q_ref, kt_ref, v_ref, o_ref, s_ref, *, rq, tk):
    tq = q_ref.shape[0]
    seq = kt_ref.shape[1]
    for r0 in range(0, tq, rq):
        q = q_ref[r0:r0 + rq, :]
        m_lane = None
        for c0 in range(0, seq, tk):
            s = jnp.dot(q, kt_ref[:, c0:c0 + tk], preferred_element_type=F32)
            s_ref[r0:r0 + rq, c0:c0 + tk] = s
            for l0 in range(0, tk, LANES):
                blk = s[:, l0:l0 + LANES]
                m_lane = blk if m_lane is None else jnp.maximum(m_lane, blk)
        m = jnp.max(m_lane, axis=-1, keepdims=True)
        acc = jnp.zeros((rq, 2 * MLA_V), F32)
        for c0 in range(0, seq, tk):
            p = jnp.exp2(s_ref[r0:r0 + rq, c0:c0 + tk] - m).astype(BF16)
            acc = acc + jnp.dot(p, v_ref[c0:c0 + tk, :], preferred_element_type=F32)
        o_ref[r0:r0 + rq, :] = (acc[:, :MLA_V] / acc[:, MLA_V:MLA_V + 1]).astype(o_ref.dtype)


def _attention(q, kt, v, batch, seq, tq, rq, tk):
    t = q.shape[0]
    nq = seq // tq
    return pl.pallas_call(
        functools.partial(_attn_kernel, rq=rq, tk=tk),
        out_shape=jax.ShapeDtypeStruct((t, MLA_HEADS * MLA_V), BF16),
        grid=(batch, MLA_HEADS, nq),
        in_specs=[pl.BlockSpec((tq, 2 * LANES), lambda b, h, i: (b * nq + i, h)),
                  pl.BlockSpec((2 * LANES, seq), lambda b, h, i: (b * MLA_HEADS + h, 0)),
                  pl.BlockSpec((seq, 2 * MLA_V), lambda b, h, i: (b, h))],
        out_specs=pl.BlockSpec((tq, MLA_V), lambda b, h, i: (b * nq + i, h)),
        scratch_shapes=[pltpu.VMEM((tq, seq), F32)],
        compiler_params=_cparams("arbitrary", "arbitrary", "arbitrary"),
        name="mla_attention",
    )(q, kt, v)


def _ret_kernel(rq_ref, rk_ref, rv_ref, rg_ref, cos_ref, sin_ref, df_ref, db_ref, gn_ref,
                o_ref, krot_ref, sf_ref, st_ref, dmat_ref, xif_ref, zf_ref, xib_ref, zb_ref, *, ch):
    seq, dk = rq_ref.shape
    half = dk // 2
    n_ch = seq // ch
    lgf = jnp.log(jax.nn.sigmoid(df_ref[...]))[:1, :1]
    lgb = jnp.log(jax.nn.sigmoid(db_ref[...]))[:1, :1]

    ri = lax.broadcasted_iota(jnp.int32, (ch, ch), 0)
    ci = lax.broadcasted_iota(jnp.int32, (ch, ch), 1)
    diff = (ri - ci).astype(F32)
    dmat_ref[...] = jnp.where(diff >= 0, jnp.exp(lgf * jnp.maximum(diff, 0.0)),
                              jnp.exp(lgb * jnp.maximum(-diff, 0.0)))
    pos = lax.broadcasted_iota(jnp.int32, (ch, dk), 0).astype(F32)
    xif_ref[...] = jnp.exp(lgf * (pos + 1.0))
    zf_ref[...] = jnp.exp(lgf * (ch - 1.0 - pos))
    xib_ref[...] = jnp.exp(lgb * (ch - pos))
    zb_ref[...] = jnp.exp(lgb * pos)
    gcf = jnp.exp(lgf * float(ch))
    gcb = jnp.exp(lgb * float(ch))
    k_scale = dk ** -0.5

    def rot(x, c, s):
        x1, x2 = x[:, :half], x[:, half:]
        return jnp.concatenate([x1 * c - x2 * s, x1 * s + x2 * c], axis=-1)

    def kv_outer(kz, v):
        return lax.dot_general(kz, v, (((0,), (0,)), ((), ())), preferred_element_type=F32)

    st_ref[...] = jnp.zeros(st_ref.shape, F32)

    def fwd(n, carry):
        r0 = pl.multiple_of(n * ch, ch)
        rows = pl.ds(r0, ch)
        k = rot(rk_ref[rows, :].astype(F32), cos_ref[rows, :], sin_ref[rows, :]) * k_scale
        krot_ref[rows, :] = k.astype(BF16)
        sf_ref[n] = st_ref[...].astype(BF16)
        kv = kv_outer((k * zf_ref[...]).astype(BF16), rv_ref[rows, :])
        st_ref[...] = st_ref[...] * gcf + kv
        return carry

    lax.fori_loop(0, n_ch, fwd, 0)

    st_ref[...] = jnp.zeros(st_ref.shape, F32)

    def bwd(i, carry):
        n = n_ch - 1 - i
        r0 = pl.multiple_of(n * ch, ch)
        rows = pl.ds(r0, ch)
        q = rot(rq_ref[rows, :].astype(F32), cos_ref[rows, :], sin_ref[rows, :])
        kb = krot_ref[rows, :]
        v = rv_ref[rows, :]
        s = lax.dot_general(q.astype(BF16), kb, (((1,), (1,)), ((), ())), preferred_element_type=F32)
        o = jnp.dot((s * dmat_ref[...]).astype(BF16), v, preferred_element_type=F32)
        o = o + jnp.dot((q * xif_ref[...]).astype(BF16), sf_ref[n], preferred_element_type=F32)
        o = o + jnp.dot((q * xib_ref[...]).astype(BF16), st_ref[...].astype(BF16),
                        preferred_element_type=F32)
        mu = jnp.mean(o, axis=-1, keepdims=True)
        oc = o - mu
        var = jnp.mean(oc * oc, axis=-1, keepdims=True)
        on = oc * lax.rsqrt(var + GN_EPS)
        g = rg_ref[rows, :].astype(F32)
        o_ref[rows, :] = (g * jax.nn.sigmoid(g) * (on * gn_ref[...])).astype(o_ref.dtype)
        kv = kv_outer((kb.astype(F32) * zb_ref[...]).astype(BF16), v)
        st_ref[...] = st_ref[...] * gcb + kv
        return carry

    lax.fori_loop(0, n_ch, bwd, 0)


def _retention(proj, cos_r, sin_r, dec_f, dec_b, gn, batch, seq, blk, ch):
    t = proj.shape[0]
    dk = 2 * LANES
    n_ch = seq // ch
    col = lambda base: pl.BlockSpec((seq, dk), lambda b, h: (b, base + h))
    tab = pl.BlockSpec((seq, LANES), lambda b, h: (b, 0))
    dec = pl.BlockSpec((None, 8, LANES), lambda b, h: (h, 0, 0))
    return pl.pallas_call(
        functools.partial(_ret_kernel, ch=ch),
        out_shape=jax.ShapeDtypeStruct((t, RET_HEADS * dk), BF16),
        grid=(batch, RET_HEADS),
        in_specs=[col(blk["rq"]), col(blk["rk"]), col(blk["rv"]), col(blk["rg"]),
                  tab, tab, dec, dec,
                  pl.BlockSpec((1, dk), lambda b, h: (0, h))],
        out_specs=pl.BlockSpec((seq, dk), lambda b, h: (b, h)),
        scratch_shapes=[pltpu.VMEM((seq, dk), BF16),
                        pltpu.VMEM((n_ch, dk, dk), BF16),
                        pltpu.VMEM((dk, dk), F32),
                        pltpu.VMEM((ch, ch), F32),
                        pltpu.VMEM((ch, dk), F32), pltpu.VMEM((ch, dk), F32),
                        pltpu.VMEM((ch, dk), F32), pltpu.VMEM((ch, dk), F32)],
        compiler_params=_cparams("arbitrary", "arbitrary"),
        name="retention",
    )(proj, proj, proj, proj, cos_r, sin_r, dec_f, dec_b, gn)


def _mixgate_kernel(oa_ref, ob_ref, ga_ref, gb_ref, wa_ref, wb_ref, m_ref):
    ya = jnp.dot(oa_ref[...], wa_ref[...], preferred_element_type=F32)
    yb = jnp.dot(ob_ref[...], wb_ref[...], preferred_element_type=F32)
    m = jax.nn.sigmoid(ga_ref[...].astype(F32)) * ya + jax.nn.sigmoid(gb_ref[...].astype(F32)) * yb
    m_ref[...] = m.astype(m_ref.dtype)


def _mix_gate(o_mla, o_ret, proj, wa, wb, ga_blk, gb_blk, tm):
    t = o_mla.shape[0]
    d = wa.shape[1]
    return pl.pallas_call(
        _mixgate_kernel,
        out_shape=jax.ShapeDtypeStruct((t, d), BF16),
        grid=(t // tm,),
        in_specs=[pl.BlockSpec((tm, o_mla.shape[1]), lambda i: (i, 0)),
                  pl.BlockSpec((tm, o_ret.shape[1]), lambda i: (i, 0)),
                  pl.BlockSpec((tm, d), lambda i: (i, ga_blk)),
                  pl.BlockSpec((tm, d), lambda i: (i, gb_blk)),
                  pl.BlockSpec(wa.shape, lambda i: (0, 0)),
                  pl.BlockSpec(wb.shape, lambda i: (0, 0))],
        out_specs=pl.BlockSpec((tm, d), lambda i: (i, 0)),
        compiler_params=_cparams("arbitrary"),
        name="mix_gate",
    )(o_mla, o_ret, proj, proj, wa, wb)


def _mixout_kernel(m_ref, x_ref, wo_ref, gpost_ref, gate_ref, gpre_ref, sc_ref, sh_ref,
                   wr_ref, br_ref, x1_ref, h2_ref, lg_ref):
    mix = jnp.dot(m_ref[...], wo_ref[...], preferred_element_type=F32)
    x1 = x_ref[...] + gate_ref[...] * _rms(mix, gpost_ref[...])
    x1_ref[...] = x1
    h2 = _rms(x1, gpre_ref[...]) * (1.0 + sc_ref[...]) + sh_ref[...]
    half = h2.shape[1] // 2
    bits = pltpu.bitcast(h2.astype(BF16).astype(F32), jnp.uint32)
    h2_ref[...] = (bits[:, :half] >> 16) | (bits[:, half:] & jnp.uint32(0xFFFF0000))
    h_hi = h2.astype(BF16)
    h_lo = (h2 - h_hi.astype(F32)).astype(BF16)
    wr = wr_ref[...]
    hh = jnp.dot(h_hi, wr, preferred_element_type=F32)
    lh = jnp.dot(h_lo, wr[:, :LANES], preferred_element_type=F32)
    lg_ref[...] = hh[:, :LANES] + (hh[:, LANES:] + lh) + br_ref[...]


def _mix_out(m, x2, wo, gpost, gate_m, gpre, scale_f, shift_f, wr, br, seq, tm):
    t, d = x2.shape
    per_b = seq // tm
    row = pl.BlockSpec((tm, d), lambda i: (i, 0))
    vec = pl.BlockSpec((1, d), lambda i: (0, 0))
    mod = pl.BlockSpec((None, 1, d), lambda i: (i // per_b, 0, 0))
    return pl.pallas_call(
        _mixout_kernel,
        out_shape=(jax.ShapeDtypeStruct((t, d), F32), jax.ShapeDtypeStruct((t, d // 2), jnp.uint32),
                   jax.ShapeDtypeStruct((t, LANES), F32)),
        grid=(t // tm,),
        in_specs=[row, row, pl.BlockSpec((d, d), lambda i: (0, 0)), vec, mod, vec, mod, mod,
                  pl.BlockSpec((d, 2 * LANES), lambda i: (0, 0)),
                  pl.BlockSpec((1, LANES), lambda i: (0, 0))],
        out_specs=(row, pl.BlockSpec((tm, d // 2), lambda i: (i, 0)),
                   pl.BlockSpec((tm, LANES), lambda i: (i, 0))),
        compiler_params=_cparams("arbitrary"),
        name="mix_out",
    )(m, x2, wo, gpost, gate_m, gpre, scale_f, shift_f, wr, br)


def _route_kernel(lg_ref, o_ref, cnt_ref, carry_ref):
    @pl.when(pl.program_id(0) == 0)
    def _():
        carry_ref[...] = jnp.zeros(carry_ref.shape, F32)

    tm = lg_ref.shape[0]
    lane = lax.broadcasted_iota(jnp.int32, (tm, LANES), 1).astype(F32)
    l = jnp.where(lane < N_EXPERTS, lg_ref[...], -jnp.inf)
    vals, idxs = [], []
    for _ in range(TOP_K):
        mx = jnp.max(l, axis=-1, keepdims=True)
        ik = jnp.min(jnp.where(l == mx, lane, float(LANES)), axis=-1, keepdims=True)
        vals.append(mx)
        idxs.append(ik)
        l = jnp.where(lane == ik, -jnp.inf, l)
    es = [jnp.exp(v - vals[0]) for v in vals]
    den = es[0] + es[1] + es[2] + es[3]
    onehot = jnp.zeros((tm, LANES), F32)
    for ik in idxs:
        onehot = onehot + jnp.where(lane == ik, 1.0, 0.0)
    ri = lax.broadcasted_iota(jnp.int32, (tm, tm), 0)
    ci = lax.broadcasted_iota(jnp.int32, (tm, tm), 1)
    tri = jnp.where(ci < ri, 1.0, 0.0).astype(BF16)
    cum = jnp.dot(tri, onehot.astype(BF16), preferred_element_type=F32) + carry_ref[0:1, :]
    packed = jnp.zeros((tm, LANES), F32)
    for k in range(TOP_K):
        rank = jnp.sum(jnp.where(lane == idxs[k], cum, 0.0), axis=-1, keepdims=True)
        packed = packed + jnp.where(lane == float(k), idxs[k], 0.0)
        packed = packed + jnp.where(lane == float(TOP_K + k), rank, 0.0)
        packed = packed + jnp.where(lane == float(2 * TOP_K + k), es[k] / den, 0.0)
    o_ref[...] = packed
    carry_ref[...] = carry_ref[...] + jnp.sum(onehot, axis=0, keepdims=True)
    cnt_ref[...] = carry_ref[...]


def _route(logits, tm):
    t = logits.shape[0]
    return pl.pallas_call(
        _route_kernel,
        out_shape=(jax.ShapeDtypeStruct((t, LANES), F32), jax.ShapeDtypeStruct((8, LANES), F32)),
        grid=(t // tm,),
        in_specs=[pl.BlockSpec((tm, LANES), lambda i: (i, 0))],
        out_specs=(pl.BlockSpec((tm, LANES), lambda i: (i, 0)),
                   pl.BlockSpec((8, LANES), lambda i: (0, 0))),
        scratch_shapes=[pltpu.VMEM((8, LANES), F32)],
        compiler_params=_cparams("arbitrary"),
        name="route",
    )(logits)


def _dispatch_kernel(dest_ref, pad_pos_ref, pad_len_ref, tail_ref, h_ref, xs_ref, z_ref, sem, zsem,
                     *, tm_e):
    tm = h_ref.shape[0]
    zr = z_ref.shape[0]
    base = pl.program_id(0) * (tm * TOP_K)
    pieces = [p for p in (2 ** i for i in range(20)) if SUBLANES <= p <= zr and p < tm_e][::-1]

    def zero_copy(pos, p):
        return pltpu.make_async_copy(z_ref.at[pl.ds(0, p), :], xs_ref.at[pl.ds(pos, p), :], zsem)

    def pad_rows(act):
        for e in range(N_EXPERTS):
            pos = pad_pos_ref[e]
            head = (-pos) & (SUBLANES - 1)
            for j in range(SUBLANES - 1):
                @pl.when(j < head)
                def _(pos=pos, j=j):
                    act(zero_copy(pos + j, 1))

            pos = pos + head
            ln = pad_len_ref[e] - head
            for p in pieces:
                hit = (ln & p) != 0

                @pl.when(hit)
                def _(pos=pos, p=p):
                    act(zero_copy(pl.multiple_of(pos, SUBLANES), p))

                pos = pos + jnp.where(hit, p, 0)

    def tail_rows(act):
        def body(j, carry):
            for r0 in range(0, tm_e, zr):
                act(zero_copy(pl.multiple_of(j * tm_e + r0, SUBLANES), zr))
            return carry

        lax.fori_loop(tail_ref[0], tail_ref[1], body, 0)

    @pl.when(pl.program_id(0) == 0)
    def _():
        z_ref[...] = jnp.zeros(z_ref.shape, z_ref.dtype)
        pad_rows(lambda c: c.start())
        tail_rows(lambda c: c.start())

    def row_copy(t, d):
        return pltpu.make_async_copy(h_ref.at[pl.ds(t, 1), :], xs_ref.at[pl.ds(d, 1), :], sem)

    def issue(t, carry):
        for k in range(TOP_K):
            row_copy(t, dest_ref[base + t * TOP_K + k]).start(priority=k % 2)
        return carry

    lax.fori_loop(0, tm, issue, 0, unroll=8)

    for k in range(TOP_K):
        pltpu.make_async_copy(h_ref, xs_ref.at[pl.ds(0, tm), :], sem).wait()

    @pl.when(pl.program_id(0) == 0)
    def _():
        pad_rows(lambda c: c.wait())
        tail_rows(lambda c: c.wait())


def _dispatch(dest, pad_pos, pad_len, tail, h2, n_rows, tm, tm_e):
    t, d = h2.shape
    zr = min(256, tm_e)
    nsp = 4
    return pl.pallas_call(
        functools.partial(_dispatch_kernel, tm_e=tm_e),
        out_shape=jax.ShapeDtypeStruct((n_rows, d), h2.dtype),
        grid_spec=pltpu.PrefetchScalarGridSpec(
            num_scalar_prefetch=nsp,
            grid=(t // tm,),
            in_specs=[pl.BlockSpec((tm, d), lambda i, *_: (i, 0))],
            out_specs=pl.BlockSpec(memory_space=pl.ANY),
            scratch_shapes=[pltpu.VMEM((zr, d), h2.dtype), pltpu.SemaphoreType.DMA,
                            pltpu.SemaphoreType.DMA],
        ),
        compiler_params=_cparams("arbitrary"),
        name="dispatch",
    )(dest, pad_pos, pad_len, tail, h2)


def _expert_kernel(ge_ref, gs_ref, gn_ref, tail_ref, xs_ref, wg_ref, wu_ref, bg_ref, bu_ref,
                   wd_ref, bd_ref, ys_ref, xu_ref, xb_ref, ab_ref, wgb_ref, wub_ref, wdb_ref,
                   ob_ref, op_ref, or_ref, z_ref, xsem, osem, psem, rsem, zsem, *, rs, nf, nd):
    del ge_ref
    g = pl.program_id(0)
    s = pl.program_id(1)
    n = gn_ref[g]
    row0 = gs_ref[g]
    half = xu_ref.shape[1]
    tf = wg_ref.shape[1]
    td = wd_ref.shape[1]

    def rows(i):
        return pl.ds(pl.multiple_of(i * rs, rs), rs)

    def x_copy(i):
        src = xs_ref.at[pl.ds(pl.multiple_of(row0 + i * rs, rs), rs), :]
        return pltpu.make_async_copy(src, xu_ref.at[rows(i), :], xsem)

    @pl.when(jnp.logical_and(n > 0, s == 0))
    def _():
        def start(i, c):
            x_copy(i).start()
            return c

        lax.fori_loop(0, n, start, 0)

        def wait(i, c):
            x_copy(i).wait()
            return c

        lax.fori_loop(0, n, wait, 0)

        def unpack(i, c):
            u = xu_ref[rows(i), :]
            lo = pltpu.bitcast(u << 16, F32)
            hi = pltpu.bitcast(u & jnp.uint32(0xFFFF0000), F32)
            xb_ref[rows(i), :half] = lo.astype(BF16)
            xb_ref[rows(i), half:] = hi.astype(BF16)
            return c

        lax.fori_loop(0, n, unpack, 0)

    nquad = lax.shift_right_logical(n, 2)
    has_pair = (n & 2) != 0
    odd = (n & 1) == 1

    def row_slice(r0, m):
        return pl.ds(pl.multiple_of(r0, rs), m)

    @pl.when(jnp.logical_and(n > 0, s < nf))
    def _():
        wgb_ref[...] = wg_ref[...].astype(BF16)
        wub_ref[...] = wu_ref[...].astype(BF16)

        def act_rows(r0, m):
            x = xb_ref[row_slice(r0, m), :]
            gt = jnp.dot(x, wgb_ref[...], preferred_element_type=F32) + bg_ref[...]
            ut = jnp.dot(x, wub_ref[...], preferred_element_type=F32) + bu_ref[...]
            gt = jnp.minimum(gt, SWIGLU_LIMIT)
            ut = jnp.clip(ut, -SWIGLU_LIMIT, SWIGLU_LIMIT)
            act = (ut + 1.0) * (gt * jax.nn.sigmoid(SWIGLU_ALPHA * gt))
            ab_ref[s, row_slice(r0, m), :] = act.astype(BF16)

        def quad(i, c):
            act_rows(i * (4 * rs), 4 * rs)
            return c

        lax.fori_loop(0, nquad, quad, 0)

        @pl.when(has_pair)
        def _():
            act_rows(nquad * (4 * rs), 2 * rs)

        @pl.when(odd)
        def _():
            act_rows((n - 1) * rs, rs)

    @pl.when(jnp.logical_and(n > 0, s >= nf))
    def _():
        d = s - nf
        wdb_ref[...] = wd_ref[...].astype(BF16)

        def out_copy(r0, m, buf, sem, j):
            dst = ys_ref.at[pl.ds(pl.multiple_of(row0 + r0, rs), m), j * td:(j + 1) * td]
            return pltpu.make_async_copy(buf, dst, sem)

        def start_out(r0, m, buf, sem):
            for j in range(nd):
                @pl.when(d == j)
                def _(j=j):
                    out_copy(r0, m, buf, sem, j).start()

        def out_rows(r0, m):
            a_full = jnp.concatenate([ab_ref[f, row_slice(r0, m), :] for f in range(nf)], axis=1)
            return jnp.dot(a_full, wdb_ref[...], preferred_element_type=F32) + bd_ref[...]

        @pl.when(odd)
        def _():
            or_ref[...] = out_rows((n - 1) * rs, rs)
            start_out((n - 1) * rs, rs, or_ref, rsem)

        @pl.when(has_pair)
        def _():
            op_ref[...] = out_rows(nquad * (4 * rs), 2 * rs)
            start_out(nquad * (4 * rs), 2 * rs, op_ref, psem)

        def quad(i, c):
            slot = i % 2

            @pl.when(i >= 2)
            def _():
                out_copy(0, 4 * rs, ob_ref.at[slot], osem.at[slot], 0).wait()

            ob_ref[slot] = out_rows(i * (4 * rs), 4 * rs)
            start_out(i * (4 * rs), 4 * rs, ob_ref.at[slot], osem.at[slot])
            return c

        lax.fori_loop(0, nquad, quad, 0)

        @pl.when(nquad >= 2)
        def _():
            slot = nquad % 2
            out_copy(0, 4 * rs, ob_ref.at[slot], osem.at[slot], 0).wait()

        @pl.when(nquad >= 1)
        def _():
            slot = (nquad + 1) % 2
            out_copy(0, 4 * rs, ob_ref.at[slot], osem.at[slot], 0).wait()

        @pl.when(has_pair)
        def _():
            out_copy(0, 2 * rs, op_ref, psem, 0).wait()

        @pl.when(odd)
        def _():
            out_copy(0, rs, or_ref, rsem, 0).wait()

    @pl.when(jnp.logical_and(g == pl.num_programs(0) - 1, s == nf + nd - 1))
    def _():
        z_ref[...] = jnp.zeros(z_ref.shape, z_ref.dtype)

        def z_copy(i):
            dst = ys_ref.at[pl.ds(pl.multiple_of(i * rs, rs), rs), :]
            return pltpu.make_async_copy(z_ref, dst, zsem)

        def start(i, c):
            z_copy(i).start()
            return c

        def wait(i, c):
            z_copy(i).wait()
            return c

        lax.fori_loop(tail_ref[0], tail_ref[1], start, 0)
        lax.fori_loop(tail_ref[0], tail_ref[1], wait, 0)


def _experts(ge, gs, gn, tail, xs, w_gu, b_gu, w_dn, b_dn, rs, rmax, tf, td):
    p = xs.shape[0]
    n_e, d, f2 = w_gu.shape
    fdim = f2 // 2
    nf = fdim // tf
    nd = d // td
    n_groups = ge.shape[0]

    def f_idx(g, s, gn):
        return jnp.where(gn[g] > 0, jnp.minimum(s, nf - 1), nf - 1)

    def d_idx(g, s, gn):
        return jnp.where(gn[g] > 0, jnp.maximum(s - nf, 0), nd - 1)

    return pl.pallas_call(
        functools.partial(_expert_kernel, rs=rs, nf=nf, nd=nd),
        out_shape=jax.ShapeDtypeStruct((p, d), F32),
        grid_spec=pltpu.PrefetchScalarGridSpec(
            num_scalar_prefetch=4,
            grid=(n_groups, nf + nd),
            in_specs=[pl.BlockSpec(memory_space=pl.ANY),
                      pl.BlockSpec((None, d, tf), lambda g, s, ge, gs, gn, tl: (ge[g], 0, f_idx(g, s, gn))),
                      pl.BlockSpec((None, d, tf),
                                   lambda g, s, ge, gs, gn, tl: (ge[g], 0, nf + f_idx(g, s, gn))),
                      pl.BlockSpec((None, 1, tf), lambda g, s, ge, gs, gn, tl: (ge[g], 0, f_idx(g, s, gn))),
                      pl.BlockSpec((None, 1, tf),
                                   lambda g, s, ge, gs, gn, tl: (ge[g], 0, nf + f_idx(g, s, gn))),
                      pl.BlockSpec((None, fdim, td), lambda g, s, ge, gs, gn, tl: (ge[g], 0, d_idx(g, s, gn))),
                      pl.BlockSpec((None, 1, td), lambda g, s, ge, gs, gn, tl: (ge[g], 0, d_idx(g, s, gn)))],
            out_specs=pl.BlockSpec(memory_space=pl.ANY),
            scratch_shapes=[pltpu.VMEM((rmax, d // 2), jnp.uint32),
                            pltpu.VMEM((rmax, d), BF16),
                            pltpu.VMEM((nf, rmax, tf), BF16),
                            pltpu.VMEM((d, tf), BF16), pltpu.VMEM((d, tf), BF16),
                            pltpu.VMEM((fdim, td), BF16),
                            pltpu.VMEM((2, 4 * rs, td), F32),
                            pltpu.VMEM((2 * rs, td), F32),
                            pltpu.VMEM((rs, td), F32),
                            pltpu.VMEM((rs, d), F32),
                            pltpu.SemaphoreType.DMA, pltpu.SemaphoreType.DMA((2,)),
                            pltpu.SemaphoreType.DMA, pltpu.SemaphoreType.DMA,
                            pltpu.SemaphoreType.DMA],
        ),
        compiler_params=_cparams("arbitrary", "arbitrary"),
        name="experts",
    )(ge, gs, gn, tail, xs, w_gu, w_gu, b_gu, b_gu, w_dn, b_dn)


def _combine_kernel(dest_ref, ys_ref, pk_ref, x1_ref, gpost_ref, gate_ref, o_ref, buf_ref, sem):
    tm = x1_ref.shape[0]
    base = pl.program_id(0) * (tm * TOP_K)

    def row_copy(t, k, d):
        return pltpu.make_async_copy(ys_ref.at[pl.ds(d, 1), :], buf_ref.at[k, pl.ds(t, 1), :], sem)

    def issue(t, carry):
        for k in range(TOP_K):
            row_copy(t, k, dest_ref[base + t * TOP_K + k]).start(priority=k % 2)
        return carry

    lax.fori_loop(0, tm, issue, 0, unroll=8)

    for k in range(TOP_K):
        pltpu.make_async_copy(ys_ref.at[pl.ds(0, tm), :], buf_ref.at[k], sem).wait()

    pk = pk_ref[...]
    f = jnp.zeros(x1_ref.shape, F32)
    for k in range(TOP_K):
        w = pk[:, 2 * TOP_K + k:2 * TOP_K + k + 1]
        f = f + buf_ref[k] * w
    o_ref[...] = x1_ref[...] + gate_ref[...] * _rms(f, gpost_ref[...])


def _combine(dest, ys, packed, x1, gpost, gate_f, seq, tm):
    t, d = x1.shape
    per_b = seq // tm
    return pl.pallas_call(
        _combine_kernel,
        out_shape=jax.ShapeDtypeStruct((t, d), F32),
        grid_spec=pltpu.PrefetchScalarGridSpec(
            num_scalar_prefetch=1,
            grid=(t // tm,),
            in_specs=[pl.BlockSpec(memory_space=pl.ANY),
                      pl.BlockSpec((tm, LANES), lambda i, dest: (i, 0)),
                      pl.BlockSpec((tm, d), lambda i, dest: (i, 0)),
                      pl.BlockSpec((1, d), lambda i, dest: (0, 0)),
                      pl.BlockSpec((None, 1, d), lambda i, dest: (i // per_b, 0, 0))],
            out_specs=pl.BlockSpec((tm, d), lambda i, dest: (i, 0)),
            scratch_shapes=[pltpu.VMEM((TOP_K, tm, d), F32), pltpu.SemaphoreType.DMA],
        ),
        compiler_params=_cparams("arbitrary"),
        name="combine",
    )(dest, ys, packed, x1, gpost, gate_f)


def _pad_heads(w, n_heads, width):
    k = w.shape[0]
    w = w.reshape(k, n_heads, width)
    return jnp.pad(w, ((0, 0), (0, 0), (0, LANES - width))).reshape(k, n_heads * LANES)


def _layer(x, c, positions, w_ada, b_ada, g_pre_mix, g_post_mix, g_pre_ffn, g_post_ffn,
           w_in, q_a_norm, w_q_b, kv_a_norm, w_kv_b, w_mla_o, ret_decay_fwd, ret_decay_bwd,
           ret_gn, w_ret_o, w_out, w_router, b_router, w_gate_up, b_gate_up, w_down, b_down):
    batch, seq, d = x.shape
    t = batch * seq
    q_rank = q_a_norm.shape[0]
    kv_rank = kv_a_norm.shape[0]
    hd = RET_HEADS * 2 * LANES
    x2 = x.reshape(t, d)

    c_pad = jnp.pad(c, ((0, 8 - batch), (0, 0)))
    ada = _ada(c_pad, w_ada, b_ada[None, :])[:batch]
    shift_m, scale_m, gate_m, shift_f, scale_f, gate_f = [
        ada[:, i * d:(i + 1) * d].reshape(batch, 1, d) for i in range(N_MOD)]

    o_q, o_kv = 0, q_rank
    o_r = q_rank + kv_rank + MLA_ROPE
    w_wide = w_in[:, o_r:]
    w_main = jnp.concatenate([w_wide, w_in[:, o_q:o_q + q_rank], w_in[:, o_kv:o_kv + kv_rank]],
                             axis=1).astype(BF16)
    w_kr = jnp.pad(w_in[:, o_kv + kv_rank:o_r], ((0, 0), (0, LANES - MLA_ROPE))).astype(BF16)
    n_wide = w_wide.shape[1]
    proj, kr_raw = _in_proj(x2, g_pre_mix[None, :], scale_m, shift_m, w_main, w_kr, seq,
                            tm=min(1024, seq), tn=1024)
    blk_w = 2 * LANES
    blk = {"rq": 0, "rk": hd // blk_w, "rv": 2 * hd // blk_w, "rg": 3 * hd // blk_w}
    ga_blk, gb_blk = (4 * hd) // d, (4 * hd + d) // d
    ql_blk, ckv_blk = n_wide // q_rank, (n_wide + q_rank) // kv_rank

    cos_r, sin_r, cm, sm = _rope_tables(positions.reshape(t, 1), tm=min(512, seq))

    wq = w_q_b.reshape(q_rank, MLA_HEADS, MLA_QK)
    wqn = wq[:, :, :MLA_NOPE].reshape(q_rank, MLA_HEADS * MLA_NOPE).astype(BF16)
    wqr = _pad_heads(wq[:, :, MLA_NOPE:].reshape(q_rank, MLA_HEADS * MLA_ROPE), MLA_HEADS,
                     MLA_ROPE).astype(BF16)
    wkv = w_kv_b.reshape(kv_rank, MLA_HEADS, MLA_NOPE + MLA_V)
    wk = wkv[:, :, :MLA_NOPE].reshape(kv_rank, MLA_HEADS * MLA_NOPE).astype(BF16)
    wv = wkv[:, :, MLA_NOPE:].reshape(kv_rank, MLA_HEADS * MLA_V).astype(BF16)
    q, k, v = _mla_prep(proj, kr_raw, cm, sm, q_a_norm[None, :], kv_a_norm[None, :],
                        wqn, wqr, wk, wv, ql_blk, ckv_blk, seq, tm=min(512, seq))
    o_mla = _attention(q, k, v, batch, seq, tq=min(1024, seq), rq=128, tk=min(512, seq))

    dec_f = jnp.broadcast_to(ret_decay_fwd.astype(F32)[:, None, None], (RET_HEADS, 8, LANES))
    dec_b = jnp.broadcast_to(ret_decay_bwd.astype(F32)[:, None, None], (RET_HEADS, 8, LANES))
    o_ret = _retention(proj, cos_r, sin_r, dec_f, dec_b, ret_gn[None, :], batch, seq, blk,
                       ch=min(256, seq))

    m = _mix_gate(o_mla, o_ret, proj, w_mla_o.astype(BF16), w_ret_o.astype(BF16), ga_blk, gb_blk,
                  tm=min(512, seq))
    wr32 = jnp.pad(w_router, ((0, 0), (0, LANES - N_EXPERTS)))
    wr_hi = wr32.astype(BF16)
    wr = jnp.concatenate([wr_hi, (wr32 - wr_hi.astype(F32)).astype(BF16)], axis=1)
    br = jnp.pad(b_router, (0, LANES - N_EXPERTS))[None, :]
    x1, h2, logits = _mix_out(m, x2, w_out.astype(BF16), g_post_mix[None, :], gate_m,
                              g_pre_ffn[None, :], scale_f, shift_f, wr, br, seq, tm=min(256, seq))

    packed, cnt = _route(logits, tm=min(256, seq))
    e_idx = packed[:, 0:TOP_K].astype(jnp.int32)
    rank = packed[:, TOP_K:2 * TOP_K].astype(jnp.int32)
    counts = cnt[0, :N_EXPERTS].astype(jnp.int32)
    a = t * TOP_K
    rs = min(256, a // N_EXPERTS)
    rmax = GROUP_SUB * rs
    n_sub = (a + N_EXPERTS * (rs - 1) + rs - 1) // rs
    n_groups = N_EXPERTS + a // rmax
    padded = ((counts + rs - 1) // rs) * rs
    pend = jnp.cumsum(padded)
    pstart = pend - padded
    dest = (pstart[e_idx] + rank).reshape(a).astype(jnp.int32)
    used_sub = (pend[-1] // rs).astype(jnp.int32).reshape(1)
    tail = jnp.concatenate([used_sub, jnp.full((1,), n_sub, jnp.int32)])
    e_sub = padded // rs
    e_grp = (e_sub + GROUP_SUB - 1) // GROUP_SUB
    g_end = jnp.cumsum(e_grp)
    slot = jnp.arange(n_groups, dtype=jnp.int32)
    g_e = jnp.minimum(jnp.sum((g_end[None, :] <= slot[:, None]).astype(jnp.int32), axis=1),
                      N_EXPERTS - 1)
    g_k = slot - (g_end - e_grp)[g_e]
    live = slot < g_end[-1]
    last_e = g_e[jnp.maximum(g_end[-1] - 1, 0)]
    g_e = jnp.where(live, g_e, last_e).astype(jnp.int32)
    g_s = jnp.where(live, pstart[g_e] + g_k * rmax, 0).astype(jnp.int32)
    g_n = jnp.where(live, jnp.clip(e_sub[g_e] - GROUP_SUB * g_k, 0, GROUP_SUB), 0).astype(jnp.int32)

    xs = _dispatch(dest, (pstart + counts).astype(jnp.int32), (padded - counts).astype(jnp.int32),
                   tail, h2, n_sub * rs, tm=min(256, seq), tm_e=rs)
    fdim = w_down.shape[1]
    ys = _experts(g_e, g_s, g_n, tail, xs, w_gate_up, b_gate_up[:, None, :], w_down,
                  b_down[:, None, :], rs=rs, rmax=rmax, tf=min(256, fdim), td=min(256, d))
    out = _combine(dest, ys, packed, x1, g_post_ffn[None, :], gate_f, seq, tm=min(256, seq))
    return out.reshape(batch, seq, d)


def kernel(x, c, positions, w_ada, b_ada, g_pre_mix, g_post_mix, g_pre_ffn, g_post_ffn, w_in,
           q_a_norm, w_q_b, kv_a_norm, w_kv_b, w_mla_o, ret_decay_fwd, ret_decay_bwd, ret_gn,
           w_ret_o, w_out, w_router, b_router, w_gate_up, b_gate_up, w_down, b_down):
    params = (w_ada, b_ada, g_pre_mix, g_post_mix, g_pre_ffn, g_post_ffn, w_in, q_a_norm, w_q_b,
              kv_a_norm, w_kv_b, w_mla_o, ret_decay_fwd, ret_decay_bwd, ret_gn, w_ret_o, w_out,
              w_router, b_router, w_gate_up, b_gate_up, w_down, b_down)
    for l in range(w_ada.shape[0]):
        x = _layer(x, c, positions, *[p[l] for p in params])
    return x
```

```python
import functools

import numpy as np
import jax
import jax.numpy as jnp
from jax import lax
from jax.experimental import pallas as pl
from jax.experimental.pallas import tpu as pltpu

F32 = jnp.float32
BF16 = jnp.bfloat16

MLA_HEADS = 8
MLA_NOPE = 128
MLA_ROPE = 64
MLA_V = 128
MLA_QK = MLA_NOPE + MLA_ROPE
RET_HEADS = 8
N_EXPERTS = 32
TOP_K = 4
SWIGLU_LIMIT = 7.0
SWIGLU_ALPHA = 1.702
ROPE_THETA = 10000.0
RMS_EPS = 1e-6
GN_EPS = 1e-6
LOG2E = 1.4426950408889634
N_MOD = 6

LANES = 128
SUBLANES = 8
VMEM_LIMIT = 56 * 1024 * 1024
GROUP_SUB = 8
FLAG_PAIR, FLAG_SINGLE = 2, 3


def _cparams(*sem):
    return pltpu.CompilerParams(dimension_semantics=sem, vmem_limit_bytes=VMEM_LIMIT)


def _rms(x, gain):
    return x * lax.rsqrt(jnp.mean(x * x, axis=-1, keepdims=True) + RMS_EPS) * gain


def _ada_kernel(c_ref, w_ref, b_ref, o_ref):
    o_ref[...] = jnp.dot(c_ref[...], w_ref[...], preferred_element_type=F32,
                         precision=lax.Precision.HIGHEST) + b_ref[...]


def _ada(c_pad, w_ada, b_ada, tn=1024):
    m, d = c_pad.shape
    n = w_ada.shape[1]
    return pl.pallas_call(
        _ada_kernel,
        out_shape=jax.ShapeDtypeStruct((m, n), F32),
        grid=(n // tn,),
        in_specs=[pl.BlockSpec((m, d), lambda j: (0, 0)),
                  pl.BlockSpec((d, tn), lambda j: (0, j)),
                  pl.BlockSpec((1, tn), lambda j: (0, j))],
        out_specs=pl.BlockSpec((m, tn), lambda j: (0, j)),
        compiler_params=_cparams("arbitrary"),
        name="ada",
    )(c_pad, w_ada, b_ada)


def _inproj_kernel(x_ref, g_ref, sc_ref, sh_ref, w_ref, wkr_ref, o_ref, okr_ref, h_ref):
    @pl.when(pl.program_id(1) == 0)
    def _():
        h = _rms(x_ref[...], g_ref[...]) * (1.0 + sc_ref[...]) + sh_ref[...]
        hb = h.astype(BF16)
        h_ref[...] = hb
        okr_ref[...] = jnp.dot(hb, wkr_ref[...], preferred_element_type=F32)

    o_ref[...] = jnp.dot(h_ref[...], w_ref[...], preferred_element_type=F32).astype(o_ref.dtype)


def _in_proj(x2, g, scale, shift, w_main, w_kr, seq, tm, tn):
    t, d = x2.shape
    n = w_main.shape[1]
    per_b = seq // tm
    return pl.pallas_call(
        _inproj_kernel,
        out_shape=(jax.ShapeDtypeStruct((t, n), BF16), jax.ShapeDtypeStruct((t, LANES), F32)),
        grid=(t // tm, n // tn),
        in_specs=[pl.BlockSpec((tm, d), lambda i, j: (i, 0)),
                  pl.BlockSpec((1, d), lambda i, j: (0, 0)),
                  pl.BlockSpec((None, 1, d), lambda i, j: (i // per_b, 0, 0)),
                  pl.BlockSpec((None, 1, d), lambda i, j: (i // per_b, 0, 0)),
                  pl.BlockSpec((d, tn), lambda i, j: (0, j)),
                  pl.BlockSpec((d, LANES), lambda i, j: (0, 0))],
        out_specs=(pl.BlockSpec((tm, tn), lambda i, j: (i, j)),
                   pl.BlockSpec((tm, LANES), lambda i, j: (i, 0))),
        scratch_shapes=[pltpu.VMEM((tm, d), BF16)],
        compiler_params=_cparams("arbitrary", "arbitrary"),
        name="in_proj",
    )(x2, g, scale, shift, w_main, w_kr)


def _rope_kernel(pos_ref, invr_ref, invm_ref, cr_ref, sr_ref, cm_ref, sm_ref):
    pos = pos_ref[...].astype(F32)
    ang_r = pos * invr_ref[...]
    cr_ref[...] = jnp.cos(ang_r)
    sr_ref[...] = jnp.sin(ang_r)
    ang_m = pos * invm_ref[...]
    lane = lax.broadcasted_iota(jnp.int32, ang_m.shape, 1)
    half = MLA_ROPE // 2
    cm_ref[...] = jnp.where(lane < MLA_ROPE, jnp.cos(ang_m), 0.0)
    s = jnp.sin(ang_m)
    sm_ref[...] = jnp.where(lane < half, -s, jnp.where(lane < MLA_ROPE, s, 0.0))


def _rope_tables(pos_col, tm):
    t = pos_col.shape[0]
    half_r = LANES
    inv_r = (1.0 / (np.float32(ROPE_THETA) ** (np.arange(half_r, dtype=np.float32) / np.float32(half_r)))
             ).astype(np.float32)
    half_m = MLA_ROPE // 2
    inv_m32 = (1.0 / (np.float32(ROPE_THETA) ** (np.arange(half_m, dtype=np.float32) / np.float32(half_m)))
               ).astype(np.float32)
    inv_m = np.zeros((LANES,), np.float32)
    inv_m[:half_m] = inv_m32
    inv_m[half_m:2 * half_m] = inv_m32
    tab = jax.ShapeDtypeStruct((t, LANES), F32)
    row = pl.BlockSpec((tm, LANES), lambda i: (i, 0))
    return pl.pallas_call(
        _rope_kernel,
        out_shape=(tab, tab, tab, tab),
        grid=(t // tm,),
        in_specs=[pl.BlockSpec((tm, 1), lambda i: (i, 0)),
                  pl.BlockSpec((1, LANES), lambda i: (0, 0)),
                  pl.BlockSpec((1, LANES), lambda i: (0, 0))],
        out_specs=(row, row, row, row),
        compiler_params=_cparams("arbitrary"),
        name="rope_tables",
    )(pos_col, jnp.asarray(inv_r)[None, :], jnp.asarray(inv_m)[None, :])


def _rot64(x, c, s):
    return x * c + (pltpu.roll(x, 32, 1) + pltpu.roll(x, 96, 1)) * s


def _mla_prep_kernel(ql_ref, ckv_ref, kr_ref, cm_ref, sm_ref, gq_ref, gkv_ref,
                     wqn_ref, wqr_ref, wk_ref, wv_ref, q_ref, k_ref, v_ref):
    c = cm_ref[...]
    s = sm_ref[...]
    qn = _rms(ql_ref[...].astype(F32), gq_ref[...]).astype(BF16)
    q_nope = jnp.dot(qn, wqn_ref[...], preferred_element_type=F32)
    q_rope = jnp.dot(qn, wqr_ref[...], preferred_element_type=F32)
    cn = _rms(ckv_ref[...].astype(F32), gkv_ref[...]).astype(BF16)
    k_nope = jnp.dot(cn, wk_ref[...], preferred_element_type=F32)
    v = jnp.dot(cn, wv_ref[...], preferred_element_type=F32)
    k_rot_t = _rot64(kr_ref[...], c, s).T.astype(BF16)
    lane = lax.broadcasted_iota(jnp.int32, c.shape, 1)
    ones_col = jnp.where(lane == 0, 1.0, 0.0).astype(BF16)
    scale = MLA_QK ** -0.5 * LOG2E
    for h in range(MLA_HEADS):
        lo, hi = h * LANES, (h + 1) * LANES
        q_ref[:, 2 * lo:2 * lo + LANES] = (q_nope[:, lo:hi] * scale).astype(BF16)
        q_ref[:, 2 * lo + LANES:2 * hi] = (_rot64(q_rope[:, lo:hi], c, s) * scale).astype(BF16)
        k_ref[2 * lo:2 * lo + LANES, :] = k_nope[:, lo:hi].T.astype(BF16)
        k_ref[2 * lo + LANES:2 * hi, :] = k_rot_t
        v_ref[:, 2 * lo:2 * lo + LANES] = v[:, lo:hi].astype(BF16)
        v_ref[:, 2 * lo + LANES:2 * hi] = ones_col


def _mla_prep(proj, kr_raw, cm, sm, gq, gkv, wqn, wqr, wk, wv, ql_blk, ckv_blk, seq, tm):
    t = proj.shape[0]
    rank = wqn.shape[0]
    hd = MLA_HEADS * LANES
    per_b = seq // tm
    full = lambda shape: pl.BlockSpec(shape, lambda i: (0, 0))
    row = lambda w: pl.BlockSpec((tm, w), lambda i: (i, 0))
    return pl.pallas_call(
        _mla_prep_kernel,
        out_shape=(jax.ShapeDtypeStruct((t, 2 * hd), BF16),
                   jax.ShapeDtypeStruct(((t // seq) * 2 * hd, seq), BF16),
                   jax.ShapeDtypeStruct((t, 2 * hd), BF16)),
        grid=(t // tm,),
        in_specs=[pl.BlockSpec((tm, rank), lambda i: (i, ql_blk)),
                  pl.BlockSpec((tm, rank), lambda i: (i, ckv_blk)),
                  row(LANES), row(LANES), row(LANES),
                  full((1, rank)), full((1, rank)),
                  full((rank, hd)), full((rank, hd)), full((rank, hd)), full((rank, hd))],
        out_specs=(row(2 * hd),
                   pl.BlockSpec((2 * hd, tm), lambda i: (i // per_b, i % per_b)),
                   row(2 * hd)),
        compiler_params=_cparams("arbitrary"),
        name="mla_prep",
    )(proj, proj, kr_raw, cm, sm, gq, gkv, wqn, wqr, wk, wv)


def _attn_kernel(q_ref, kt_ref, v_ref, o_ref, s_ref, *, rq, tk):
    tq = q_ref.shape[0]
    seq = kt_ref.shape[1]
    for r0 in range(0, tq, rq):
        q = q_ref[r0:r0 + rq, :]
        m_lane = None
        for c0 in range(0, seq, tk):
            s = jnp.dot(q, kt_ref[:, c0:c0 + tk], preferred_element_type=F32)
            s_ref[r0:r0 + rq, c0:c0 + tk] = s
            for l0 in range(0, tk, LANES):
                blk = s[:, l0:l0 + LANES]
                m_lane = blk if m_lane is None else jnp.maximum(m_lane, blk)
        m = jnp.max(m_lane, axis=-1, keepdims=True)
        acc = jnp.zeros((rq, 2 * MLA_V), F32)
        for c0 in range(0, seq, tk):
            p = jnp.exp2(s_ref[r0:r0 + rq, c0:c0 + tk] - m).astype(BF16)
            acc = acc + jnp.dot(p, v_ref[c0:c0 + tk, :], preferred_element_type=F32)
        o_ref[r0:r0 + rq, :] = (acc[:, :MLA_V] / acc[:, MLA_V:MLA_V + 1]).astype(o_ref.dtype)


def _attention(q, kt, v, batch, seq, tq, rq, tk):
    t = q.shape[0]
    nq = seq // tq
    return pl.pallas_call(
        functools.partial(_attn_kernel, rq=rq, tk=tk),
        out_shape=jax.ShapeDtypeStruct((t, MLA_HEADS * MLA_V), BF16),
        grid=(batch, MLA_HEADS, nq),
        in_specs=[pl.BlockSpec((tq, 2 * LANES), lambda b, h, i: (b * nq + i, h)),
                  pl.BlockSpec((2 * LANES, seq), lambda b, h, i: (b * MLA_HEADS + h, 0)),
                  pl.BlockSpec((seq, 2 * MLA_V), lambda b, h, i: (b, h))],
        out_specs=pl.BlockSpec((tq, MLA_V), lambda b, h, i: (b * nq + i, h)),
        scratch_shapes=[pltpu.VMEM((tq, seq), F32)],
        compiler_params=_cparams("arbitrary", "arbitrary", "arbitrary"),
        name="mla_attention",
    )(q, kt, v)


def _ret_kernel(rq_ref, rk_ref, rv_ref, rg_ref, cos_ref, sin_ref, df_ref, db_ref, gn_ref,
                o_ref, krot_ref, sf_ref, st_ref, dmat_ref, xif_ref, zf_ref, xib_ref, zb_ref, *, ch):
    seq, dk = rq_ref.shape
    half = dk // 2
    n_ch = seq // ch
    lgf = jnp.log(jax.nn.sigmoid(df_ref[...]))[:1, :1]
    lgb = jnp.log(jax.nn.sigmoid(db_ref[...]))[:1, :1]

    ri = lax.broadcasted_iota(jnp.int32, (ch, ch), 0)
    ci = lax.broadcasted_iota(jnp.int32, (ch, ch), 1)
    diff = (ri - ci).astype(F32)
    dmat_ref[...] = jnp.where(diff >= 0, jnp.exp(lgf * jnp.maximum(diff, 0.0)),
                              jnp.exp(lgb * jnp.maximum(-diff, 0.0)))
    pos = lax.broadcasted_iota(jnp.int32, (ch, dk), 0).astype(F32)
    xif_ref[...] = jnp.exp(lgf * (pos + 1.0))
    zf_ref[...] = jnp.exp(lgf * (ch - 1.0 - pos))
    xib_ref[...] = jnp.exp(lgb * (ch - pos))
    zb_ref[...] = jnp.exp(lgb * pos)
    gcf = jnp.exp(lgf * float(ch))
    gcb = jnp.exp(lgb * float(ch))
    k_scale = dk ** -0.5

    def rot(x, c, s):
        x1, x2 = x[:, :half], x[:, half:]
        return jnp.concatenate([x1 * c - x2 * s, x1 * s + x2 * c], axis=-1)

    def kv_outer(kz, v):
        return lax.dot_general(kz, v, (((0,), (0,)), ((), ())), preferred_element_type=F32)

    st_ref[...] = jnp.zeros(st_ref.shape, F32)

    def fwd(n, carry):
        r0 = pl.multiple_of(n * ch, ch)
        rows = pl.ds(r0, ch)
        k = rot(rk_ref[rows, :].astype(F32), cos_ref[rows, :], sin_ref[rows, :]) * k_scale
        krot_ref[rows, :] = k.astype(BF16)
        sf_ref[n] = st_ref[...].astype(BF16)
        kv = kv_outer((k * zf_ref[...]).astype(BF16), rv_ref[rows, :])
        st_ref[...] = st_ref[...] * gcf + kv
        return carry

    lax.fori_loop(0, n_ch, fwd, 0)

    st_ref[...] = jnp.zeros(st_ref.shape, F32)

    def bwd(i, carry):
        n = n_ch - 1 - i
        r0 = pl.multiple_of(n * ch, ch)
        rows = pl.ds(r0, ch)
        q = rot(rq_ref[rows, :].astype(F32), cos_ref[rows, :], sin_ref[rows, :])
        kb = krot_ref[rows, :]
        v = rv_ref[rows, :]
        s = lax.dot_general(q.astype(BF16), kb, (((1,), (1,)), ((), ())), preferred_element_type=F32)
        o = jnp.dot((s * dmat_ref[...]).astype(BF16), v, preferred_element_type=F32)
        o = o + jnp.dot((q * xif_ref[...]).astype(BF16), sf_ref[n], preferred_element_type=F32)
        o = o + jnp.dot((q * xib_ref[...]).astype(BF16), st_ref[...].astype(BF16),
                        preferred_element_type=F32)
        mu = jnp.mean(o, axis=-1, keepdims=True)
        oc = o - mu
        var = jnp.mean(oc * oc, axis=-1, keepdims=True)
        on = oc * lax.rsqrt(var + GN_EPS)
        g = rg_ref[rows, :].astype(F32)
        o_ref[rows, :] = (g * jax.nn.sigmoid(g) * (on * gn_ref[...])).astype(o_ref.dtype)
        kv = kv_outer((kb.astype(F32) * zb_ref[...]).astype(BF16), v)
        st_ref[...] = st_ref[...] * gcb + kv
        return carry

    lax.fori_loop(0, n_ch, bwd, 0)


def _retention(proj, cos_r, sin_r, dec_f, dec_b, gn, batch, seq, blk, ch):
    t = proj.shape[0]
    dk = 2 * LANES
    n_ch = seq // ch
    col = lambda base: pl.BlockSpec((seq, dk), lambda b, h: (b, base + h))
    tab = pl.BlockSpec((seq, LANES), lambda b, h: (b, 0))
    dec = pl.BlockSpec((None, 8, LANES), lambda b, h: (h, 0, 0))
    return pl.pallas_call(
        functools.partial(_ret_kernel, ch=ch),
        out_shape=jax.ShapeDtypeStruct((t, RET_HEADS * dk), BF16),
        grid=(batch, RET_HEADS),
        in_specs=[col(blk["rq"]), col(blk["rk"]), col(blk["rv"]), col(blk["rg"]),
                  tab, tab, dec, dec,
                  pl.BlockSpec((1, dk), lambda b, h: (0, h))],
        out_specs=pl.BlockSpec((seq, dk), lambda b, h: (b, h)),
        scratch_shapes=[pltpu.VMEM((seq, dk), BF16),
                        pltpu.VMEM((n_ch, dk, dk), BF16),
                        pltpu.VMEM((dk, dk), F32),
                        pltpu.VMEM((ch, ch), F32),
                        pltpu.VMEM((ch, dk), F32), pltpu.VMEM((ch, dk), F32),
                        pltpu.VMEM((ch, dk), F32), pltpu.VMEM((ch, dk), F32)],
        compiler_params=_cparams("arbitrary", "arbitrary"),
        name="retention",
    )(proj, proj, proj, proj, cos_r, sin_r, dec_f, dec_b, gn)


def _mixgate_kernel(oa_ref, ob_ref, ga_ref, gb_ref, wa_ref, wb_ref, m_ref):
    ya = jnp.dot(oa_ref[...], wa_ref[...], preferred_element_type=F32)
    yb = jnp.dot(ob_ref[...], wb_ref[...], preferred_element_type=F32)
    m = jax.nn.sigmoid(ga_ref[...].astype(F32)) * ya + jax.nn.sigmoid(gb_ref[...].astype(F32)) * yb
    m_ref[...] = m.astype(m_ref.dtype)


def _mix_gate(o_mla, o_ret, proj, wa, wb, ga_blk, gb_blk, tm):
    t = o_mla.shape[0]
    d = wa.shape[1]
    return pl.pallas_call(
        _mixgate_kernel,
        out_shape=jax.ShapeDtypeStruct((t, d), BF16),
        grid=(t // tm,),
        in_specs=[pl.BlockSpec((tm, o_mla.shape[1]), lambda i: (i, 0)),
                  pl.BlockSpec((tm, o_ret.shape[1]), lambda i: (i, 0)),
                  pl.BlockSpec((tm, d), lambda i: (i, ga_blk)),
                  pl.BlockSpec((tm, d), lambda i: (i, gb_blk)),
                  pl.BlockSpec(wa.shape, lambda i: (0, 0)),
                  pl.BlockSpec(wb.shape, lambda i: (0, 0))],
        out_specs=pl.BlockSpec((tm, d), lambda i: (i, 0)),
        compiler_params=_cparams("arbitrary"),
        name="mix_gate",
    )(o_mla, o_ret, proj, proj, wa, wb)


def _mixout_kernel(m_ref, x_ref, wo_ref, gpost_ref, gate_ref, gpre_ref, sc_ref, sh_ref,
                   wr_ref, br_ref, x1_ref, h2_ref, lg_ref):
    mix = jnp.dot(m_ref[...], wo_ref[...], preferred_element_type=F32)
    x1 = x_ref[...] + gate_ref[...] * _rms(mix, gpost_ref[...])
    x1_ref[...] = x1
    h2 = _rms(x1, gpre_ref[...]) * (1.0 + sc_ref[...]) + sh_ref[...]
    half = h2.shape[1] // 2
    bits = pltpu.bitcast(h2.astype(BF16).astype(F32), jnp.uint32)
    h2_ref[...] = (bits[:, :half] >> 16) | (bits[:, half:] & jnp.uint32(0xFFFF0000))
    h_hi = h2.astype(BF16)
    h_lo = (h2 - h_hi.astype(F32)).astype(BF16)
    wr = wr_ref[...]
    hh = jnp.dot(h_hi, wr, preferred_element_type=F32)
    lh = jnp.dot(h_lo, wr[:, :LANES], preferred_element_type=F32)
    lg_ref[...] = hh[:, :LANES] + (hh[:, LANES:] + lh) + br_ref[...]


def _mix_out(m, x2, wo, gpost, gate_m, gpre, scale_f, shift_f, wr, br, seq, tm):
    t, d = x2.shape
    per_b = seq // tm
    row = pl.BlockSpec((tm, d), lambda i: (i, 0))
    vec = pl.BlockSpec((1, d), lambda i: (0, 0))
    mod = pl.BlockSpec((None, 1, d), lambda i: (i // per_b, 0, 0))
    return pl.pallas_call(
        _mixout_kernel,
        out_shape=(jax.ShapeDtypeStruct((t, d), F32), jax.ShapeDtypeStruct((t, d // 2), jnp.uint32),
                   jax.ShapeDtypeStruct((t, LANES), F32)),
        grid=(t // tm,),
        in_specs=[row, row, pl.BlockSpec((d, d), lambda i: (0, 0)), vec, mod, vec, mod, mod,
                  pl.BlockSpec((d, 2 * LANES), lambda i: (0, 0)),
                  pl.BlockSpec((1, LANES), lambda i: (0, 0))],
        out_specs=(row, pl.BlockSpec((tm, d // 2), lambda i: (i, 0)),
                   pl.BlockSpec((tm, LANES), lambda i: (i, 0))),
        compiler_params=_cparams("arbitrary"),
        name="mix_out",
    )(m, x2, wo, gpost, gate_m, gpre, scale_f, shift_f, wr, br)


def _route_kernel(lg_ref, o_ref, cnt_ref, carry_ref):
    @pl.when(pl.program_id(0) == 0)
    def _():
        carry_ref[...] = jnp.zeros(carry_ref.shape, F32)

    tm = lg_ref.shape[0]
    lane = lax.broadcasted_iota(jnp.int32, (tm, LANES), 1).astype(F32)
    l = jnp.where(lane < N_EXPERTS, lg_ref[...], -jnp.inf)
    vals, idxs = [], []
    for _ in range(TOP_K):
        mx = jnp.max(l, axis=-1, keepdims=True)
        ik = jnp.min(jnp.where(l == mx, lane, float(LANES)), axis=-1, keepdims=True)
        vals.append(mx)
        idxs.append(ik)
        l = jnp.where(lane == ik, -jnp.inf, l)
    es = [jnp.exp(v - vals[0]) for v in vals]
    den = es[0] + es[1] + es[2] + es[3]
    onehot = jnp.zeros((tm, LANES), F32)
    for ik in idxs:
        onehot = onehot + jnp.where(lane == ik, 1.0, 0.0)
    ri = lax.broadcasted_iota(jnp.int32, (tm, tm), 0)
    ci = lax.broadcasted_iota(jnp.int32, (tm, tm), 1)
    tri = jnp.where(ci < ri, 1.0, 0.0).astype(BF16)
    cum = jnp.dot(tri, onehot.astype(BF16), preferred_element_type=F32) + carry_ref[0:1, :]
    packed = jnp.zeros((tm, LANES), F32)
    for k in range(TOP_K):
        rank = jnp.sum(jnp.where(lane == idxs[k], cum, 0.0), axis=-1, keepdims=True)
        packed = packed + jnp.where(lane == float(k), idxs[k], 0.0)
        packed = packed + jnp.where(lane == float(TOP_K + k), rank, 0.0)
        packed = packed + jnp.where(lane == float(2 * TOP_K + k), es[k] / den, 0.0)
    o_ref[...] = packed
    carry_ref[...] = carry_ref[...] + jnp.sum(onehot, axis=0, keepdims=True)
    cnt_ref[...] = carry_ref[...]


def _route(logits, tm):
    t = logits.shape[0]
    return pl.pallas_call(
        _route_kernel,
        out_shape=(jax.ShapeDtypeStruct((t, LANES), F32), jax.ShapeDtypeStruct((8, LANES), F32)),
        grid=(t // tm,),
        in_specs=[pl.BlockSpec((tm, LANES), lambda i: (i, 0))],
        out_specs=(pl.BlockSpec((tm, LANES), lambda i: (i, 0)),
                   pl.BlockSpec((8, LANES), lambda i: (0, 0))),
        scratch_shapes=[pltpu.VMEM((8, LANES), F32)],
        compiler_params=_cparams("arbitrary"),
        name="route",
    )(logits)


def _dispatch_kernel(dest_ref, pad_pos_ref, pad_len_ref, tail_ref, h_ref, xs_ref, z_ref, sem, zsem,
                     *, tm_e):
    tm = h_ref.shape[0]
    zr = z_ref.shape[0]
    base = pl.program_id(0) * (tm * TOP_K)
    pieces = [p for p in (2 ** i for i in range(20)) if SUBLANES <= p <= zr and p < tm_e][::-1]

    def zero_copy(pos, p):
        return pltpu.make_async_copy(z_ref.at[pl.ds(0, p), :], xs_ref.at[pl.ds(pos, p), :], zsem)

    def pad_rows(act):
        for e in range(N_EXPERTS):
            pos = pad_pos_ref[e]
            head = (-pos) & (SUBLANES - 1)
            for j in range(SUBLANES - 1):
                @pl.when(j < head)
                def _(pos=pos, j=j):
                    act(zero_copy(pos + j, 1))

            pos = pos + head
            ln = pad_len_ref[e] - head
            for p in pieces:
                hit = (ln & p) != 0

                @pl.when(hit)
                def _(pos=pos, p=p):
                    act(zero_copy(pl.multiple_of(pos, SUBLANES), p))

                pos = pos + jnp.where(hit, p, 0)

    def tail_rows(act):
        def body(j, carry):
            for r0 in range(0, tm_e, zr):
                act(zero_copy(pl.multiple_of(j * tm_e + r0, SUBLANES), zr))
            return carry

        lax.fori_loop(tail_ref[0], tail_ref[1], body, 0)

    @pl.when(pl.program_id(0) == 0)
    def _():
        z_ref[...] = jnp.zeros(z_ref.shape, z_ref.dtype)
        pad_rows(lambda c: c.start())
        tail_rows(lambda c: c.start())

    def row_copy(t, d):
        return pltpu.make_async_copy(h_ref.at[pl.ds(t, 1), :], xs_ref.at[pl.ds(d, 1), :], sem)

    def issue(t, carry):
        for k in range(TOP_K):
            row_copy(t, dest_ref[base + t * TOP_K + k]).start(priority=k % 2)
        return carry

    lax.fori_loop(0, tm, issue, 0, unroll=8)

    for k in range(TOP_K):
        pltpu.make_async_copy(h_ref, xs_ref.at[pl.ds(0, tm), :], sem).wait()

    @pl.when(pl.program_id(0) == 0)
    def _():
        pad_rows(lambda c: c.wait())
        tail_rows(lambda c: c.wait())


def _dispatch(dest, pad_pos, pad_len, tail, h2, n_rows, tm, tm_e):
    t, d = h2.shape
    zr = min(256, tm_e)
    nsp = 4
    return pl.pallas_call(
        functools.partial(_dispatch_kernel, tm_e=tm_e),
        out_shape=jax.ShapeDtypeStruct((n_rows, d), h2.dtype),
        grid_spec=pltpu.PrefetchScalarGridSpec(
            num_scalar_prefetch=nsp,
            grid=(t // tm,),
            in_specs=[pl.BlockSpec((tm, d), lambda i, *_: (i, 0))],
            out_specs=pl.BlockSpec(memory_space=pl.ANY),
            scratch_shapes=[pltpu.VMEM((zr, d), h2.dtype), pltpu.SemaphoreType.DMA,
                            pltpu.SemaphoreType.DMA],
        ),
        compiler_params=_cparams("arbitrary"),
        name="dispatch",
    )(dest, pad_pos, pad_len, tail, h2)


def _expert_kernel(ge_ref, gs_ref, gn_ref, tail_ref, xs_ref, wg_ref, wu_ref, bg_ref, bu_ref,
                   wd_ref, bd_ref, ys_ref, xu_ref, xb_ref, ab_ref, wgb_ref, wub_ref, wdb_ref,
                   ob_ref, op_ref, or_ref, z_ref, flag_ref, xsem, osem, psem, rsem, zsem, *, rs, nf, nd):
    del ge_ref
    g = pl.program_id(0)
    s = pl.program_id(1)
    n = gn_ref[g]
    row0 = gs_ref[g]
    half = xu_ref.shape[1]
    tf = wg_ref.shape[1]
    td = wd_ref.shape[1]

    def rows(i):
        return pl.ds(pl.multiple_of(i * rs, rs), rs)

    def x_copy(base, i):
        src = xs_ref.at[pl.ds(pl.multiple_of(base + i * rs, rs), rs), :]
        return pltpu.make_async_copy(src, xu_ref.at[rows(i), :], xsem)

    def x_start(base, cnt):
        def start(i, c):
            x_copy(base, i).start()
            return c

        lax.fori_loop(0, cnt, start, 0)

    g_next = jnp.minimum(g + 1, pl.num_programs(0) - 1)
    n_next = jnp.where(g + 1 < pl.num_programs(0), gn_ref[g_next], 0)

    @pl.when(jnp.logical_and(n > 0, s == 1))
    def _():
        x_start(gs_ref[g_next], n_next)

    @pl.when(jnp.logical_and(n > 0, s == 0))
    def _():
        @pl.when(g == 0)
        def _():
            x_start(row0, n)

        def wait(i, c):
            x_copy(row0, i).wait()
            return c

        lax.fori_loop(0, n, wait, 0)

        def unpack(i, c):
            u = xu_ref[rows(i), :]
            lo = pltpu.bitcast(u << 16, F32)
            hi = pltpu.bitcast(u & jnp.uint32(0xFFFF0000), F32)
            xb_ref[rows(i), :half] = lo.astype(BF16)
            xb_ref[rows(i), half:] = hi.astype(BF16)
            return c

        lax.fori_loop(0, n, unpack, 0)

    def out_wait(k):
        buf, sem, m = ((ob_ref.at[0], osem.at[0], 4 * rs), (ob_ref.at[1], osem.at[1], 4 * rs),
                       (op_ref, psem, 2 * rs), (or_ref, rsem, rs))[k]
        pltpu.make_async_copy(buf, ys_ref.at[pl.ds(0, m), 0:td], sem).wait()

    def drain(k):
        @pl.when(flag_ref[k] == 1)
        def _():
            out_wait(k)
            flag_ref[k] = 0

    @pl.when(jnp.logical_and(g == 0, s == 0))
    def _():
        for k in range(4):
            flag_ref[k] = 0

    nquad = lax.shift_right_logical(n, 2)
    has_pair = (n & 2) != 0
    odd = (n & 1) == 1

    def row_slice(r0, m):
        return pl.ds(pl.multiple_of(r0, rs), m)

    @pl.when(jnp.logical_and(n > 0, s < nf))
    def _():
        wgb_ref[...] = wg_ref[...].astype(BF16)
        wub_ref[...] = wu_ref[...].astype(BF16)

        def act_rows(r0, m):
            x = xb_ref[row_slice(r0, m), :]
            gt = jnp.dot(x, wgb_ref[...], preferred_element_type=F32) + bg_ref[...]
            ut = jnp.dot(x, wub_ref[...], preferred_element_type=F32) + bu_ref[...]
            gt = jnp.minimum(gt, SWIGLU_LIMIT)
            ut = jnp.clip(ut, -SWIGLU_LIMIT, SWIGLU_LIMIT)
            act = (ut + 1.0) * (gt * jax.nn.sigmoid(SWIGLU_ALPHA * gt))
            ab_ref[s, row_slice(r0, m), :] = act.astype(BF16)

        def quad(i, c):
            act_rows(i * (4 * rs), 4 * rs)
            return c

        lax.fori_loop(0, nquad, quad, 0)

        @pl.when(has_pair)
        def _():
            act_rows(nquad * (4 * rs), 2 * rs)

        @pl.when(odd)
        def _():
            act_rows((n - 1) * rs, rs)

    @pl.when(jnp.logical_and(n > 0, s >= nf))
    def _():
        d = s - nf
        wdb_ref[...] = wd_ref[...].astype(BF16)

        def out_copy(r0, m, buf, sem, j):
            dst = ys_ref.at[pl.ds(pl.multiple_of(row0 + r0, rs), m), j * td:(j + 1) * td]
            return pltpu.make_async_copy(buf, dst, sem)

        def start_out(r0, m, buf, sem):
            for j in range(nd):
                @pl.when(d == j)
                def _(j=j):
                    out_copy(r0, m, buf, sem, j).start()

        def out_rows(r0, m):
            a_full = jnp.concatenate([ab_ref[f, row_slice(r0, m), :] for f in range(nf)], axis=1)
            return jnp.dot(a_full, wdb_ref[...], preferred_element_type=F32) + bd_ref[...]

        @pl.when(odd)
        def _():
            drain(FLAG_SINGLE)
            or_ref[...] = out_rows((n - 1) * rs, rs)
            start_out((n - 1) * rs, rs, or_ref, rsem)
            flag_ref[FLAG_SINGLE] = 1

        @pl.when(has_pair)
        def _():
            drain(FLAG_PAIR)
            op_ref[...] = out_rows(nquad * (4 * rs), 2 * rs)
            start_out(nquad * (4 * rs), 2 * rs, op_ref, psem)
            flag_ref[FLAG_PAIR] = 1

        def quad(i, c):
            slot = i % 2
            for k in range(2):
                @pl.when(slot == k)
                def _(k=k):
                    drain(k)
                    ob_ref[k] = out_rows(i * (4 * rs), 4 * rs)
                    start_out(i * (4 * rs), 4 * rs, ob_ref.at[k], osem.at[k])
                    flag_ref[k] = 1

            return c

        lax.fori_loop(0, nquad, quad, 0)

    @pl.when(jnp.logical_and(g == pl.num_programs(0) - 1, s == nf + nd - 1))
    def _():
        for k in range(4):
            drain(k)
        z_ref[...] = jnp.zeros(z_ref.shape, z_ref.dtype)

        def z_copy(i):
            dst = ys_ref.at[pl.ds(pl.multiple_of(i * rs, rs), rs), :]
            return pltpu.make_async_copy(z_ref, dst, zsem)

        def start(i, c):
            z_copy(i).start()
            return c

        def wait(i, c):
            z_copy(i).wait()
            return c

        lax.fori_loop(tail_ref[0], tail_ref[1], start, 0)
        lax.fori_loop(tail_ref[0], tail_ref[1], wait, 0)


def _experts(ge, gs, gn, tail, xs, w_gu, b_gu, w_dn, b_dn, rs, rmax, tf, td):
    p = xs.shape[0]
    n_e, d, f2 = w_gu.shape
    fdim = f2 // 2
    nf = fdim // tf
    nd = d // td
    n_groups = ge.shape[0]

    def f_idx(g, s, gn):
        return jnp.where(gn[g] > 0, jnp.minimum(s, nf - 1), nf - 1)

    def d_idx(g, s, gn):
        return jnp.where(gn[g] > 0, jnp.maximum(s - nf, 0), nd - 1)

    return pl.pallas_call(
        functools.partial(_expert_kernel, rs=rs, nf=nf, nd=nd),
        out_shape=jax.ShapeDtypeStruct((p, d), F32),
        grid_spec=pltpu.PrefetchScalarGridSpec(
            num_scalar_prefetch=4,
            grid=(n_groups, nf + nd),
            in_specs=[pl.BlockSpec(memory_space=pl.ANY),
                      pl.BlockSpec((None, d, tf), lambda g, s, ge, gs, gn, tl: (ge[g], 0, f_idx(g, s, gn))),
                      pl.BlockSpec((None, d, tf),
                                   lambda g, s, ge, gs, gn, tl: (ge[g], 0, nf + f_idx(g, s, gn))),
                      pl.BlockSpec((None, 1, tf), lambda g, s, ge, gs, gn, tl: (ge[g], 0, f_idx(g, s, gn))),
                      pl.BlockSpec((None, 1, tf),
                                   lambda g, s, ge, gs, gn, tl: (ge[g], 0, nf + f_idx(g, s, gn))),
                      pl.BlockSpec((None, fdim, td), lambda g, s, ge, gs, gn, tl: (ge[g], 0, d_idx(g, s, gn))),
                      pl.BlockSpec((None, 1, td), lambda g, s, ge, gs, gn, tl: (ge[g], 0, d_idx(g, s, gn)))],
            out_specs=pl.BlockSpec(memory_space=pl.ANY),
            scratch_shapes=[pltpu.VMEM((rmax, d // 2), jnp.uint32),
                            pltpu.VMEM((rmax, d), BF16),
                            pltpu.VMEM((nf, rmax, tf), BF16),
                            pltpu.VMEM((d, tf), BF16), pltpu.VMEM((d, tf), BF16),
                            pltpu.VMEM((fdim, td), BF16),
                            pltpu.VMEM((2, 4 * rs, td), F32),
                            pltpu.VMEM((2 * rs, td), F32),
                            pltpu.VMEM((rs, td), F32),
                            pltpu.VMEM((rs, d), F32),
                            pltpu.SMEM((4,), jnp.int32),
                            pltpu.SemaphoreType.DMA, pltpu.SemaphoreType.DMA((2,)),
                            pltpu.SemaphoreType.DMA, pltpu.SemaphoreType.DMA,
                            pltpu.SemaphoreType.DMA],
        ),
        compiler_params=_cparams("arbitrary", "arbitrary"),
        name="experts",
    )(ge, gs, gn, tail, xs, w_gu, w_gu, b_gu, b_gu, w_dn, b_dn)


def _combine_kernel(dest_ref, ys_ref, pk_ref, x1_ref, gpost_ref, gate_ref, o_ref, buf_ref, sem):
    tm = x1_ref.shape[0]
    base = pl.program_id(0) * (tm * TOP_K)

    def row_copy(t, k, d):
        return pltpu.make_async_copy(ys_ref.at[pl.ds(d, 1), :], buf_ref.at[k, pl.ds(t, 1), :], sem)

    def issue(t, carry):
        for k in range(TOP_K):
            row_copy(t, k, dest_ref[base + t * TOP_K + k]).start(priority=k % 2)
        return carry

    lax.fori_loop(0, tm, issue, 0, unroll=8)

    for k in range(TOP_K):
        pltpu.make_async_copy(ys_ref.at[pl.ds(0, tm), :], buf_ref.at[k], sem).wait()

    pk = pk_ref[...]
    f = jnp.zeros(x1_ref.shape, F32)
    for k in range(TOP_K):
        w = pk[:, 2 * TOP_K + k:2 * TOP_K + k + 1]
        f = f + buf_ref[k] * w
    o_ref[...] = x1_ref[...] + gate_ref[...] * _rms(f, gpost_ref[...])


def _combine(dest, ys, packed, x1, gpost, gate_f, seq, tm):
    t, d = x1.shape
    per_b = seq // tm
    return pl.pallas_call(
        _combine_kernel,
        out_shape=jax.ShapeDtypeStruct((t, d), F32),
        grid_spec=pltpu.PrefetchScalarGridSpec(
            num_scalar_prefetch=1,
            grid=(t // tm,),
            in_specs=[pl.BlockSpec(memory_space=pl.ANY),
                      pl.BlockSpec((tm, LANES), lambda i, dest: (i, 0)),
                      pl.BlockSpec((tm, d), lambda i, dest: (i, 0)),
                      pl.BlockSpec((1, d), lambda i, dest: (0, 0)),
                      pl.BlockSpec((None, 1, d), lambda i, dest: (i // per_b, 0, 0))],
            out_specs=pl.BlockSpec((tm, d), lambda i, dest: (i, 0)),
            scratch_shapes=[pltpu.VMEM((TOP_K, tm, d), F32), pltpu.SemaphoreType.DMA],
        ),
        compiler_params=_cparams("arbitrary"),
        name="combine",
    )(dest, ys, packed, x1, gpost, gate_f)


def _pad_heads(w, n_heads, width):
    k = w.shape[0]
    w = w.reshape(k, n_heads, width)
    return jnp.pad(w, ((0, 0), (0, 0), (0, LANES - width))).reshape(k, n_heads * LANES)


def _layer(x, c, positions, w_ada, b_ada, g_pre_mix, g_post_mix, g_pre_ffn, g_post_ffn,
           w_in, q_a_norm, w_q_b, kv_a_norm, w_kv_b, w_mla_o, ret_decay_fwd, ret_decay_bwd,
           ret_gn, w_ret_o, w_out, w_router, b_router, w_gate_up, b_gate_up, w_down, b_down):
    batch, seq, d = x.shape
    t = batch * seq
    q_rank = q_a_norm.shape[0]
    kv_rank = kv_a_norm.shape[0]
    hd = RET_HEADS * 2 * LANES
    x2 = x.reshape(t, d)

    c_pad = jnp.pad(c, ((0, 8 - batch), (0, 0)))
    ada = _ada(c_pad, w_ada, b_ada[None, :])[:batch]
    shift_m, scale_m, gate_m, shift_f, scale_f, gate_f = [
        ada[:, i * d:(i + 1) * d].reshape(batch, 1, d) for i in range(N_MOD)]

    o_q, o_kv = 0, q_rank
    o_r = q_rank + kv_rank + MLA_ROPE
    w_wide = w_in[:, o_r:]
    w_main = jnp.concatenate([w_wide, w_in[:, o_q:o_q + q_rank], w_in[:, o_kv:o_kv + kv_rank]],
                             axis=1).astype(BF16)
    w_kr = jnp.pad(w_in[:, o_kv + kv_rank:o_r], ((0, 0), (0, LANES - MLA_ROPE))).astype(BF16)
    n_wide = w_wide.shape[1]
    proj, kr_raw = _in_proj(x2, g_pre_mix[None, :], scale_m, shift_m, w_main, w_kr, seq,
                            tm=min(1024, seq), tn=1024)
    blk_w = 2 * LANES
    blk = {"rq": 0, "rk": hd // blk_w, "rv": 2 * hd // blk_w, "rg": 3 * hd // blk_w}
    ga_blk, gb_blk = (4 * hd) // d, (4 * hd + d) // d
    ql_blk, ckv_blk = n_wide // q_rank, (n_wide + q_rank) // kv_rank

    cos_r, sin_r, cm, sm = _rope_tables(positions.reshape(t, 1), tm=min(512, seq))

    wq = w_q_b.reshape(q_rank, MLA_HEADS, MLA_QK)
    wqn = wq[:, :, :MLA_NOPE].reshape(q_rank, MLA_HEADS * MLA_NOPE).astype(BF16)
    wqr = _pad_heads(wq[:, :, MLA_NOPE:].reshape(q_rank, MLA_HEADS * MLA_ROPE), MLA_HEADS,
                     MLA_ROPE).astype(BF16)
    wkv = w_kv_b.reshape(kv_rank, MLA_HEADS, MLA_NOPE + MLA_V)
    wk = wkv[:, :, :MLA_NOPE].reshape(kv_rank, MLA_HEADS * MLA_NOPE).astype(BF16)
    wv = wkv[:, :, MLA_NOPE:].reshape(kv_rank, MLA_HEADS * MLA_V).astype(BF16)
    q, k, v = _mla_prep(proj, kr_raw, cm, sm, q_a_norm[None, :], kv_a_norm[None, :],
                        wqn, wqr, wk, wv, ql_blk, ckv_blk, seq, tm=min(512, seq))
    o_mla = _attention(q, k, v, batch, seq, tq=min(1024, seq), rq=128, tk=min(512, seq))

    dec_f = jnp.broadcast_to(ret_decay_fwd.astype(F32)[:, None, None], (RET_HEADS, 8, LANES))
    dec_b = jnp.broadcast_to(ret_decay_bwd.astype(F32)[:, None, None], (RET_HEADS, 8, LANES))
    o_ret = _retention(proj, cos_r, sin_r, dec_f, dec_b, ret_gn[None, :], batch, seq, blk,
                       ch=min(256, seq))

    m = _mix_gate(o_mla, o_ret, proj, w_mla_o.astype(BF16), w_ret_o.astype(BF16), ga_blk, gb_blk,
                  tm=min(512, seq))
    wr32 = jnp.pad(w_router, ((0, 0), (0, LANES - N_EXPERTS)))
    wr_hi = wr32.astype(BF16)
    wr = jnp.concatenate([wr_hi, (wr32 - wr_hi.astype(F32)).astype(BF16)], axis=1)
    br = jnp.pad(b_router, (0, LANES - N_EXPERTS))[None, :]
    x1, h2, logits = _mix_out(m, x2, w_out.astype(BF16), g_post_mix[None, :], gate_m,
                              g_pre_ffn[None, :], scale_f, shift_f, wr, br, seq, tm=min(256, seq))

    packed, cnt = _route(logits, tm=min(256, seq))
    e_idx = packed[:, 0:TOP_K].astype(jnp.int32)
    rank = packed[:, TOP_K:2 * TOP_K].astype(jnp.int32)
    counts = cnt[0, :N_EXPERTS].astype(jnp.int32)
    a = t * TOP_K
    rs = min(256, a // N_EXPERTS)
    rmax = GROUP_SUB * rs
    n_sub = (a + N_EXPERTS * (rs - 1) + rs - 1) // rs
    n_groups = N_EXPERTS + a // rmax
    padded = ((counts + rs - 1) // rs) * rs
    pend = jnp.cumsum(padded)
    pstart = pend - padded
    dest = (pstart[e_idx] + rank).reshape(a).astype(jnp.int32)
    used_sub = (pend[-1] // rs).astype(jnp.int32).reshape(1)
    tail = jnp.concatenate([used_sub, jnp.full((1,), n_sub, jnp.int32)])
    e_sub = padded // rs
    e_grp = (e_sub + GROUP_SUB - 1) // GROUP_SUB
    g_end = jnp.cumsum(e_grp)
    slot = jnp.arange(n_groups, dtype=jnp.int32)
    g_e = jnp.minimum(jnp.sum((g_end[None, :] <= slot[:, None]).astype(jnp.int32), axis=1),
                      N_EXPERTS - 1)
    g_k = slot - (g_end - e_grp)[g_e]
    live = slot < g_end[-1]
    last_e = g_e[jnp.maximum(g_end[-1] - 1, 0)]
    g_e = jnp.where(live, g_e, last_e).astype(jnp.int32)
    g_s = jnp.where(live, pstart[g_e] + g_k * rmax, 0).astype(jnp.int32)
    g_n = jnp.where(live, jnp.clip(e_sub[g_e] - GROUP_SUB * g_k, 0, GROUP_SUB), 0).astype(jnp.int32)

    xs = _dispatch(dest, (pstart + counts).astype(jnp.int32), (padded - counts).astype(jnp.int32),
                   tail, h2, n_sub * rs, tm=min(256, seq), tm_e=rs)
    fdim = w_down.shape[1]
    ys = _experts(g_e, g_s, g_n, tail, xs, w_gate_up, b_gate_up[:, None, :], w_down,
                  b_down[:, None, :], rs=rs, rmax=rmax, tf=min(256, fdim), td=min(256, d))
    out = _combine(dest, ys, packed, x1, g_post_ffn[None, :], gate_f, seq, tm=min(256, seq))
    return out.reshape(batch, seq, d)


def kernel(x, c, positions, w_ada, b_ada, g_pre_mix, g_post_mix, g_pre_ffn, g_post_ffn, w_in,
           q_a_norm, w_q_b, kv_a_norm, w_kv_b, w_mla_o, ret_decay_fwd, ret_decay_bwd, ret_gn,
           w_ret_o, w_out, w_router, b_router, w_gate_up, b_gate_up, w_down, b_down):
    params = (w_ada, b_ada, g_pre_mix, g_post_mix, g_pre_ffn, g_post_ffn, w_in, q_a_norm, w_q_b,
              kv_a_norm, w_kv_b, w_mla_o, ret_decay_fwd, ret_decay_bwd, ret_gn, w_ret_o, w_out,
              w_router, b_router, w_gate_up, b_gate_up, w_down, b_down)
    for l in range(w_ada.shape[0]):
        x = _layer(x, c, positions, *[p[l] for p in params])
    return x
```

```python
import functools

import numpy as np
import jax
import jax.numpy as jnp
from jax import lax
from jax.experimental import pallas as pl
from jax.experimental.pallas import tpu as pltpu

F32 = jnp.float32
BF16 = jnp.bfloat16

MLA_HEADS = 8
MLA_NOPE = 128
MLA_ROPE = 64
MLA_V = 128
MLA_QK = MLA_NOPE + MLA_ROPE
RET_HEADS = 8
N_EXPERTS = 32
TOP_K = 4
SWIGLU_LIMIT = 7.0
SWIGLU_ALPHA = 1.702
ROPE_THETA = 10000.0
RMS_EPS = 1e-6
GN_EPS = 1e-6
LOG2E = 1.4426950408889634
N_MOD = 6

LANES = 128
SUBLANES = 8
VMEM_LIMIT = 56 * 1024 * 1024
GROUP_SUB = 9
FLAG_PAIR, FLAG_SINGLE = 2, 3


def _cparams(*sem):
    return pltpu.CompilerParams(dimension_semantics=sem, vmem_limit_bytes=VMEM_LIMIT)


def _rms(x, gain):
    return x * lax.rsqrt(jnp.mean(x * x, axis=-1, keepdims=True) + RMS_EPS) * gain


def _ada_kernel(c_ref, w_ref, b_ref, o_ref):
    o_ref[...] = jnp.dot(c_ref[...], w_ref[...], preferred_element_type=F32,
                         precision=lax.Precision.HIGHEST) + b_ref[...]


def _ada(c_pad, w_ada, b_ada, tn=1024):
    m, d = c_pad.shape
    n = w_ada.shape[1]
    return pl.pallas_call(
        _ada_kernel,
        out_shape=jax.ShapeDtypeStruct((m, n), F32),
        grid=(n // tn,),
        in_specs=[pl.BlockSpec((m, d), lambda j: (0, 0)),
                  pl.BlockSpec((d, tn), lambda j: (0, j)),
                  pl.BlockSpec((1, tn), lambda j: (0, j))],
        out_specs=pl.BlockSpec((m, tn), lambda j: (0, j)),
        compiler_params=_cparams("arbitrary"),
        name="ada",
    )(c_pad, w_ada, b_ada)


def _inproj_kernel(x_ref, g_ref, sc_ref, sh_ref, w_ref, wkr_ref, o_ref, okr_ref, h_ref):
    @pl.when(pl.program_id(1) == 0)
    def _():
        h = _rms(x_ref[...], g_ref[...]) * (1.0 + sc_ref[...]) + sh_ref[...]
        hb = h.astype(BF16)
        h_ref[...] = hb
        okr_ref[...] = jnp.dot(hb, wkr_ref[...], preferred_element_type=F32)

    o_ref[...] = jnp.dot(h_ref[...], w_ref[...], preferred_element_type=F32).astype(o_ref.dtype)


def _in_proj(x2, g, scale, shift, w_main, w_kr, seq, tm, tn):
    t, d = x2.shape
    n = w_main.shape[1]
    per_b = seq // tm
    return pl.pallas_call(
        _inproj_kernel,
        out_shape=(jax.ShapeDtypeStruct((t, n), BF16), jax.ShapeDtypeStruct((t, LANES), F32)),
        grid=(t // tm, n // tn),
        in_specs=[pl.BlockSpec((tm, d), lambda i, j: (i, 0)),
                  pl.BlockSpec((1, d), lambda i, j: (0, 0)),
                  pl.BlockSpec((None, 1, d), lambda i, j: (i // per_b, 0, 0)),
                  pl.BlockSpec((None, 1, d), lambda i, j: (i // per_b, 0, 0)),
                  pl.BlockSpec((d, tn), lambda i, j: (0, j)),
                  pl.BlockSpec((d, LANES), lambda i, j: (0, 0))],
        out_specs=(pl.BlockSpec((tm, tn), lambda i, j: (i, j)),
                   pl.BlockSpec((tm, LANES), lambda i, j: (i, 0))),
        scratch_shapes=[pltpu.VMEM((tm, d), BF16)],
        compiler_params=_cparams("arbitrary", "arbitrary"),
        name="in_proj",
    )(x2, g, scale, shift, w_main, w_kr)


def _rope_kernel(pos_ref, invr_ref, invm_ref, cr_ref, sr_ref, cm_ref, sm_ref):
    pos = pos_ref[...].astype(F32)
    ang_r = pos * invr_ref[...]
    cr_ref[...] = jnp.cos(ang_r)
    sr_ref[...] = jnp.sin(ang_r)
    ang_m = pos * invm_ref[...]
    lane = lax.broadcasted_iota(jnp.int32, ang_m.shape, 1)
    half = MLA_ROPE // 2
    cm_ref[...] = jnp.where(lane < MLA_ROPE, jnp.cos(ang_m), 0.0)
    s = jnp.sin(ang_m)
    sm_ref[...] = jnp.where(lane < half, -s, jnp.where(lane < MLA_ROPE, s, 0.0))


def _rope_tables(pos_col, tm):
    t = pos_col.shape[0]
    half_r = LANES
    inv_r = (1.0 / (np.float32(ROPE_THETA) ** (np.arange(half_r, dtype=np.float32) / np.float32(half_r)))
             ).astype(np.float32)
    half_m = MLA_ROPE // 2
    inv_m32 = (1.0 / (np.float32(ROPE_THETA) ** (np.arange(half_m, dtype=np.float32) / np.float32(half_m)))
               ).astype(np.float32)
    inv_m = np.zeros((LANES,), np.float32)
    inv_m[:half_m] = inv_m32
    inv_m[half_m:2 * half_m] = inv_m32
    tab = jax.ShapeDtypeStruct((t, LANES), F32)
    row = pl.BlockSpec((tm, LANES), lambda i: (i, 0))
    return pl.pallas_call(
        _rope_kernel,
        out_shape=(tab, tab, tab, tab),
        grid=(t // tm,),
        in_specs=[pl.BlockSpec((tm, 1), lambda i: (i, 0)),
                  pl.BlockSpec((1, LANES), lambda i: (0, 0)),
                  pl.BlockSpec((1, LANES), lambda i: (0, 0))],
        out_specs=(row, row, row, row),
        compiler_params=_cparams("arbitrary"),
        name="rope_tables",
    )(pos_col, jnp.asarray(inv_r)[None, :], jnp.asarray(inv_m)[None, :])


def _rot64(x, c, s):
    return x * c + (pltpu.roll(x, 32, 1) + pltpu.roll(x, 96, 1)) * s


def _mla_prep_kernel(ql_ref, ckv_ref, kr_ref, cm_ref, sm_ref, gq_ref, gkv_ref,
                     wqn_ref, wqr_ref, wk_ref, wv_ref, q_ref, k_ref, v_ref):
    c = cm_ref[...]
    s = sm_ref[...]
    qn = _rms(ql_ref[...].astype(F32), gq_ref[...]).astype(BF16)
    q_nope = jnp.dot(qn, wqn_ref[...], preferred_element_type=F32)
    q_rope = jnp.dot(qn, wqr_ref[...], preferred_element_type=F32)
    cn = _rms(ckv_ref[...].astype(F32), gkv_ref[...]).astype(BF16)
    k_nope = jnp.dot(cn, wk_ref[...], preferred_element_type=F32)
    v = jnp.dot(cn, wv_ref[...], preferred_element_type=F32)
    k_rot_t = _rot64(kr_ref[...], c, s).T.astype(BF16)
    lane = lax.broadcasted_iota(jnp.int32, c.shape, 1)
    ones_col = jnp.where(lane == 0, 1.0, 0.0).astype(BF16)
    scale = MLA_QK ** -0.5 * LOG2E
    for h in range(MLA_HEADS):
        lo, hi = h * LANES, (h + 1) * LANES
        q_ref[:, 2 * lo:2 * lo + LANES] = (q_nope[:, lo:hi] * scale).astype(BF16)
        q_ref[:, 2 * lo + LANES:2 * hi] = (_rot64(q_rope[:, lo:hi], c, s) * scale).astype(BF16)
        k_ref[2 * lo:2 * lo + LANES, :] = k_nope[:, lo:hi].T.astype(BF16)
        k_ref[2 * lo + LANES:2 * hi, :] = k_rot_t
        v_ref[:, 2 * lo:2 * lo + LANES] = v[:, lo:hi].astype(BF16)
        v_ref[:, 2 * lo + LANES:2 * hi] = ones_col


def _mla_prep(proj, kr_raw, cm, sm, gq, gkv, wqn, wqr, wk, wv, ql_blk, ckv_blk, seq, tm):
    t = proj.shape[0]
    rank = wqn.shape[0]
    hd = MLA_HEADS * LANES
    per_b = seq // tm
    full = lambda shape: pl.BlockSpec(shape, lambda i: (0, 0))
    row = lambda w: pl.BlockSpec((tm, w), lambda i: (i, 0))
    return pl.pallas_call(
        _mla_prep_kernel,
        out_shape=(jax.ShapeDtypeStruct((t, 2 * hd), BF16),
                   jax.ShapeDtypeStruct(((t // seq) * 2 * hd, seq), BF16),
                   jax.ShapeDtypeStruct((t, 2 * hd), BF16)),
        grid=(t // tm,),
        in_specs=[pl.BlockSpec((tm, rank), lambda i: (i, ql_blk)),
                  pl.BlockSpec((tm, rank), lambda i: (i, ckv_blk)),
                  row(LANES), row(LANES), row(LANES),
                  full((1, rank)), full((1, rank)),
                  full((rank, hd)), full((rank, hd)), full((rank, hd)), full((rank, hd))],
        out_specs=(row(2 * hd),
                   pl.BlockSpec((2 * hd, tm), lambda i: (i // per_b, i % per_b)),
                   row(2 * hd)),
        compiler_params=_cparams("arbitrary"),
        name="mla_prep",
    )(proj, proj, kr_raw, cm, sm, gq, gkv, wqn, wqr, wk, wv)


def _attn_kernel(q_ref, kt_ref, v_ref, o_ref, s_ref, *, rq, tk):
    tq = q_ref.shape[0]
    seq = kt_ref.shape[1]
    for r0 in range(0, tq, rq):
        q = q_ref[r0:r0 + rq, :]
        m_lane = None
        for c0 in range(0, seq, tk):
            s = jnp.dot(q, kt_ref[:, c0:c0 + tk], preferred_element_type=F32)
            s_ref[r0:r0 + rq, c0:c0 + tk] = s
            for l0 in range(0, tk, LANES):
                blk = s[:, l0:l0 + LANES]
                m_lane = blk if m_lane is None else jnp.maximum(m_lane, blk)
        m = jnp.max(m_lane, axis=-1, keepdims=True)
        acc = jnp.zeros((rq, 2 * MLA_V), F32)
        for c0 in range(0, seq, tk):
            p = jnp.exp2(s_ref[r0:r0 + rq, c0:c0 + tk] - m).astype(BF16)
            acc = acc + jnp.dot(p, v_ref[c0:c0 + tk, :], preferred_element_type=F32)
        o_ref[r0:r0 + rq, :] = (acc[:, :MLA_V] / acc[:, MLA_V:MLA_V + 1]).astype(o_ref.dtype)


def _attention(q, kt, v, batch, seq, tq, rq, tk):
    t = q.shape[0]
    nq = seq // tq
    return pl.pallas_call(
        functools.partial(_attn_kernel, rq=rq, tk=tk),
        out_shape=jax.ShapeDtypeStruct((t, MLA_HEADS * MLA_V), BF16),
        grid=(batch, MLA_HEADS, nq),
        in_specs=[pl.BlockSpec((tq, 2 * LANES), lambda b, h, i: (b * nq + i, h)),
                  pl.BlockSpec((2 * LANES, seq), lambda b, h, i: (b * MLA_HEADS + h, 0)),
                  pl.BlockSpec((seq, 2 * MLA_V), lambda b, h, i: (b, h))],
        out_specs=pl.BlockSpec((tq, MLA_V), lambda b, h, i: (b * nq + i, h)),
        scratch_shapes=[pltpu.VMEM((tq, seq), F32)],
        compiler_params=_cparams("arbitrary", "arbitrary", "arbitrary"),
        name="mla_attention",
    )(q, kt, v)


def _ret_kernel(rq_ref, rk_ref, rv_ref, rg_ref, cos_ref, sin_ref, df_ref, db_ref, gn_ref,
                o_ref, krot_ref, sf_ref, st_ref, dmat_ref, xif_ref, zf_ref, xib_ref, zb_ref, *, ch):
    seq, dk = rq_ref.shape
    half = dk // 2
    n_ch = seq // ch
    lgf = jnp.log(jax.nn.sigmoid(df_ref[...]))[:1, :1]
    lgb = jnp.log(jax.nn.sigmoid(db_ref[...]))[:1, :1]

    ri = lax.broadcasted_iota(jnp.int32, (ch, ch), 0)
    ci = lax.broadcasted_iota(jnp.int32, (ch, ch), 1)
    diff = (ri - ci).astype(F32)
    dmat_ref[...] = jnp.where(diff >= 0, jnp.exp(lgf * jnp.maximum(diff, 0.0)),
                              jnp.exp(lgb * jnp.maximum(-diff, 0.0)))
    pos = lax.broadcasted_iota(jnp.int32, (ch, dk), 0).astype(F32)
    xif_ref[...] = jnp.exp(lgf * (pos + 1.0))
    zf_ref[...] = jnp.exp(lgf * (ch - 1.0 - pos))
    xib_ref[...] = jnp.exp(lgb * (ch - pos))
    zb_ref[...] = jnp.exp(lgb * pos)
    gcf = jnp.exp(lgf * float(ch))
    gcb = jnp.exp(lgb * float(ch))
    k_scale = dk ** -0.5

    def rot(x, c, s):
        x1, x2 = x[:, :half], x[:, half:]
        return jnp.concatenate([x1 * c - x2 * s, x1 * s + x2 * c], axis=-1)

    def kv_outer(kz, v):
        return lax.dot_general(kz, v, (((0,), (0,)), ((), ())), preferred_element_type=F32)

    st_ref[...] = jnp.zeros(st_ref.shape, F32)

    def fwd(n, carry):
        r0 = pl.multiple_of(n * ch, ch)
        rows = pl.ds(r0, ch)
        k = rot(rk_ref[rows, :].astype(F32), cos_ref[rows, :], sin_ref[rows, :]) * k_scale
        krot_ref[rows, :] = k.astype(BF16)
        sf_ref[n] = st_ref[...].astype(BF16)
        kv = kv_outer((k * zf_ref[...]).astype(BF16), rv_ref[rows, :])
        st_ref[...] = st_ref[...] * gcf + kv
        return carry

    lax.fori_loop(0, n_ch, fwd, 0)

    st_ref[...] = jnp.zeros(st_ref.shape, F32)

    def bwd(i, carry):
        n = n_ch - 1 - i
        r0 = pl.multiple_of(n * ch, ch)
        rows = pl.ds(r0, ch)
        q = rot(rq_ref[rows, :].astype(F32), cos_ref[rows, :], sin_ref[rows, :])
        kb = krot_ref[rows, :]
        v = rv_ref[rows, :]
        s = lax.dot_general(q.astype(BF16), kb, (((1,), (1,)), ((), ())), preferred_element_type=F32)
        o = jnp.dot((s * dmat_ref[...]).astype(BF16), v, preferred_element_type=F32)
        o = o + jnp.dot((q * xif_ref[...]).astype(BF16), sf_ref[n], preferred_element_type=F32)
        o = o + jnp.dot((q * xib_ref[...]).astype(BF16), st_ref[...].astype(BF16),
                        preferred_element_type=F32)
        mu = jnp.mean(o, axis=-1, keepdims=True)
        oc = o - mu
        var = jnp.mean(oc * oc, axis=-1, keepdims=True)
        on = oc * lax.rsqrt(var + GN_EPS)
        g = rg_ref[rows, :].astype(F32)
        o_ref[rows, :] = (g * jax.nn.sigmoid(g) * (on * gn_ref[...])).astype(o_ref.dtype)
        kv = kv_outer((kb.astype(F32) * zb_ref[...]).astype(BF16), v)
        st_ref[...] = st_ref[...] * gcb + kv
        return carry

    lax.fori_loop(0, n_ch, bwd, 0)


def _retention(proj, cos_r, sin_r, dec_f, dec_b, gn, batch, seq, blk, ch):
    t = proj.shape[0]
    dk = 2 * LANES
    n_ch = seq // ch
    col = lambda base: pl.BlockSpec((seq, dk), lambda b, h: (b, base + h))
    tab = pl.BlockSpec((seq, LANES), lambda b, h: (b, 0))
    dec = pl.BlockSpec((None, 8, LANES), lambda b, h: (h, 0, 0))
    return pl.pallas_call(
        functools.partial(_ret_kernel, ch=ch),
        out_shape=jax.ShapeDtypeStruct((t, RET_HEADS * dk), BF16),
        grid=(batch, RET_HEADS),
        in_specs=[col(blk["rq"]), col(blk["rk"]), col(blk["rv"]), col(blk["rg"]),
                  tab, tab, dec, dec,
                  pl.BlockSpec((1, dk), lambda b, h: (0, h))],
        out_specs=pl.BlockSpec((seq, dk), lambda b, h: (b, h)),
        scratch_shapes=[pltpu.VMEM((seq, dk), BF16),
                        pltpu.VMEM((n_ch, dk, dk), BF16),
                        pltpu.VMEM((dk, dk), F32),
                        pltpu.VMEM((ch, ch), F32),
                        pltpu.VMEM((ch, dk), F32), pltpu.VMEM((ch, dk), F32),
                        pltpu.VMEM((ch, dk), F32), pltpu.VMEM((ch, dk), F32)],
        compiler_params=_cparams("arbitrary", "arbitrary"),
        name="retention",
    )(proj, proj, proj, proj, cos_r, sin_r, dec_f, dec_b, gn)


def _mixgate_kernel(oa_ref, ob_ref, ga_ref, gb_ref, wa_ref, wb_ref, m_ref):
    ya = jnp.dot(oa_ref[...], wa_ref[...], preferred_element_type=F32)
    yb = jnp.dot(ob_ref[...], wb_ref[...], preferred_element_type=F32)
    m = jax.nn.sigmoid(ga_ref[...].astype(F32)) * ya + jax.nn.sigmoid(gb_ref[...].astype(F32)) * yb
    m_ref[...] = m.astype(m_ref.dtype)


def _mix_gate(o_mla, o_ret, proj, wa, wb, ga_blk, gb_blk, tm):
    t = o_mla.shape[0]
    d = wa.shape[1]
    return pl.pallas_call(
        _mixgate_kernel,
        out_shape=jax.ShapeDtypeStruct((t, d), BF16),
        grid=(t // tm,),
        in_specs=[pl.BlockSpec((tm, o_mla.shape[1]), lambda i: (i, 0)),
                  pl.BlockSpec((tm, o_ret.shape[1]), lambda i: (i, 0)),
                  pl.BlockSpec((tm, d), lambda i: (i, ga_blk)),
                  pl.BlockSpec((tm, d), lambda i: (i, gb_blk)),
                  pl.BlockSpec(wa.shape, lambda i: (0, 0)),
                  pl.BlockSpec(wb.shape, lambda i: (0, 0))],
        out_specs=pl.BlockSpec((tm, d), lambda i: (i, 0)),
        compiler_params=_cparams("arbitrary"),
        name="mix_gate",
    )(o_mla, o_ret, proj, proj, wa, wb)


def _mixout_kernel(m_ref, x_ref, wo_ref, gpost_ref, gate_ref, gpre_ref, sc_ref, sh_ref,
                   wr_ref, br_ref, x1_ref, h2_ref, lg_ref):
    mix = jnp.dot(m_ref[...], wo_ref[...], preferred_element_type=F32)
    x1 = x_ref[...] + gate_ref[...] * _rms(mix, gpost_ref[...])
    x1_ref[...] = x1
    h2 = _rms(x1, gpre_ref[...]) * (1.0 + sc_ref[...]) + sh_ref[...]
    half = h2.shape[1] // 2
    bits = pltpu.bitcast(h2.astype(BF16).astype(F32), jnp.uint32)
    h2_ref[...] = (bits[:, :half] >> 16) | (bits[:, half:] & jnp.uint32(0xFFFF0000))
    h_hi = h2.astype(BF16)
    h_lo = (h2 - h_hi.astype(F32)).astype(BF16)
    wr = wr_ref[...]
    hh = jnp.dot(h_hi, wr, preferred_element_type=F32)
    lh = jnp.dot(h_lo, wr[:, :LANES], preferred_element_type=F32)
    lg_ref[...] = hh[:, :LANES] + (hh[:, LANES:] + lh) + br_ref[...]


def _mix_out(m, x2, wo, gpost, gate_m, gpre, scale_f, shift_f, wr, br, seq, tm):
    t, d = x2.shape
    per_b = seq // tm
    row = pl.BlockSpec((tm, d), lambda i: (i, 0))
    vec = pl.BlockSpec((1, d), lambda i: (0, 0))
    mod = pl.BlockSpec((None, 1, d), lambda i: (i // per_b, 0, 0))
    return pl.pallas_call(
        _mixout_kernel,
        out_shape=(jax.ShapeDtypeStruct((t, d), F32), jax.ShapeDtypeStruct((t, d // 2), jnp.uint32),
                   jax.ShapeDtypeStruct((t, LANES), F32)),
        grid=(t // tm,),
        in_specs=[row, row, pl.BlockSpec((d, d), lambda i: (0, 0)), vec, mod, vec, mod, mod,
                  pl.BlockSpec((d, 2 * LANES), lambda i: (0, 0)),
                  pl.BlockSpec((1, LANES), lambda i: (0, 0))],
        out_specs=(row, pl.BlockSpec((tm, d // 2), lambda i: (i, 0)),
                   pl.BlockSpec((tm, LANES), lambda i: (i, 0))),
        compiler_params=_cparams("arbitrary"),
        name="mix_out",
    )(m, x2, wo, gpost, gate_m, gpre, scale_f, shift_f, wr, br)


def _route_kernel(lg_ref, o_ref, cnt_ref, carry_ref):
    @pl.when(pl.program_id(0) == 0)
    def _():
        carry_ref[...] = jnp.zeros(carry_ref.shape, F32)

    tm = lg_ref.shape[0]
    lane = lax.broadcasted_iota(jnp.int32, (tm, LANES), 1).astype(F32)
    l = jnp.where(lane < N_EXPERTS, lg_ref[...], -jnp.inf)
    vals, idxs = [], []
    for _ in range(TOP_K):
        mx = jnp.max(l, axis=-1, keepdims=True)
        ik = jnp.min(jnp.where(l == mx, lane, float(LANES)), axis=-1, keepdims=True)
        vals.append(mx)
        idxs.append(ik)
        l = jnp.where(lane == ik, -jnp.inf, l)
    es = [jnp.exp(v - vals[0]) for v in vals]
    den = es[0] + es[1] + es[2] + es[3]
    onehot = jnp.zeros((tm, LANES), F32)
    for ik in idxs:
        onehot = onehot + jnp.where(lane == ik, 1.0, 0.0)
    ri = lax.broadcasted_iota(jnp.int32, (tm, tm), 0)
    ci = lax.broadcasted_iota(jnp.int32, (tm, tm), 1)
    tri = jnp.where(ci < ri, 1.0, 0.0).astype(BF16)
    cum = jnp.dot(tri, onehot.astype(BF16), preferred_element_type=F32) + carry_ref[0:1, :]
    packed = jnp.zeros((tm, LANES), F32)
    for k in range(TOP_K):
        rank = jnp.sum(jnp.where(lane == idxs[k], cum, 0.0), axis=-1, keepdims=True)
        packed = packed + jnp.where(lane == float(k), idxs[k], 0.0)
        packed = packed + jnp.where(lane == float(TOP_K + k), rank, 0.0)
        packed = packed + jnp.where(lane == float(2 * TOP_K + k), es[k] / den, 0.0)
    o_ref[...] = packed
    carry_ref[...] = carry_ref[...] + jnp.sum(onehot, axis=0, keepdims=True)
    cnt_ref[...] = carry_ref[...]


def _route(logits, tm):
    t = logits.shape[0]
    return pl.pallas_call(
        _route_kernel,
        out_shape=(jax.ShapeDtypeStruct((t, LANES), F32), jax.ShapeDtypeStruct((8, LANES), F32)),
        grid=(t // tm,),
        in_specs=[pl.BlockSpec((tm, LANES), lambda i: (i, 0))],
        out_specs=(pl.BlockSpec((tm, LANES), lambda i: (i, 0)),
                   pl.BlockSpec((8, LANES), lambda i: (0, 0))),
        scratch_shapes=[pltpu.VMEM((8, LANES), F32)],
        compiler_params=_cparams("arbitrary"),
        name="route",
    )(logits)


def _dispatch_kernel(dest_ref, pad_pos_ref, pad_len_ref, tail_ref, h_ref, xs_ref, z_ref, sem, zsem,
                     *, tm_e):
    tm = h_ref.shape[0]
    zr = z_ref.shape[0]
    base = pl.program_id(0) * (tm * TOP_K)
    pieces = [p for p in (2 ** i for i in range(20)) if SUBLANES <= p <= zr and p < tm_e][::-1]

    def zero_copy(pos, p):
        return pltpu.make_async_copy(z_ref.at[pl.ds(0, p), :], xs_ref.at[pl.ds(pos, p), :], zsem)

    def pad_rows(act):
        for e in range(N_EXPERTS):
            pos = pad_pos_ref[e]
            head = (-pos) & (SUBLANES - 1)
            for j in range(SUBLANES - 1):
                @pl.when(j < head)
                def _(pos=pos, j=j):
                    act(zero_copy(pos + j, 1))

            pos = pos + head
            ln = pad_len_ref[e] - head
            for p in pieces:
                hit = (ln & p) != 0

                @pl.when(hit)
                def _(pos=pos, p=p):
                    act(zero_copy(pl.multiple_of(pos, SUBLANES), p))

                pos = pos + jnp.where(hit, p, 0)

    def tail_rows(act):
        def body(j, carry):
            for r0 in range(0, tm_e, zr):
                act(zero_copy(pl.multiple_of(j * tm_e + r0, SUBLANES), zr))
            return carry

        lax.fori_loop(tail_ref[0], tail_ref[1], body, 0)

    @pl.when(pl.program_id(0) == 0)
    def _():
        z_ref[...] = jnp.zeros(z_ref.shape, z_ref.dtype)
        pad_rows(lambda c: c.start())
        tail_rows(lambda c: c.start())

    def row_copy(t, d):
        return pltpu.make_async_copy(h_ref.at[pl.ds(t, 1), :], xs_ref.at[pl.ds(d, 1), :], sem)

    def issue(t, carry):
        for k in range(TOP_K):
            row_copy(t, dest_ref[base + t * TOP_K + k]).start(priority=k % 2)
        return carry

    lax.fori_loop(0, tm, issue, 0, unroll=8)

    for k in range(TOP_K):
        pltpu.make_async_copy(h_ref, xs_ref.at[pl.ds(0, tm), :], sem).wait()

    @pl.when(pl.program_id(0) == 0)
    def _():
        pad_rows(lambda c: c.wait())
        tail_rows(lambda c: c.wait())


def _dispatch(dest, pad_pos, pad_len, tail, h2, n_rows, tm, tm_e):
    t, d = h2.shape
    zr = min(256, tm_e)
    nsp = 4
    return pl.pallas_call(
        functools.partial(_dispatch_kernel, tm_e=tm_e),
        out_shape=jax.ShapeDtypeStruct((n_rows, d), h2.dtype),
        grid_spec=pltpu.PrefetchScalarGridSpec(
            num_scalar_prefetch=nsp,
            grid=(t // tm,),
            in_specs=[pl.BlockSpec((tm, d), lambda i, *_: (i, 0))],
            out_specs=pl.BlockSpec(memory_space=pl.ANY),
            scratch_shapes=[pltpu.VMEM((zr, d), h2.dtype), pltpu.SemaphoreType.DMA,
                            pltpu.SemaphoreType.DMA],
        ),
        compiler_params=_cparams("arbitrary"),
        name="dispatch",
    )(dest, pad_pos, pad_len, tail, h2)


def _expert_kernel(ge_ref, gs_ref, gn_ref, tail_ref, xs_ref, wg_ref, wu_ref, bg_ref, bu_ref,
                   wd_ref, bd_ref, ys_ref, xu_ref, xb_ref, ab_ref, wgb_ref, wub_ref, wdb_ref,
                   ob_ref, op_ref, or_ref, z_ref, flag_ref, xsem, osem, psem, rsem, zsem, *, rs, nf, nd):
    del ge_ref
    g = pl.program_id(0)
    s = pl.program_id(1)
    n = gn_ref[g]
    row0 = gs_ref[g]
    half = xu_ref.shape[1]
    tf = wg_ref.shape[1]
    td = wd_ref.shape[1]

    def rows(i):
        return pl.ds(pl.multiple_of(i * rs, rs), rs)

    def x_copy(base, i):
        src = xs_ref.at[pl.ds(pl.multiple_of(base + i * rs, rs), rs), :]
        return pltpu.make_async_copy(src, xu_ref.at[rows(i), :], xsem)

    def x_start(base, cnt):
        def start(i, c):
            x_copy(base, i).start()
            return c

        lax.fori_loop(0, cnt, start, 0)

    g_next = jnp.minimum(g + 1, pl.num_programs(0) - 1)
    n_next = jnp.where(g + 1 < pl.num_programs(0), gn_ref[g_next], 0)

    @pl.when(jnp.logical_and(n > 0, s == 1))
    def _():
        x_start(gs_ref[g_next], n_next)

    @pl.when(jnp.logical_and(n > 0, s == 0))
    def _():
        @pl.when(g == 0)
        def _():
            x_start(row0, n)

        def wait(i, c):
            x_copy(row0, i).wait()
            return c

        lax.fori_loop(0, n, wait, 0)

        def unpack(i, c):
            u = xu_ref[rows(i), :]
            lo = pltpu.bitcast(u << 16, F32)
            hi = pltpu.bitcast(u & jnp.uint32(0xFFFF0000), F32)
            xb_ref[rows(i), :half] = lo.astype(BF16)
            xb_ref[rows(i), half:] = hi.astype(BF16)
            return c

        lax.fori_loop(0, n, unpack, 0)

    def out_wait(k):
        buf, sem, m = ((ob_ref.at[0], osem.at[0], 4 * rs), (ob_ref.at[1], osem.at[1], 4 * rs),
                       (op_ref, psem, 2 * rs), (or_ref, rsem, rs))[k]
        pltpu.make_async_copy(buf, ys_ref.at[pl.ds(0, m), 0:td], sem).wait()

    def drain(k):
        @pl.when(flag_ref[k] == 1)
        def _():
            out_wait(k)
            flag_ref[k] = 0

    @pl.when(jnp.logical_and(g == 0, s == 0))
    def _():
        for k in range(4):
            flag_ref[k] = 0

    nquad = lax.shift_right_logical(n, 2)
    has_pair = (n & 2) != 0
    odd = (n & 1) == 1

    def row_slice(r0, m):
        return pl.ds(pl.multiple_of(r0, rs), m)

    @pl.when(jnp.logical_and(n > 0, s < nf))
    def _():
        wgb_ref[...] = wg_ref[...].astype(BF16)
        wub_ref[...] = wu_ref[...].astype(BF16)

        def act_rows(r0, m):
            x = xb_ref[row_slice(r0, m), :]
            gt = jnp.dot(x, wgb_ref[...], preferred_element_type=F32) + bg_ref[...]
            ut = jnp.dot(x, wub_ref[...], preferred_element_type=F32) + bu_ref[...]
            gt = jnp.minimum(gt, SWIGLU_LIMIT)
            ut = jnp.clip(ut, -SWIGLU_LIMIT, SWIGLU_LIMIT)
            act = (ut + 1.0) * (gt * jax.nn.sigmoid(SWIGLU_ALPHA * gt))
            ab_ref[s, row_slice(r0, m), :] = act.astype(BF16)

        def quad(i, c):
            act_rows(i * (4 * rs), 4 * rs)
            return c

        lax.fori_loop(0, nquad, quad, 0)

        @pl.when(has_pair)
        def _():
            act_rows(nquad * (4 * rs), 2 * rs)

        @pl.when(odd)
        def _():
            act_rows((n - 1) * rs, rs)

    @pl.when(jnp.logical_and(n > 0, s >= nf))
    def _():
        d = s - nf
        wdb_ref[...] = wd_ref[...].astype(BF16)

        def out_copy(r0, m, buf, sem, j):
            dst = ys_ref.at[pl.ds(pl.multiple_of(row0 + r0, rs), m), j * td:(j + 1) * td]
            return pltpu.make_async_copy(buf, dst, sem)

        def start_out(r0, m, buf, sem):
            for j in range(nd):
                @pl.when(d == j)
                def _(j=j):
                    out_copy(r0, m, buf, sem, j).start()

        def out_rows(r0, m):
            a_full = jnp.concatenate([ab_ref[f, row_slice(r0, m), :] for f in range(nf)], axis=1)
            return jnp.dot(a_full, wdb_ref[...], preferred_element_type=F32) + bd_ref[...]

        @pl.when(odd)
        def _():
            drain(FLAG_SINGLE)
            or_ref[...] = out_rows((n - 1) * rs, rs)
            start_out((n - 1) * rs, rs, or_ref, rsem)
            flag_ref[FLAG_SINGLE] = 1

        @pl.when(has_pair)
        def _():
            drain(FLAG_PAIR)
            op_ref[...] = out_rows(nquad * (4 * rs), 2 * rs)
            start_out(nquad * (4 * rs), 2 * rs, op_ref, psem)
            flag_ref[FLAG_PAIR] = 1

        def quad(i, c):
            slot = i % 2
            for k in range(2):
                @pl.when(slot == k)
                def _(k=k):
                    drain(k)
                    ob_ref[k] = out_rows(i * (4 * rs), 4 * rs)
                    start_out(i * (4 * rs), 4 * rs, ob_ref.at[k], osem.at[k])
                    flag_ref[k] = 1

            return c

        lax.fori_loop(0, nquad, quad, 0)

    @pl.when(jnp.logical_and(g == pl.num_programs(0) - 1, s == nf + nd - 1))
    def _():
        for k in range(4):
            drain(k)
        z_ref[...] = jnp.zeros(z_ref.shape, z_ref.dtype)

        def z_copy(i):
            dst = ys_ref.at[pl.ds(pl.multiple_of(i * rs, rs), rs), :]
            return pltpu.make_async_copy(z_ref, dst, zsem)

        def start(i, c):
            z_copy(i).start()
            return c

        def wait(i, c):
            z_copy(i).wait()
            return c

        lax.fori_loop(tail_ref[0], tail_ref[1], start, 0)
        lax.fori_loop(tail_ref[0], tail_ref[1], wait, 0)


def _experts(ge, gs, gn, tail, xs, w_gu, b_gu, w_dn, b_dn, rs, rmax, tf, td):
    p = xs.shape[0]
    n_e, d, f2 = w_gu.shape
    fdim = f2 // 2
    nf = fdim // tf
    nd = d // td
    n_groups = ge.shape[0]

    def f_idx(g, s, gn):
        return jnp.where(gn[g] > 0, jnp.minimum(s, nf - 1), nf - 1)

    def d_idx(g, s, gn):
        return jnp.where(gn[g] > 0, jnp.maximum(s - nf, 0), nd - 1)

    return pl.pallas_call(
        functools.partial(_expert_kernel, rs=rs, nf=nf, nd=nd),
        out_shape=jax.ShapeDtypeStruct((p, d), F32),
        grid_spec=pltpu.PrefetchScalarGridSpec(
            num_scalar_prefetch=4,
            grid=(n_groups, nf + nd),
            in_specs=[pl.BlockSpec(memory_space=pl.ANY),
                      pl.BlockSpec((None, d, tf), lambda g, s, ge, gs, gn, tl: (ge[g], 0, f_idx(g, s, gn))),
                      pl.BlockSpec((None, d, tf),
                                   lambda g, s, ge, gs, gn, tl: (ge[g], 0, nf + f_idx(g, s, gn))),
                      pl.BlockSpec((None, 1, tf), lambda g, s, ge, gs, gn, tl: (ge[g], 0, f_idx(g, s, gn))),
                      pl.BlockSpec((None, 1, tf),
                                   lambda g, s, ge, gs, gn, tl: (ge[g], 0, nf + f_idx(g, s, gn))),
                      pl.BlockSpec((None, fdim, td), lambda g, s, ge, gs, gn, tl: (ge[g], 0, d_idx(g, s, gn))),
                      pl.BlockSpec((None, 1, td), lambda g, s, ge, gs, gn, tl: (ge[g], 0, d_idx(g, s, gn)))],
            out_specs=pl.BlockSpec(memory_space=pl.ANY),
            scratch_shapes=[pltpu.VMEM((rmax, d // 2), jnp.uint32),
                            pltpu.VMEM((rmax, d), BF16),
                            pltpu.VMEM((nf, rmax, tf), BF16),
                            pltpu.VMEM((d, tf), BF16), pltpu.VMEM((d, tf), BF16),
                            pltpu.VMEM((fdim, td), BF16),
                            pltpu.VMEM((2, 4 * rs, td), F32),
                            pltpu.VMEM((2 * rs, td), F32),
                            pltpu.VMEM((rs, td), F32),
                            pltpu.VMEM((rs, d), F32),
                            pltpu.SMEM((4,), jnp.int32),
                            pltpu.SemaphoreType.DMA, pltpu.SemaphoreType.DMA((2,)),
                            pltpu.SemaphoreType.DMA, pltpu.SemaphoreType.DMA,
                            pltpu.SemaphoreType.DMA],
        ),
        compiler_params=_cparams("arbitrary", "arbitrary"),
        name="experts",
    )(ge, gs, gn, tail, xs, w_gu, w_gu, b_gu, b_gu, w_dn, b_dn)


def _combine_kernel(dest_ref, ys_ref, pk_ref, x1_ref, gpost_ref, gate_ref, o_ref, buf_ref, sem):
    tm = x1_ref.shape[0]
    base = pl.program_id(0) * (tm * TOP_K)

    def row_copy(t, k, d):
        return pltpu.make_async_copy(ys_ref.at[pl.ds(d, 1), :], buf_ref.at[k, pl.ds(t, 1), :], sem)

    def issue(t, carry):
        for k in range(TOP_K):
            row_copy(t, k, dest_ref[base + t * TOP_K + k]).start(priority=k % 2)
        return carry

    lax.fori_loop(0, tm, issue, 0, unroll=8)

    for k in range(TOP_K):
        pltpu.make_async_copy(ys_ref.at[pl.ds(0, tm), :], buf_ref.at[k], sem).wait()

    pk = pk_ref[...]
    f = jnp.zeros(x1_ref.shape, F32)
    for k in range(TOP_K):
        w = pk[:, 2 * TOP_K + k:2 * TOP_K + k + 1]
        f = f + buf_ref[k] * w
    o_ref[...] = x1_ref[...] + gate_ref[...] * _rms(f, gpost_ref[...])


def _combine(dest, ys, packed, x1, gpost, gate_f, seq, tm):
    t, d = x1.shape
    per_b = seq // tm
    return pl.pallas_call(
        _combine_kernel,
        out_shape=jax.ShapeDtypeStruct((t, d), F32),
        grid_spec=pltpu.PrefetchScalarGridSpec(
            num_scalar_prefetch=1,
            grid=(t // tm,),
            in_specs=[pl.BlockSpec(memory_space=pl.ANY),
                      pl.BlockSpec((tm, LANES), lambda i, dest: (i, 0)),
                      pl.BlockSpec((tm, d), lambda i, dest: (i, 0)),
                      pl.BlockSpec((1, d), lambda i, dest: (0, 0)),
                      pl.BlockSpec((None, 1, d), lambda i, dest: (i // per_b, 0, 0))],
            out_specs=pl.BlockSpec((tm, d), lambda i, dest: (i, 0)),
            scratch_shapes=[pltpu.VMEM((TOP_K, tm, d), F32), pltpu.SemaphoreType.DMA],
        ),
        compiler_params=_cparams("arbitrary"),
        name="combine",
    )(dest, ys, packed, x1, gpost, gate_f)


def _pad_heads(w, n_heads, width):
    k = w.shape[0]
    w = w.reshape(k, n_heads, width)
    return jnp.pad(w, ((0, 0), (0, 0), (0, LANES - width))).reshape(k, n_heads * LANES)


def _layer(x, c, positions, w_ada, b_ada, g_pre_mix, g_post_mix, g_pre_ffn, g_post_ffn,
           w_in, q_a_norm, w_q_b, kv_a_norm, w_kv_b, w_mla_o, ret_decay_fwd, ret_decay_bwd,
           ret_gn, w_ret_o, w_out, w_router, b_router, w_gate_up, b_gate_up, w_down, b_down):
    batch, seq, d = x.shape
    t = batch * seq
    q_rank = q_a_norm.shape[0]
    kv_rank = kv_a_norm.shape[0]
    hd = RET_HEADS * 2 * LANES
    x2 = x.reshape(t, d)

    c_pad = jnp.pad(c, ((0, 8 - batch), (0, 0)))
    ada = _ada(c_pad, w_ada, b_ada[None, :])[:batch]
    shift_m, scale_m, gate_m, shift_f, scale_f, gate_f = [
        ada[:, i * d:(i + 1) * d].reshape(batch, 1, d) for i in range(N_MOD)]

    o_q, o_kv = 0, q_rank
    o_r = q_rank + kv_rank + MLA_ROPE
    w_wide = w_in[:, o_r:]
    w_main = jnp.concatenate([w_wide, w_in[:, o_q:o_q + q_rank], w_in[:, o_kv:o_kv + kv_rank]],
                             axis=1).astype(BF16)
    w_kr = jnp.pad(w_in[:, o_kv + kv_rank:o_r], ((0, 0), (0, LANES - MLA_ROPE))).astype(BF16)
    n_wide = w_wide.shape[1]
    proj, kr_raw = _in_proj(x2, g_pre_mix[None, :], scale_m, shift_m, w_main, w_kr, seq,
                            tm=min(1024, seq), tn=1024)
    blk_w = 2 * LANES
    blk = {"rq": 0, "rk": hd // blk_w, "rv": 2 * hd // blk_w, "rg": 3 * hd // blk_w}
    ga_blk, gb_blk = (4 * hd) // d, (4 * hd + d) // d
    ql_blk, ckv_blk = n_wide // q_rank, (n_wide + q_rank) // kv_rank

    cos_r, sin_r, cm, sm = _rope_tables(positions.reshape(t, 1), tm=min(512, seq))

    wq = w_q_b.reshape(q_rank, MLA_HEADS, MLA_QK)
    wqn = wq[:, :, :MLA_NOPE].reshape(q_rank, MLA_HEADS * MLA_NOPE).astype(BF16)
    wqr = _pad_heads(wq[:, :, MLA_NOPE:].reshape(q_rank, MLA_HEADS * MLA_ROPE), MLA_HEADS,
                     MLA_ROPE).astype(BF16)
    wkv = w_kv_b.reshape(kv_rank, MLA_HEADS, MLA_NOPE + MLA_V)
    wk = wkv[:, :, :MLA_NOPE].reshape(kv_rank, MLA_HEADS * MLA_NOPE).astype(BF16)
    wv = wkv[:, :, MLA_NOPE:].reshape(kv_rank, MLA_HEADS * MLA_V).astype(BF16)
    q, k, v = _mla_prep(proj, kr_raw, cm, sm, q_a_norm[None, :], kv_a_norm[None, :],
                        wqn, wqr, wk, wv, ql_blk, ckv_blk, seq, tm=min(512, seq))
    o_mla = _attention(q, k, v, batch, seq, tq=min(2048, seq), rq=128, tk=min(512, seq))

    dec_f = jnp.broadcast_to(ret_decay_fwd.astype(F32)[:, None, None], (RET_HEADS, 8, LANES))
    dec_b = jnp.broadcast_to(ret_decay_bwd.astype(F32)[:, None, None], (RET_HEADS, 8, LANES))
    o_ret = _retention(proj, cos_r, sin_r, dec_f, dec_b, ret_gn[None, :], batch, seq, blk,
                       ch=min(256, seq))

    m = _mix_gate(o_mla, o_ret, proj, w_mla_o.astype(BF16), w_ret_o.astype(BF16), ga_blk, gb_blk,
                  tm=min(512, seq))
    wr32 = jnp.pad(w_router, ((0, 0), (0, LANES - N_EXPERTS)))
    wr_hi = wr32.astype(BF16)
    wr = jnp.concatenate([wr_hi, (wr32 - wr_hi.astype(F32)).astype(BF16)], axis=1)
    br = jnp.pad(b_router, (0, LANES - N_EXPERTS))[None, :]
    x1, h2, logits = _mix_out(m, x2, w_out.astype(BF16), g_post_mix[None, :], gate_m,
                              g_pre_ffn[None, :], scale_f, shift_f, wr, br, seq, tm=min(256, seq))

    packed, cnt = _route(logits, tm=min(256, seq))
    e_idx = packed[:, 0:TOP_K].astype(jnp.int32)
    rank = packed[:, TOP_K:2 * TOP_K].astype(jnp.int32)
    counts = cnt[0, :N_EXPERTS].astype(jnp.int32)
    a = t * TOP_K
    rs = min(256, a // N_EXPERTS)
    rmax = GROUP_SUB * rs
    n_sub = (a + N_EXPERTS * (rs - 1) + rs - 1) // rs
    n_groups = N_EXPERTS + a // rmax
    padded = ((counts + rs - 1) // rs) * rs
    pend = jnp.cumsum(padded)
    pstart = pend - padded
    dest = (pstart[e_idx] + rank).reshape(a).astype(jnp.int32)
    used_sub = (pend[-1] // rs).astype(jnp.int32).reshape(1)
    tail = jnp.concatenate([used_sub, jnp.full((1,), n_sub, jnp.int32)])
    e_sub = padded // rs
    e_grp = (e_sub + GROUP_SUB - 1) // GROUP_SUB
    g_end = jnp.cumsum(e_grp)
    slot = jnp.arange(n_groups, dtype=jnp.int32)
    g_e = jnp.minimum(jnp.sum((g_end[None, :] <= slot[:, None]).astype(jnp.int32), axis=1),
                      N_EXPERTS - 1)
    g_k = slot - (g_end - e_grp)[g_e]
    live = slot < g_end[-1]
    last_e = g_e[jnp.maximum(g_end[-1] - 1, 0)]
    g_e = jnp.where(live, g_e, last_e).astype(jnp.int32)
    e_chunk = (e_sub + jnp.maximum(e_grp, 1) - 1) // jnp.maximum(e_grp, 1)
    g_s = jnp.where(live, pstart[g_e] + g_k * e_chunk[g_e] * rs, 0).astype(jnp.int32)
    g_n = jnp.where(live, jnp.clip(e_sub[g_e] - e_chunk[g_e] * g_k, 0, e_chunk[g_e]), 0).astype(jnp.int32)

    xs = _dispatch(dest, (pstart + counts).astype(jnp.int32), (padded - counts).astype(jnp.int32),
                   tail, h2, n_sub * rs, tm=min(256, seq), tm_e=rs)
    fdim = w_down.shape[1]
    ys = _experts(g_e, g_s, g_n, tail, xs, w_gate_up, b_gate_up[:, None, :], w_down,
                  b_down[:, None, :], rs=rs, rmax=rmax, tf=min(256, fdim), td=min(256, d))
    out = _combine(dest, ys, packed, x1, g_post_ffn[None, :], gate_f, seq, tm=min(256, seq))
    return out.reshape(batch, seq, d)


def kernel(x, c, positions, w_ada, b_ada, g_pre_mix, g_post_mix, g_pre_ffn, g_post_ffn, w_in,
           q_a_norm, w_q_b, kv_a_norm, w_kv_b, w_mla_o, ret_decay_fwd, ret_decay_bwd, ret_gn,
           w_ret_o, w_out, w_router, b_router, w_gate_up, b_gate_up, w_down, b_down):
    params = (w_ada, b_ada, g_pre_mix, g_post_mix, g_pre_ffn, g_post_ffn, w_in, q_a_norm, w_q_b,
              kv_a_norm, w_kv_b, w_mla_o, ret_decay_fwd, ret_decay_bwd, ret_gn, w_ret_o, w_out,
              w_router, b_router, w_gate_up, b_gate_up, w_down, b_down)
    for l in range(w_ada.shape[0]):
        x = _layer(x, c, positions, *[p[l] for p in params])
    return x
```

```python
import functools

import numpy as np
import jax
import jax.numpy as jnp
from jax import lax
from jax.experimental import pallas as pl
from jax.experimental.pallas import tpu as pltpu

F32 = jnp.float32
BF16 = jnp.bfloat16

MLA_HEADS = 8
MLA_NOPE = 128
MLA_ROPE = 64
MLA_V = 128
MLA_QK = MLA_NOPE + MLA_ROPE
RET_HEADS = 8
N_EXPERTS = 32
TOP_K = 4
SWIGLU_LIMIT = 7.0
SWIGLU_ALPHA = 1.702
ROPE_THETA = 10000.0
RMS_EPS = 1e-6
GN_EPS = 1e-6
LOG2E = 1.4426950408889634
N_MOD = 6

LANES = 128
SUBLANES = 8
VMEM_LIMIT = 56 * 1024 * 1024
GROUP_SUB = 9
FLAG_PAIR, FLAG_SINGLE = 2, 3


def _cparams(*sem):
    return pltpu.CompilerParams(dimension_semantics=sem, vmem_limit_bytes=VMEM_LIMIT)


def _rms(x, gain):
    return x * lax.rsqrt(jnp.mean(x * x, axis=-1, keepdims=True) + RMS_EPS) * gain


def _ada_kernel(c_ref, w_ref, b_ref, o_ref):
    o_ref[...] = jnp.dot(c_ref[...], w_ref[...], preferred_element_type=F32,
                         precision=lax.Precision.HIGHEST) + b_ref[...]


def _ada(c_pad, w_ada, b_ada, tn=1024):
    m, d = c_pad.shape
    n = w_ada.shape[1]
    return pl.pallas_call(
        _ada_kernel,
        out_shape=jax.ShapeDtypeStruct((m, n), F32),
        grid=(n // tn,),
        in_specs=[pl.BlockSpec((m, d), lambda j: (0, 0)),
                  pl.BlockSpec((d, tn), lambda j: (0, j)),
                  pl.BlockSpec((1, tn), lambda j: (0, j))],
        out_specs=pl.BlockSpec((m, tn), lambda j: (0, j)),
        compiler_params=_cparams("arbitrary"),
        name="ada",
    )(c_pad, w_ada, b_ada)


def _inproj_kernel(x_ref, g_ref, sc_ref, sh_ref, w_ref, wkr_ref, o_ref, okr_ref, h_ref):
    @pl.when(pl.program_id(1) == 0)
    def _():
        h = _rms(x_ref[...], g_ref[...]) * (1.0 + sc_ref[...]) + sh_ref[...]
        hb = h.astype(BF16)
        h_ref[...] = hb
        okr_ref[...] = jnp.dot(hb, wkr_ref[...], preferred_element_type=F32)

    o_ref[...] = jnp.dot(h_ref[...], w_ref[...], preferred_element_type=F32).astype(o_ref.dtype)


def _in_proj(x2, g, scale, shift, w_main, w_kr, seq, tm, tn):
    t, d = x2.shape
    n = w_main.shape[1]
    per_b = seq // tm
    return pl.pallas_call(
        _inproj_kernel,
        out_shape=(jax.ShapeDtypeStruct((t, n), BF16), jax.ShapeDtypeStruct((t, LANES), F32)),
        grid=(t // tm, n // tn),
        in_specs=[pl.BlockSpec((tm, d), lambda i, j: (i, 0)),
                  pl.BlockSpec((1, d), lambda i, j: (0, 0)),
                  pl.BlockSpec((None, 1, d), lambda i, j: (i // per_b, 0, 0)),
                  pl.BlockSpec((None, 1, d), lambda i, j: (i // per_b, 0, 0)),
                  pl.BlockSpec((d, tn), lambda i, j: (0, j)),
                  pl.BlockSpec((d, LANES), lambda i, j: (0, 0))],
        out_specs=(pl.BlockSpec((tm, tn), lambda i, j: (i, j)),
                   pl.BlockSpec((tm, LANES), lambda i, j: (i, 0))),
        scratch_shapes=[pltpu.VMEM((tm, d), BF16)],
        compiler_params=_cparams("arbitrary", "arbitrary"),
        name="in_proj",
    )(x2, g, scale, shift, w_main, w_kr)


def _rope_kernel(pos_ref, invr_ref, invm_ref, cr_ref, sr_ref, cm_ref, sm_ref):
    pos = pos_ref[...].astype(F32)
    ang_r = pos * invr_ref[...]
    cr_ref[...] = jnp.cos(ang_r)
    sr_ref[...] = jnp.sin(ang_r)
    ang_m = pos * invm_ref[...]
    lane = lax.broadcasted_iota(jnp.int32, ang_m.shape, 1)
    half = MLA_ROPE // 2
    cm_ref[...] = jnp.where(lane < MLA_ROPE, jnp.cos(ang_m), 0.0)
    s = jnp.sin(ang_m)
    sm_ref[...] = jnp.where(lane < half, -s, jnp.where(lane < MLA_ROPE, s, 0.0))


def _rope_tables(pos_col, tm):
    t = pos_col.shape[0]
    half_r = LANES
    inv_r = (1.0 / (np.float32(ROPE_THETA) ** (np.arange(half_r, dtype=np.float32) / np.float32(half_r)))
             ).astype(np.float32)
    half_m = MLA_ROPE // 2
    inv_m32 = (1.0 / (np.float32(ROPE_THETA) ** (np.arange(half_m, dtype=np.float32) / np.float32(half_m)))
               ).astype(np.float32)
    inv_m = np.zeros((LANES,), np.float32)
    inv_m[:half_m] = inv_m32
    inv_m[half_m:2 * half_m] = inv_m32
    tab = jax.ShapeDtypeStruct((t, LANES), F32)
    row = pl.BlockSpec((tm, LANES), lambda i: (i, 0))
    return pl.pallas_call(
        _rope_kernel,
        out_shape=(tab, tab, tab, tab),
        grid=(t // tm,),
        in_specs=[pl.BlockSpec((tm, 1), lambda i: (i, 0)),
                  pl.BlockSpec((1, LANES), lambda i: (0, 0)),
                  pl.BlockSpec((1, LANES), lambda i: (0, 0))],
        out_specs=(row, row, row, row),
        compiler_params=_cparams("arbitrary"),
        name="rope_tables",
    )(pos_col, jnp.asarray(inv_r)[None, :], jnp.asarray(inv_m)[None, :])


def _rot64(x, c, s):
    return x * c + (pltpu.roll(x, 32, 1) + pltpu.roll(x, 96, 1)) * s


def _mla_prep_kernel(ql_ref, ckv_ref, kr_ref, cm_ref, sm_ref, gq_ref, gkv_ref,
                     wqn_ref, wqr_ref, wk_ref, wv_ref, q_ref, k_ref, v_ref):
    c = cm_ref[...]
    s = sm_ref[...]
    qn = _rms(ql_ref[...].astype(F32), gq_ref[...]).astype(BF16)
    q_nope = jnp.dot(qn, wqn_ref[...], preferred_element_type=F32)
    q_rope = jnp.dot(qn, wqr_ref[...], preferred_element_type=F32)
    cn = _rms(ckv_ref[...].astype(F32), gkv_ref[...]).astype(BF16)
    k_nope = jnp.dot(cn, wk_ref[...], preferred_element_type=F32)
    v = jnp.dot(cn, wv_ref[...], preferred_element_type=F32)
    k_rot_t = _rot64(kr_ref[...], c, s).T.astype(BF16)
    lane = lax.broadcasted_iota(jnp.int32, c.shape, 1)
    ones_col = jnp.where(lane == 0, 1.0, 0.0).astype(BF16)
    scale = MLA_QK ** -0.5 * LOG2E
    for h in range(MLA_HEADS):
        lo, hi = h * LANES, (h + 1) * LANES
        q_ref[:, 2 * lo:2 * lo + LANES] = (q_nope[:, lo:hi] * scale).astype(BF16)
        q_ref[:, 2 * lo + LANES:2 * hi] = (_rot64(q_rope[:, lo:hi], c, s) * scale).astype(BF16)
        k_ref[2 * lo:2 * lo + LANES, :] = k_nope[:, lo:hi].T.astype(BF16)
        k_ref[2 * lo + LANES:2 * hi, :] = k_rot_t
        v_ref[:, 2 * lo:2 * lo + LANES] = v[:, lo:hi].astype(BF16)
        v_ref[:, 2 * lo + LANES:2 * hi] = ones_col


def _mla_prep(proj, kr_raw, cm, sm, gq, gkv, wqn, wqr, wk, wv, ql_blk, ckv_blk, seq, tm):
    t = proj.shape[0]
    rank = wqn.shape[0]
    hd = MLA_HEADS * LANES
    per_b = seq // tm
    full = lambda shape: pl.BlockSpec(shape, lambda i: (0, 0))
    row = lambda w: pl.BlockSpec((tm, w), lambda i: (i, 0))
    return pl.pallas_call(
        _mla_prep_kernel,
        out_shape=(jax.ShapeDtypeStruct((t, 2 * hd), BF16),
                   jax.ShapeDtypeStruct(((t // seq) * 2 * hd, seq), BF16),
                   jax.ShapeDtypeStruct((t, 2 * hd), BF16)),
        grid=(t // tm,),
        in_specs=[pl.BlockSpec((tm, rank), lambda i: (i, ql_blk)),
                  pl.BlockSpec((tm, rank), lambda i: (i, ckv_blk)),
                  row(LANES), row(LANES), row(LANES),
                  full((1, rank)), full((1, rank)),
                  full((rank, hd)), full((rank, hd)), full((rank, hd)), full((rank, hd))],
        out_specs=(row(2 * hd),
                   pl.BlockSpec((2 * hd, tm), lambda i: (i // per_b, i % per_b)),
                   row(2 * hd)),
        compiler_params=_cparams("arbitrary"),
        name="mla_prep",
    )(proj, proj, kr_raw, cm, sm, gq, gkv, wqn, wqr, wk, wv)


def _attn_kernel(q_ref, kt_ref, v_ref, o_ref, s_ref, *, rq, tk):
    tq = q_ref.shape[0]
    seq = kt_ref.shape[1]
    for r0 in range(0, tq, rq):
        q = q_ref[r0:r0 + rq, :]
        m_lane = None
        for c0 in range(0, seq, tk):
            s = jnp.dot(q, kt_ref[:, c0:c0 + tk], preferred_element_type=F32)
            s_ref[r0:r0 + rq, c0:c0 + tk] = s
            for l0 in range(0, tk, LANES):
                blk = s[:, l0:l0 + LANES]
                m_lane = blk if m_lane is None else jnp.maximum(m_lane, blk)
        m = jnp.max(m_lane, axis=-1, keepdims=True)
        acc = jnp.zeros((rq, 2 * MLA_V), F32)
        for c0 in range(0, seq, tk):
            p = jnp.exp2(s_ref[r0:r0 + rq, c0:c0 + tk] - m).astype(BF16)
            acc = acc + jnp.dot(p, v_ref[c0:c0 + tk, :], preferred_element_type=F32)
        o_ref[r0:r0 + rq, :] = (acc[:, :MLA_V] / acc[:, MLA_V:MLA_V + 1]).astype(o_ref.dtype)


def _attention(q, kt, v, batch, seq, tq, rq, tk):
    t = q.shape[0]
    nq = seq // tq
    return pl.pallas_call(
        functools.partial(_attn_kernel, rq=rq, tk=tk),
        out_shape=jax.ShapeDtypeStruct((t, MLA_HEADS * MLA_V), BF16),
        grid=(batch, MLA_HEADS, nq),
        in_specs=[pl.BlockSpec((tq, 2 * LANES), lambda b, h, i: (b * nq + i, h)),
                  pl.BlockSpec((2 * LANES, seq), lambda b, h, i: (b * MLA_HEADS + h, 0)),
                  pl.BlockSpec((seq, 2 * MLA_V), lambda b, h, i: (b, h))],
        out_specs=pl.BlockSpec((tq, MLA_V), lambda b, h, i: (b * nq + i, h)),
        scratch_shapes=[pltpu.VMEM((tq, seq), F32)],
        compiler_params=_cparams("arbitrary", "arbitrary", "arbitrary"),
        name="mla_attention",
    )(q, kt, v)


def _ret_kernel(rq_ref, rk_ref, rv_ref, rg_ref, cos_ref, sin_ref, df_ref, db_ref, gn_ref,
                o_ref, krot_ref, sf_ref, st_ref, dmat_ref, xif_ref, zf_ref, xib_ref, zb_ref, *, ch):
    seq, dk = rq_ref.shape
    half = dk // 2
    n_ch = seq // ch
    lgf = jnp.log(jax.nn.sigmoid(df_ref[...]))[:1, :1]
    lgb = jnp.log(jax.nn.sigmoid(db_ref[...]))[:1, :1]

    ri = lax.broadcasted_iota(jnp.int32, (ch, ch), 0)
    ci = lax.broadcasted_iota(jnp.int32, (ch, ch), 1)
    diff = (ri - ci).astype(F32)
    dmat_ref[...] = jnp.where(diff >= 0, jnp.exp(lgf * jnp.maximum(diff, 0.0)),
                              jnp.exp(lgb * jnp.maximum(-diff, 0.0)))
    pos = lax.broadcasted_iota(jnp.int32, (ch, dk), 0).astype(F32)
    xif_ref[...] = jnp.exp(lgf * (pos + 1.0))
    zf_ref[...] = jnp.exp(lgf * (ch - 1.0 - pos))
    xib_ref[...] = jnp.exp(lgb * (ch - pos))
    zb_ref[...] = jnp.exp(lgb * pos)
    gcf = jnp.exp(lgf * float(ch))
    gcb = jnp.exp(lgb * float(ch))
    k_scale = dk ** -0.5

    def rot(x, c, s):
        x1, x2 = x[:, :half], x[:, half:]
        return jnp.concatenate([x1 * c - x2 * s, x1 * s + x2 * c], axis=-1)

    def kv_outer(kz, v):
        return lax.dot_general(kz, v, (((0,), (0,)), ((), ())), preferred_element_type=F32)

    def chunk_pairs(body):
        step = 2 if n_ch % 2 == 0 else 1

        def it(j, carry):
            for u in range(step):
                body(j * step + u, carry)
            return carry

        lax.fori_loop(0, n_ch // step, it, 0)

    st_ref[...] = jnp.zeros(st_ref.shape, F32)

    def fwd(n, carry):
        r0 = pl.multiple_of(n * ch, ch)
        rows = pl.ds(r0, ch)
        k = rot(rk_ref[rows, :].astype(F32), cos_ref[rows, :], sin_ref[rows, :]) * k_scale
        krot_ref[rows, :] = k.astype(BF16)
        sf_ref[n] = st_ref[...].astype(BF16)
        kv = kv_outer((k * zf_ref[...]).astype(BF16), rv_ref[rows, :])
        st_ref[...] = st_ref[...] * gcf + kv
        return carry

    chunk_pairs(fwd)

    st_ref[...] = jnp.zeros(st_ref.shape, F32)

    def bwd(i, carry):
        n = n_ch - 1 - i
        r0 = pl.multiple_of(n * ch, ch)
        rows = pl.ds(r0, ch)
        q = rot(rq_ref[rows, :].astype(F32), cos_ref[rows, :], sin_ref[rows, :])
        kb = krot_ref[rows, :]
        v = rv_ref[rows, :]
        s = lax.dot_general(q.astype(BF16), kb, (((1,), (1,)), ((), ())), preferred_element_type=F32)
        o = jnp.dot((s * dmat_ref[...]).astype(BF16), v, preferred_element_type=F32)
        o = o + jnp.dot((q * xif_ref[...]).astype(BF16), sf_ref[n], preferred_element_type=F32)
        o = o + jnp.dot((q * xib_ref[...]).astype(BF16), st_ref[...].astype(BF16),
                        preferred_element_type=F32)
        mu = jnp.mean(o, axis=-1, keepdims=True)
        oc = o - mu
        var = jnp.mean(oc * oc, axis=-1, keepdims=True)
        on = oc * lax.rsqrt(var + GN_EPS)
        g = rg_ref[rows, :].astype(F32)
        o_ref[rows, :] = (g * jax.nn.sigmoid(g) * (on * gn_ref[...])).astype(o_ref.dtype)
        kv = kv_outer((kb.astype(F32) * zb_ref[...]).astype(BF16), v)
        st_ref[...] = st_ref[...] * gcb + kv
        return carry

    chunk_pairs(bwd)


def _retention(proj, cos_r, sin_r, dec_f, dec_b, gn, batch, seq, blk, ch):
    t = proj.shape[0]
    dk = 2 * LANES
    n_ch = seq // ch
    col = lambda base: pl.BlockSpec((seq, dk), lambda b, h: (b, base + h))
    tab = pl.BlockSpec((seq, LANES), lambda b, h: (b, 0))
    dec = pl.BlockSpec((None, 8, LANES), lambda b, h: (h, 0, 0))
    return pl.pallas_call(
        functools.partial(_ret_kernel, ch=ch),
        out_shape=jax.ShapeDtypeStruct((t, RET_HEADS * dk), BF16),
        grid=(batch, RET_HEADS),
        in_specs=[col(blk["rq"]), col(blk["rk"]), col(blk["rv"]), col(blk["rg"]),
                  tab, tab, dec, dec,
                  pl.BlockSpec((1, dk), lambda b, h: (0, h))],
        out_specs=pl.BlockSpec((seq, dk), lambda b, h: (b, h)),
        scratch_shapes=[pltpu.VMEM((seq, dk), BF16),
                        pltpu.VMEM((n_ch, dk, dk), BF16),
                        pltpu.VMEM((dk, dk), F32),
                        pltpu.VMEM((ch, ch), F32),
                        pltpu.VMEM((ch, dk), F32), pltpu.VMEM((ch, dk), F32),
                        pltpu.VMEM((ch, dk), F32), pltpu.VMEM((ch, dk), F32)],
        compiler_params=_cparams("arbitrary", "arbitrary"),
        name="retention",
    )(proj, proj, proj, proj, cos_r, sin_r, dec_f, dec_b, gn)


def _mixgate_kernel(oa_ref, ob_ref, ga_ref, gb_ref, wa_ref, wb_ref, m_ref):
    ya = jnp.dot(oa_ref[...], wa_ref[...], preferred_element_type=F32)
    yb = jnp.dot(ob_ref[...], wb_ref[...], preferred_element_type=F32)
    m = jax.nn.sigmoid(ga_ref[...].astype(F32)) * ya + jax.nn.sigmoid(gb_ref[...].astype(F32)) * yb
    m_ref[...] = m.astype(m_ref.dtype)


def _mix_gate(o_mla, o_ret, proj, wa, wb, ga_blk, gb_blk, tm):
    t = o_mla.shape[0]
    d = wa.shape[1]
    return pl.pallas_call(
        _mixgate_kernel,
        out_shape=jax.ShapeDtypeStruct((t, d), BF16),
        grid=(t // tm,),
        in_specs=[pl.BlockSpec((tm, o_mla.shape[1]), lambda i: (i, 0)),
                  pl.BlockSpec((tm, o_ret.shape[1]), lambda i: (i, 0)),
                  pl.BlockSpec((tm, d), lambda i: (i, ga_blk)),
                  pl.BlockSpec((tm, d), lambda i: (i, gb_blk)),
                  pl.BlockSpec(wa.shape, lambda i: (0, 0)),
                  pl.BlockSpec(wb.shape, lambda i: (0, 0))],
        out_specs=pl.BlockSpec((tm, d), lambda i: (i, 0)),
        compiler_params=_cparams("arbitrary"),
        name="mix_gate",
    )(o_mla, o_ret, proj, proj, wa, wb)


def _mixout_kernel(m_ref, x_ref, wo_ref, gpost_ref, gate_ref, gpre_ref, sc_ref, sh_ref,
                   wr_ref, br_ref, x1_ref, h2_ref, lg_ref):
    mix = jnp.dot(m_ref[...], wo_ref[...], preferred_element_type=F32)
    x1 = x_ref[...] + gate_ref[...] * _rms(mix, gpost_ref[...])
    x1_ref[...] = x1
    h2 = _rms(x1, gpre_ref[...]) * (1.0 + sc_ref[...]) + sh_ref[...]
    half = h2.shape[1] // 2
    bits = pltpu.bitcast(h2.astype(BF16).astype(F32), jnp.uint32)
    h2_ref[...] = (bits[:, :half] >> 16) | (bits[:, half:] & jnp.uint32(0xFFFF0000))
    h_hi = h2.astype(BF16)
    h_lo = (h2 - h_hi.astype(F32)).astype(BF16)
    wr = wr_ref[...]
    hh = jnp.dot(h_hi, wr, preferred_element_type=F32)
    lh = jnp.dot(h_lo, wr[:, :LANES], preferred_element_type=F32)
    lg_ref[...] = hh[:, :LANES] + (hh[:, LANES:] + lh) + br_ref[...]


def _mix_out(m, x2, wo, gpost, gate_m, gpre, scale_f, shift_f, wr, br, seq, tm):
    t, d = x2.shape
    per_b = seq // tm
    row = pl.BlockSpec((tm, d), lambda i: (i, 0))
    vec = pl.BlockSpec((1, d), lambda i: (0, 0))
    mod = pl.BlockSpec((None, 1, d), lambda i: (i // per_b, 0, 0))
    return pl.pallas_call(
        _mixout_kernel,
        out_shape=(jax.ShapeDtypeStruct((t, d), F32), jax.ShapeDtypeStruct((t, d // 2), jnp.uint32),
                   jax.ShapeDtypeStruct((t, LANES), F32)),
        grid=(t // tm,),
        in_specs=[row, row, pl.BlockSpec((d, d), lambda i: (0, 0)), vec, mod, vec, mod, mod,
                  pl.BlockSpec((d, 2 * LANES), lambda i: (0, 0)),
                  pl.BlockSpec((1, LANES), lambda i: (0, 0))],
        out_specs=(row, pl.BlockSpec((tm, d // 2), lambda i: (i, 0)),
                   pl.BlockSpec((tm, LANES), lambda i: (i, 0))),
        compiler_params=_cparams("arbitrary"),
        name="mix_out",
    )(m, x2, wo, gpost, gate_m, gpre, scale_f, shift_f, wr, br)


def _route_kernel(lg_ref, o_ref, cnt_ref, carry_ref):
    @pl.when(pl.program_id(0) == 0)
    def _():
        carry_ref[...] = jnp.zeros(carry_ref.shape, F32)

    tm = lg_ref.shape[0]
    lane = lax.broadcasted_iota(jnp.int32, (tm, LANES), 1).astype(F32)
    l = jnp.where(lane < N_EXPERTS, lg_ref[...], -jnp.inf)
    vals, idxs = [], []
    for _ in range(TOP_K):
        mx = jnp.max(l, axis=-1, keepdims=True)
        ik = jnp.min(jnp.where(l == mx, lane, float(LANES)), axis=-1, keepdims=True)
        vals.append(mx)
        idxs.append(ik)
        l = jnp.where(lane == ik, -jnp.inf, l)
    es = [jnp.exp(v - vals[0]) for v in vals]
    den = es[0] + es[1] + es[2] + es[3]
    onehot = jnp.zeros((tm, LANES), F32)
    for ik in idxs:
        onehot = onehot + jnp.where(lane == ik, 1.0, 0.0)
    ri = lax.broadcasted_iota(jnp.int32, (tm, tm), 0)
    ci = lax.broadcasted_iota(jnp.int32, (tm, tm), 1)
    tri = jnp.where(ci < ri, 1.0, 0.0).astype(BF16)
    cum = jnp.dot(tri, onehot.astype(BF16), preferred_element_type=F32) + carry_ref[0:1, :]
    packed = jnp.zeros((tm, LANES), F32)
    for k in range(TOP_K):
        rank = jnp.sum(jnp.where(lane == idxs[k], cum, 0.0), axis=-1, keepdims=True)
        packed = packed + jnp.where(lane == float(k), idxs[k], 0.0)
        packed = packed + jnp.where(lane == float(TOP_K + k), rank, 0.0)
        packed = packed + jnp.where(lane == float(2 * TOP_K + k), es[k] / den, 0.0)
    o_ref[...] = packed
    carry_ref[...] = carry_ref[...] + jnp.sum(onehot, axis=0, keepdims=True)
    cnt_ref[...] = carry_ref[...]


def _route(logits, tm):
    t = logits.shape[0]
    return pl.pallas_call(
        _route_kernel,
        out_shape=(jax.ShapeDtypeStruct((t, LANES), F32), jax.ShapeDtypeStruct((8, LANES), F32)),
        grid=(t // tm,),
        in_specs=[pl.BlockSpec((tm, LANES), lambda i: (i, 0))],
        out_specs=(pl.BlockSpec((tm, LANES), lambda i: (i, 0)),
                   pl.BlockSpec((8, LANES), lambda i: (0, 0))),
        scratch_shapes=[pltpu.VMEM((8, LANES), F32)],
        compiler_params=_cparams("arbitrary"),
        name="route",
    )(logits)


def _dispatch_kernel(dest_ref, pad_pos_ref, pad_len_ref, tail_ref, h_ref, xs_ref, z_ref, sem, zsem,
                     *, tm_e):
    tm = h_ref.shape[0]
    zr = z_ref.shape[0]
    base = pl.program_id(0) * (tm * TOP_K)
    pieces = [p for p in (2 ** i for i in range(20)) if SUBLANES <= p <= zr and p < tm_e][::-1]

    def zero_copy(pos, p):
        return pltpu.make_async_copy(z_ref.at[pl.ds(0, p), :], xs_ref.at[pl.ds(pos, p), :], zsem)

    def pad_rows(act):
        for e in range(N_EXPERTS):
            pos = pad_pos_ref[e]
            head = (-pos) & (SUBLANES - 1)
            for j in range(SUBLANES - 1):
                @pl.when(j < head)
                def _(pos=pos, j=j):
                    act(zero_copy(pos + j, 1))

            pos = pos + head
            ln = pad_len_ref[e] - head
            for p in pieces:
                hit = (ln & p) != 0

                @pl.when(hit)
                def _(pos=pos, p=p):
                    act(zero_copy(pl.multiple_of(pos, SUBLANES), p))

                pos = pos + jnp.where(hit, p, 0)

    def tail_rows(act):
        def body(j, carry):
            for r0 in range(0, tm_e, zr):
                act(zero_copy(pl.multiple_of(j * tm_e + r0, SUBLANES), zr))
            return carry

        lax.fori_loop(tail_ref[0], tail_ref[1], body, 0)

    @pl.when(pl.program_id(0) == 0)
    def _():
        z_ref[...] = jnp.zeros(z_ref.shape, z_ref.dtype)
        pad_rows(lambda c: c.start())
        tail_rows(lambda c: c.start())

    def row_copy(t, d):
        return pltpu.make_async_copy(h_ref.at[pl.ds(t, 1), :], xs_ref.at[pl.ds(d, 1), :], sem)

    def issue(t, carry):
        for k in range(TOP_K):
            row_copy(t, dest_ref[base + t * TOP_K + k]).start(priority=k % 2)
        return carry

    lax.fori_loop(0, tm, issue, 0, unroll=8)

    for k in range(TOP_K):
        pltpu.make_async_copy(h_ref, xs_ref.at[pl.ds(0, tm), :], sem).wait()

    @pl.when(pl.program_id(0) == 0)
    def _():
        pad_rows(lambda c: c.wait())
        tail_rows(lambda c: c.wait())


def _dispatch(dest, pad_pos, pad_len, tail, h2, n_rows, tm, tm_e):
    t, d = h2.shape
    zr = min(256, tm_e)
    nsp = 4
    return pl.pallas_call(
        functools.partial(_dispatch_kernel, tm_e=tm_e),
        out_shape=jax.ShapeDtypeStruct((n_rows, d), h2.dtype),
        grid_spec=pltpu.PrefetchScalarGridSpec(
            num_scalar_prefetch=nsp,
            grid=(t // tm,),
            in_specs=[pl.BlockSpec((tm, d), lambda i, *_: (i, 0))],
            out_specs=pl.BlockSpec(memory_space=pl.ANY),
            scratch_shapes=[pltpu.VMEM((zr, d), h2.dtype), pltpu.SemaphoreType.DMA,
                            pltpu.SemaphoreType.DMA],
        ),
        compiler_params=_cparams("arbitrary"),
        name="dispatch",
    )(dest, pad_pos, pad_len, tail, h2)


def _expert_kernel(ge_ref, gs_ref, gn_ref, tail_ref, xs_ref, wg_ref, wu_ref, bg_ref, bu_ref,
                   wd_ref, bd_ref, ys_ref, xu_ref, xb_ref, ab_ref,
                   ob_ref, op_ref, or_ref, z_ref, flag_ref, xsem, osem, psem, rsem, zsem, *, rs, nf, nd):
    del ge_ref
    g = pl.program_id(0)
    s = pl.program_id(1)
    n = gn_ref[g]
    row0 = gs_ref[g]
    half = xu_ref.shape[1]
    tf = wg_ref.shape[1]
    td = wd_ref.shape[1]

    def rows(i):
        return pl.ds(pl.multiple_of(i * rs, rs), rs)

    def x_copy(base, i):
        src = xs_ref.at[pl.ds(pl.multiple_of(base + i * rs, rs), rs), :]
        return pltpu.make_async_copy(src, xu_ref.at[rows(i), :], xsem)

    def x_start(base, cnt):
        def start(i, c):
            x_copy(base, i).start()
            return c

        lax.fori_loop(0, cnt, start, 0)

    g_next = jnp.minimum(g + 1, pl.num_programs(0) - 1)
    n_next = jnp.where(g + 1 < pl.num_programs(0), gn_ref[g_next], 0)

    @pl.when(jnp.logical_and(n > 0, s == 1))
    def _():
        x_start(gs_ref[g_next], n_next)

    @pl.when(jnp.logical_and(n > 0, s == 0))
    def _():
        @pl.when(g == 0)
        def _():
            x_start(row0, n)

        def wait(i, c):
            x_copy(row0, i).wait()
            return c

        lax.fori_loop(0, n, wait, 0)

        def unpack(i, c):
            u = xu_ref[rows(i), :]
            lo = pltpu.bitcast(u << 16, F32)
            hi = pltpu.bitcast(u & jnp.uint32(0xFFFF0000), F32)
            xb_ref[rows(i), :half] = lo.astype(BF16)
            xb_ref[rows(i), half:] = hi.astype(BF16)
            return c

        lax.fori_loop(0, n, unpack, 0)

    def out_wait(k):
        buf, sem, m = ((ob_ref.at[0], osem.at[0], 4 * rs), (ob_ref.at[1], osem.at[1], 4 * rs),
                       (op_ref, psem, 2 * rs), (or_ref, rsem, rs))[k]
        pltpu.make_async_copy(buf, ys_ref.at[pl.ds(0, m), 0:td], sem).wait()

    def drain(k):
        @pl.when(flag_ref[k] == 1)
        def _():
            out_wait(k)
            flag_ref[k] = 0

    @pl.when(jnp.logical_and(g == 0, s == 0))
    def _():
        for k in range(4):
            flag_ref[k] = 0

    noct = lax.shift_right_logical(n, 3)
    has_quad = (n & 4) != 0
    has_pair = (n & 2) != 0
    odd = (n & 1) == 1
    quad_r0 = noct * (8 * rs)
    pair_r0 = quad_r0 + jnp.where(has_quad, 4 * rs, 0)

    def row_slice(r0, m):
        return pl.ds(pl.multiple_of(r0, rs), m)

    @pl.when(jnp.logical_and(n > 0, s < nf))
    def _():
        def act_rows(r0, m, wgb, wub):
            x = xb_ref[row_slice(r0, m), :]
            gt = jnp.dot(x, wgb, preferred_element_type=F32) + bg_ref[...]
            ut = jnp.dot(x, wub, preferred_element_type=F32) + bu_ref[...]
            gt = jnp.minimum(gt, SWIGLU_LIMIT)
            ut = jnp.clip(ut, -SWIGLU_LIMIT, SWIGLU_LIMIT)
            act = (ut + 1.0) * (gt * jax.nn.sigmoid(SWIGLU_ALPHA * gt))
            ab_ref[s, row_slice(r0, m), :] = act.astype(BF16)

        def chunks(r0, sizes):
            wgb = wg_ref[...].astype(BF16)
            wub = wu_ref[...].astype(BF16)
            for m in sizes:
                act_rows(r0, m, wgb, wub)
                r0 = r0 + m

        def oct_body(i, c):
            chunks(i * (8 * rs), (4 * rs, 4 * rs))
            return c

        lax.fori_loop(0, noct, oct_body, 0)

        @pl.when(has_quad)
        def _():
            chunks(quad_r0, (4 * rs,))

        @pl.when(has_pair)
        def _():
            chunks(pair_r0, (2 * rs,))

        @pl.when(odd)
        def _():
            chunks((n - 1) * rs, (rs,))

    @pl.when(jnp.logical_and(n > 0, s >= nf))
    def _():
        d = s - nf

        def out_copy(r0, m, buf, sem, j):
            dst = ys_ref.at[pl.ds(pl.multiple_of(row0 + r0, rs), m), j * td:(j + 1) * td]
            return pltpu.make_async_copy(buf, dst, sem)

        def emit(r0, m, k, wdb):
            buf, sem = ((ob_ref.at[0], osem.at[0]), (ob_ref.at[1], osem.at[1]),
                        (op_ref, psem), (or_ref, rsem))[k]
            drain(k)
            hm = m // 2 if m >= 2 * rs else m
            for h0 in range(0, m, hm):
                a_full = jnp.concatenate([ab_ref[f, row_slice(r0 + h0, hm), :] for f in range(nf)],
                                         axis=1)
                buf[h0:h0 + hm, :] = jnp.dot(a_full, wdb, preferred_element_type=F32) + bd_ref[...]
            for j in range(nd):
                @pl.when(d == j)
                def _(j=j):
                    out_copy(r0, m, buf, sem, j).start()

            flag_ref[k] = 1

        def w_down():
            return wd_ref[...].astype(BF16)

        @pl.when(odd)
        def _():
            emit((n - 1) * rs, rs, FLAG_SINGLE, w_down())

        @pl.when(has_pair)
        def _():
            emit(pair_r0, 2 * rs, FLAG_PAIR, w_down())

        @pl.when(has_quad)
        def _():
            emit(quad_r0, 4 * rs, 0, w_down())

        def oct_body(i, c):
            wdb = w_down()
            emit(i * (8 * rs), 4 * rs, 0, wdb)
            emit(i * (8 * rs) + 4 * rs, 4 * rs, 1, wdb)
            return c

        lax.fori_loop(0, noct, oct_body, 0)

    @pl.when(jnp.logical_and(g == pl.num_programs(0) - 1, s == nf + nd - 1))
    def _():
        for k in range(4):
            drain(k)
        z_ref[...] = jnp.zeros(z_ref.shape, z_ref.dtype)

        def z_copy(i):
            dst = ys_ref.at[pl.ds(pl.multiple_of(i * rs, rs), rs), :]
            return pltpu.make_async_copy(z_ref, dst, zsem)

        def start(i, c):
            z_copy(i).start()
            return c

        def wait(i, c):
            z_copy(i).wait()
            return c

        lax.fori_loop(tail_ref[0], tail_ref[1], start, 0)
        lax.fori_loop(tail_ref[0], tail_ref[1], wait, 0)


def _experts(ge, gs, gn, tail, xs, w_gu, b_gu, w_dn, b_dn, rs, rmax, tf, td):
    p = xs.shape[0]
    n_e, d, f2 = w_gu.shape
    fdim = f2 // 2
    nf = fdim // tf
    nd = d // td
    n_groups = ge.shape[0]

    def f_idx(g, s, gn):
        return jnp.where(gn[g] > 0, jnp.minimum(s, nf - 1), nf - 1)

    def d_idx(g, s, gn):
        return jnp.where(gn[g] > 0, jnp.maximum(s - nf, 0), nd - 1)

    return pl.pallas_call(
        functools.partial(_expert_kernel, rs=rs, nf=nf, nd=nd),
        out_shape=jax.ShapeDtypeStruct((p, d), F32),
        grid_spec=pltpu.PrefetchScalarGridSpec(
            num_scalar_prefetch=4,
            grid=(n_groups, nf + nd),
            in_specs=[pl.BlockSpec(memory_space=pl.ANY),
                      pl.BlockSpec((None, d, tf), lambda g, s, ge, gs, gn, tl: (ge[g], 0, f_idx(g, s, gn))),
                      pl.BlockSpec((None, d, tf),
                                   lambda g, s, ge, gs, gn, tl: (ge[g], 0, nf + f_idx(g, s, gn))),
                      pl.BlockSpec((None, 1, tf), lambda g, s, ge, gs, gn, tl: (ge[g], 0, f_idx(g, s, gn))),
                      pl.BlockSpec((None, 1, tf),
                                   lambda g, s, ge, gs, gn, tl: (ge[g], 0, nf + f_idx(g, s, gn))),
                      pl.BlockSpec((None, fdim, td), lambda g, s, ge, gs, gn, tl: (ge[g], 0, d_idx(g, s, gn))),
                      pl.BlockSpec((None, 1, td), lambda g, s, ge, gs, gn, tl: (ge[g], 0, d_idx(g, s, gn)))],
            out_specs=pl.BlockSpec(memory_space=pl.ANY),
            scratch_shapes=[pltpu.VMEM((rmax, d // 2), jnp.uint32),
                            pltpu.VMEM((rmax, d), BF16),
                            pltpu.VMEM((nf, rmax, tf), BF16),
                            pltpu.VMEM((2, 4 * rs, td), F32),
                            pltpu.VMEM((2 * rs, td), F32),
                            pltpu.VMEM((rs, td), F32),
                            pltpu.VMEM((rs, d), F32),
                            pltpu.SMEM((4,), jnp.int32),
                            pltpu.SemaphoreType.DMA, pltpu.SemaphoreType.DMA((2,)),
                            pltpu.SemaphoreType.DMA, pltpu.SemaphoreType.DMA,
                            pltpu.SemaphoreType.DMA],
        ),
        compiler_params=_cparams("arbitrary", "arbitrary"),
        name="experts",
    )(ge, gs, gn, tail, xs, w_gu, w_gu, b_gu, b_gu, w_dn, b_dn)


def _combine_kernel(dest_ref, ys_ref, pk_ref, x1_ref, gpost_ref, gate_ref, o_ref, buf_ref, sem):
    tm = x1_ref.shape[0]
    base = pl.program_id(0) * (tm * TOP_K)

    def row_copy(t, k, d):
        return pltpu.make_async_copy(ys_ref.at[pl.ds(d, 1), :], buf_ref.at[k, pl.ds(t, 1), :], sem)

    def issue(t, carry):
        for k in range(TOP_K):
            row_copy(t, k, dest_ref[base + t * TOP_K + k]).start(priority=k % 2)
        return carry

    lax.fori_loop(0, tm, issue, 0, unroll=8)

    for k in range(TOP_K):
        pltpu.make_async_copy(ys_ref.at[pl.ds(0, tm), :], buf_ref.at[k], sem).wait()

    pk = pk_ref[...]
    f = jnp.zeros(x1_ref.shape, F32)
    for k in range(TOP_K):
        w = pk[:, 2 * TOP_K + k:2 * TOP_K + k + 1]
        f = f + buf_ref[k] * w
    o_ref[...] = x1_ref[...] + gate_ref[...] * _rms(f, gpost_ref[...])


def _combine(dest, ys, packed, x1, gpost, gate_f, seq, tm):
    t, d = x1.shape
    per_b = seq // tm
    return pl.pallas_call(
        _combine_kernel,
        out_shape=jax.ShapeDtypeStruct((t, d), F32),
        grid_spec=pltpu.PrefetchScalarGridSpec(
            num_scalar_prefetch=1,
            grid=(t // tm,),
            in_specs=[pl.BlockSpec(memory_space=pl.ANY),
                      pl.BlockSpec((tm, LANES), lambda i, dest: (i, 0)),
                      pl.BlockSpec((tm, d), lambda i, dest: (i, 0)),
                      pl.BlockSpec((1, d), lambda i, dest: (0, 0)),
                      pl.BlockSpec((None, 1, d), lambda i, dest: (i // per_b, 0, 0))],
            out_specs=pl.BlockSpec((tm, d), lambda i, dest: (i, 0)),
            scratch_shapes=[pltpu.VMEM((TOP_K, tm, d), F32), pltpu.SemaphoreType.DMA],
        ),
        compiler_params=_cparams("arbitrary"),
        name="combine",
    )(dest, ys, packed, x1, gpost, gate_f)


def _pad_heads(w, n_heads, width):
    k = w.shape[0]
    w = w.reshape(k, n_heads, width)
    return jnp.pad(w, ((0, 0), (0, 0), (0, LANES - width))).reshape(k, n_heads * LANES)


def _layer(x, c, positions, w_ada, b_ada, g_pre_mix, g_post_mix, g_pre_ffn, g_post_ffn,
           w_in, q_a_norm, w_q_b, kv_a_norm, w_kv_b, w_mla_o, ret_decay_fwd, ret_decay_bwd,
           ret_gn, w_ret_o, w_out, w_router, b_router, w_gate_up, b_gate_up, w_down, b_down):
    batch, seq, d = x.shape
    t = batch * seq
    q_rank = q_a_norm.shape[0]
    kv_rank = kv_a_norm.shape[0]
    hd = RET_HEADS * 2 * LANES
    x2 = x.reshape(t, d)

    c_pad = jnp.pad(c, ((0, 8 - batch), (0, 0)))
    ada = _ada(c_pad, w_ada, b_ada[None, :])[:batch]
    shift_m, scale_m, gate_m, shift_f, scale_f, gate_f = [
        ada[:, i * d:(i + 1) * d].reshape(batch, 1, d) for i in range(N_MOD)]

    o_q, o_kv = 0, q_rank
    o_r = q_rank + kv_rank + MLA_ROPE
    w_wide = w_in[:, o_r:]
    w_main = jnp.concatenate([w_wide, w_in[:, o_q:o_q + q_rank], w_in[:, o_kv:o_kv + kv_rank]],
                             axis=1).astype(BF16)
    w_kr = jnp.pad(w_in[:, o_kv + kv_rank:o_r], ((0, 0), (0, LANES - MLA_ROPE))).astype(BF16)
    n_wide = w_wide.shape[1]
    proj, kr_raw = _in_proj(x2, g_pre_mix[None, :], scale_m, shift_m, w_main, w_kr, seq,
                            tm=min(1024, seq), tn=1024)
    blk_w = 2 * LANES
    blk = {"rq": 0, "rk": hd // blk_w, "rv": 2 * hd // blk_w, "rg": 3 * hd // blk_w}
    ga_blk, gb_blk = (4 * hd) // d, (4 * hd + d) // d
    ql_blk, ckv_blk = n_wide // q_rank, (n_wide + q_rank) // kv_rank

    cos_r, sin_r, cm, sm = _rope_tables(positions.reshape(t, 1), tm=min(512, seq))

    wq = w_q_b.reshape(q_rank, MLA_HEADS, MLA_QK)
    wqn = wq[:, :, :MLA_NOPE].reshape(q_rank, MLA_HEADS * MLA_NOPE).astype(BF16)
    wqr = _pad_heads(wq[:, :, MLA_NOPE:].reshape(q_rank, MLA_HEADS * MLA_ROPE), MLA_HEADS,
                     MLA_ROPE).astype(BF16)
    wkv = w_kv_b.reshape(kv_rank, MLA_HEADS, MLA_NOPE + MLA_V)
    wk = wkv[:, :, :MLA_NOPE].reshape(kv_rank, MLA_HEADS * MLA_NOPE).astype(BF16)
    wv = wkv[:, :, MLA_NOPE:].reshape(kv_rank, MLA_HEADS * MLA_V).astype(BF16)
    q, k, v = _mla_prep(proj, kr_raw, cm, sm, q_a_norm[None, :], kv_a_norm[None, :],
                        wqn, wqr, wk, wv, ql_blk, ckv_blk, seq, tm=min(512, seq))
    o_mla = _attention(q, k, v, batch, seq, tq=min(2048, seq), rq=128, tk=min(512, seq))

    dec_f = jnp.broadcast_to(ret_decay_fwd.astype(F32)[:, None, None], (RET_HEADS, 8, LANES))
    dec_b = jnp.broadcast_to(ret_decay_bwd.astype(F32)[:, None, None], (RET_HEADS, 8, LANES))
    o_ret = _retention(proj, cos_r, sin_r, dec_f, dec_b, ret_gn[None, :], batch, seq, blk,
                       ch=min(256, seq))

    m = _mix_gate(o_mla, o_ret, proj, w_mla_o.astype(BF16), w_ret_o.astype(BF16), ga_blk, gb_blk,
                  tm=min(512, seq))
    wr32 = jnp.pad(w_router, ((0, 0), (0, LANES - N_EXPERTS)))
    wr_hi = wr32.astype(BF16)
    wr = jnp.concatenate([wr_hi, (wr32 - wr_hi.astype(F32)).astype(BF16)], axis=1)
    br = jnp.pad(b_router, (0, LANES - N_EXPERTS))[None, :]
    x1, h2, logits = _mix_out(m, x2, w_out.astype(BF16), g_post_mix[None, :], gate_m,
                              g_pre_ffn[None, :], scale_f, shift_f, wr, br, seq, tm=min(256, seq))

    packed, cnt = _route(logits, tm=min(256, seq))
    e_idx = packed[:, 0:TOP_K].astype(jnp.int32)
    rank = packed[:, TOP_K:2 * TOP_K].astype(jnp.int32)
    counts = cnt[0, :N_EXPERTS].astype(jnp.int32)
    a = t * TOP_K
    rs = min(256, a // N_EXPERTS)
    rmax = GROUP_SUB * rs
    n_sub = (a + N_EXPERTS * (rs - 1) + rs - 1) // rs
    n_groups = N_EXPERTS + a // rmax
    padded = ((counts + rs - 1) // rs) * rs
    pend = jnp.cumsum(padded)
    pstart = pend - padded
    dest = (pstart[e_idx] + rank).reshape(a).astype(jnp.int32)
    used_sub = (pend[-1] // rs).astype(jnp.int32).reshape(1)
    tail = jnp.concatenate([used_sub, jnp.full((1,), n_sub, jnp.int32)])
    e_sub = padded // rs
    e_grp = (e_sub + GROUP_SUB - 1) // GROUP_SUB
    g_end = jnp.cumsum(e_grp)
    slot = jnp.arange(n_groups, dtype=jnp.int32)
    g_e = jnp.minimum(jnp.sum((g_end[None, :] <= slot[:, None]).astype(jnp.int32), axis=1),
                      N_EXPERTS - 1)
    g_k = slot - (g_end - e_grp)[g_e]
    live = slot < g_end[-1]
    last_e = g_e[jnp.maximum(g_end[-1] - 1, 0)]
    g_e = jnp.where(live, g_e, last_e).astype(jnp.int32)
    e_chunk = (e_sub + jnp.maximum(e_grp, 1) - 1) // jnp.maximum(e_grp, 1)
    g_s = jnp.where(live, pstart[g_e] + g_k * e_chunk[g_e] * rs, 0).astype(jnp.int32)
    g_n = jnp.where(live, jnp.clip(e_sub[g_e] - e_chunk[g_e] * g_k, 0, e_chunk[g_e]), 0).astype(jnp.int32)

    xs = _dispatch(dest, (pstart + counts).astype(jnp.int32), (padded - counts).astype(jnp.int32),
                   tail, h2, n_sub * rs, tm=min(256, seq), tm_e=rs)
    fdim = w_down.shape[1]
    ys = _experts(g_e, g_s, g_n, tail, xs, w_gate_up, b_gate_up[:, None, :], w_down,
                  b_down[:, None, :], rs=rs, rmax=rmax, tf=min(256, fdim), td=min(256, d))
    out = _combine(dest, ys, packed, x1, g_post_ffn[None, :], gate_f, seq, tm=min(256, seq))
    return out.reshape(batch, seq, d)


def kernel(x, c, positions, w_ada, b_ada, g_pre_mix, g_post_mix, g_pre_ffn, g_post_ffn, w_in,
           q_a_norm, w_q_b, kv_a_norm, w_kv_b, w_mla_o, ret_decay_fwd, ret_decay_bwd, ret_gn,
           w_ret_o, w_out, w_router, b_router, w_gate_up, b_gate_up, w_down, b_down):
    params = (w_ada, b_ada, g_pre_mix, g_post_mix, g_pre_ffn, g_post_ffn, w_in, q_a_norm, w_q_b,
              kv_a_norm, w_kv_b, w_mla_o, ret_decay_fwd, ret_decay_bwd, ret_gn, w_ret_o, w_out,
              w_router, b_router, w_gate_up, b_gate_up, w_down, b_down)
    for l in range(w_ada.shape[0]):
        x = _layer(x, c, positions, *[p[l] for p in params])
    return x
```

```python
import functools

import numpy as np
import jax
import jax.numpy as jnp
from jax import lax
from jax.experimental import pallas as pl
from jax.experimental.pallas import tpu as pltpu

F32 = jnp.float32
BF16 = jnp.bfloat16

MLA_HEADS = 8
MLA_NOPE = 128
MLA_ROPE = 64
MLA_V = 128
MLA_QK = MLA_NOPE + MLA_ROPE
RET_HEADS = 8
N_EXPERTS = 32
TOP_K = 4
SWIGLU_LIMIT = 7.0
SWIGLU_ALPHA = 1.702
ROPE_THETA = 10000.0
RMS_EPS = 1e-6
GN_EPS = 1e-6
LOG2E = 1.4426950408889634
N_MOD = 6

LANES = 128
SUBLANES = 8
VMEM_LIMIT = 56 * 1024 * 1024
EXPERT_SUB_ROWS = 128
EXPERT_PIECE_SUB = 8
GROUP_SUB = 18


def _cparams(*sem):
    return pltpu.CompilerParams(dimension_semantics=sem, vmem_limit_bytes=VMEM_LIMIT)


def _rms(x, gain):
    return x * lax.rsqrt(jnp.mean(x * x, axis=-1, keepdims=True) + RMS_EPS) * gain


def _ada_kernel(c_ref, w_ref, b_ref, o_ref):
    o_ref[...] = jnp.dot(c_ref[...], w_ref[...], preferred_element_type=F32,
                         precision=lax.Precision.HIGHEST) + b_ref[...]


def _ada(c_pad, w_ada, b_ada, tn=1024):
    m, d = c_pad.shape
    n = w_ada.shape[1]
    return pl.pallas_call(
        _ada_kernel,
        out_shape=jax.ShapeDtypeStruct((m, n), F32),
        grid=(n // tn,),
        in_specs=[pl.BlockSpec((m, d), lambda j: (0, 0)),
                  pl.BlockSpec((d, tn), lambda j: (0, j)),
                  pl.BlockSpec((1, tn), lambda j: (0, j))],
        out_specs=pl.BlockSpec((m, tn), lambda j: (0, j)),
        compiler_params=_cparams("arbitrary"),
        name="ada",
    )(c_pad, w_ada, b_ada)


def _inproj_kernel(x_ref, g_ref, sc_ref, sh_ref, w_ref, wkr_ref, o_ref, okr_ref, h_ref):
    @pl.when(pl.program_id(1) == 0)
    def _():
        h = _rms(x_ref[...], g_ref[...]) * (1.0 + sc_ref[...]) + sh_ref[...]
        hb = h.astype(BF16)
        h_ref[...] = hb
        okr_ref[...] = jnp.dot(hb, wkr_ref[...], preferred_element_type=F32)

    o_ref[...] = jnp.dot(h_ref[...], w_ref[...], preferred_element_type=F32).astype(o_ref.dtype)


def _in_proj(x2, g, scale, shift, w_main, w_kr, seq, tm, tn):
    t, d = x2.shape
    n = w_main.shape[1]
    per_b = seq // tm
    return pl.pallas_call(
        _inproj_kernel,
        out_shape=(jax.ShapeDtypeStruct((t, n), BF16), jax.ShapeDtypeStruct((t, LANES), F32)),
        grid=(t // tm, n // tn),
        in_specs=[pl.BlockSpec((tm, d), lambda i, j: (i, 0)),
                  pl.BlockSpec((1, d), lambda i, j: (0, 0)),
                  pl.BlockSpec((None, 1, d), lambda i, j: (i // per_b, 0, 0)),
                  pl.BlockSpec((None, 1, d), lambda i, j: (i // per_b, 0, 0)),
                  pl.BlockSpec((d, tn), lambda i, j: (0, j)),
                  pl.BlockSpec((d, LANES), lambda i, j: (0, 0))],
        out_specs=(pl.BlockSpec((tm, tn), lambda i, j: (i, j)),
                   pl.BlockSpec((tm, LANES), lambda i, j: (i, 0))),
        scratch_shapes=[pltpu.VMEM((tm, d), BF16)],
        compiler_params=_cparams("arbitrary", "arbitrary"),
        name="in_proj",
    )(x2, g, scale, shift, w_main, w_kr)


def _rope_kernel(pos_ref, invr_ref, invm_ref, cr_ref, sr_ref, cm_ref, sm_ref):
    pos = pos_ref[...].astype(F32)
    ang_r = pos * invr_ref[...]
    cr_ref[...] = jnp.cos(ang_r)
    sr_ref[...] = jnp.sin(ang_r)
    ang_m = pos * invm_ref[...]
    lane = lax.broadcasted_iota(jnp.int32, ang_m.shape, 1)
    half = MLA_ROPE // 2
    cm_ref[...] = jnp.where(lane < MLA_ROPE, jnp.cos(ang_m), 0.0)
    s = jnp.sin(ang_m)
    sm_ref[...] = jnp.where(lane < half, -s, jnp.where(lane < MLA_ROPE, s, 0.0))


def _rope_tables(pos_col, tm):
    t = pos_col.shape[0]
    half_r = LANES
    inv_r = (1.0 / (np.float32(ROPE_THETA) ** (np.arange(half_r, dtype=np.float32) / np.float32(half_r)))
             ).astype(np.float32)
    half_m = MLA_ROPE // 2
    inv_m32 = (1.0 / (np.float32(ROPE_THETA) ** (np.arange(half_m, dtype=np.float32) / np.float32(half_m)))
               ).astype(np.float32)
    inv_m = np.zeros((LANES,), np.float32)
    inv_m[:half_m] = inv_m32
    inv_m[half_m:2 * half_m] = inv_m32
    tab = jax.ShapeDtypeStruct((t, LANES), F32)
    row = pl.BlockSpec((tm, LANES), lambda i: (i, 0))
    return pl.pallas_call(
        _rope_kernel,
        out_shape=(tab, tab, tab, tab),
        grid=(t // tm,),
        in_specs=[pl.BlockSpec((tm, 1), lambda i: (i, 0)),
                  pl.BlockSpec((1, LANES), lambda i: (0, 0)),
                  pl.BlockSpec((1, LANES), lambda i: (0, 0))],
        out_specs=(row, row, row, row),
        compiler_params=_cparams("arbitrary"),
        name="rope_tables",
    )(pos_col, jnp.asarray(inv_r)[None, :], jnp.asarray(inv_m)[None, :])


def _rot64(x, c, s):
    return x * c + (pltpu.roll(x, 32, 1) + pltpu.roll(x, 96, 1)) * s


def _mla_prep_kernel(ql_ref, ckv_ref, kr_ref, cm_ref, sm_ref, gq_ref, gkv_ref,
                     wqn_ref, wqr_ref, wk_ref, wv_ref, q_ref, k_ref, v_ref):
    c = cm_ref[...]
    s = sm_ref[...]
    qn = _rms(ql_ref[...].astype(F32), gq_ref[...]).astype(BF16)
    q_nope = jnp.dot(qn, wqn_ref[...], preferred_element_type=F32)
    q_rope = jnp.dot(qn, wqr_ref[...], preferred_element_type=F32)
    cn = _rms(ckv_ref[...].astype(F32), gkv_ref[...]).astype(BF16)
    k_nope = jnp.dot(cn, wk_ref[...], preferred_element_type=F32)
    v = jnp.dot(cn, wv_ref[...], preferred_element_type=F32)
    k_rot_t = _rot64(kr_ref[...], c, s).T.astype(BF16)
    lane = lax.broadcasted_iota(jnp.int32, c.shape, 1)
    ones_col = jnp.where(lane == 0, 1.0, 0.0).astype(BF16)
    scale = MLA_QK ** -0.5 * LOG2E
    for h in range(MLA_HEADS):
        lo, hi = h * LANES, (h + 1) * LANES
        q_ref[:, 2 * lo:2 * lo + LANES] = (q_nope[:, lo:hi] * scale).astype(BF16)
        q_ref[:, 2 * lo + LANES:2 * hi] = (_rot64(q_rope[:, lo:hi], c, s) * scale).astype(BF16)
        k_ref[2 * lo:2 * lo + LANES, :] = k_nope[:, lo:hi].T.astype(BF16)
        k_ref[2 * lo + LANES:2 * hi, :] = k_rot_t
        v_ref[:, 2 * lo:2 * lo + LANES] = v[:, lo:hi].astype(BF16)
        v_ref[:, 2 * lo + LANES:2 * hi] = ones_col


def _mla_prep(proj, kr_raw, cm, sm, gq, gkv, wqn, wqr, wk, wv, ql_blk, ckv_blk, seq, tm):
    t = proj.shape[0]
    rank = wqn.shape[0]
    hd = MLA_HEADS * LANES
    per_b = seq // tm
    full = lambda shape: pl.BlockSpec(shape, lambda i: (0, 0))
    row = lambda w: pl.BlockSpec((tm, w), lambda i: (i, 0))
    return pl.pallas_call(
        _mla_prep_kernel,
        out_shape=(jax.ShapeDtypeStruct((t, 2 * hd), BF16),
                   jax.ShapeDtypeStruct(((t // seq) * 2 * hd, seq), BF16),
                   jax.ShapeDtypeStruct((t, 2 * hd), BF16)),
        grid=(t // tm,),
        in_specs=[pl.BlockSpec((tm, rank), lambda i: (i, ql_blk)),
                  pl.BlockSpec((tm, rank), lambda i: (i, ckv_blk)),
                  row(LANES), row(LANES), row(LANES),
                  full((1, rank)), full((1, rank)),
                  full((rank, hd)), full((rank, hd)), full((rank, hd)), full((rank, hd))],
        out_specs=(row(2 * hd),
                   pl.BlockSpec((2 * hd, tm), lambda i: (i // per_b, i % per_b)),
                   row(2 * hd)),
        compiler_params=_cparams("arbitrary"),
        name="mla_prep",
    )(proj, proj, kr_raw, cm, sm, gq, gkv, wqn, wqr, wk, wv)


def _attn_kernel(q_ref, kt_ref, v_ref, o_ref, s_ref, *, rq, tk):
    tq = q_ref.shape[0]
    seq = kt_ref.shape[1]
    for r0 in range(0, tq, rq):
        q = q_ref[r0:r0 + rq, :]
        m_lane = None
        for c0 in range(0, seq, tk):
            s = jnp.dot(q, kt_ref[:, c0:c0 + tk], preferred_element_type=F32)
            s_ref[r0:r0 + rq, c0:c0 + tk] = s
            for l0 in range(0, tk, LANES):
                blk = s[:, l0:l0 + LANES]
                m_lane = blk if m_lane is None else jnp.maximum(m_lane, blk)
        m = jnp.max(m_lane, axis=-1, keepdims=True)
        acc = jnp.zeros((rq, 2 * MLA_V), F32)
        for c0 in range(0, seq, tk):
            p = jnp.exp2(s_ref[r0:r0 + rq, c0:c0 + tk] - m).astype(BF16)
            acc = acc + jnp.dot(p, v_ref[c0:c0 + tk, :], preferred_element_type=F32)
        o_ref[r0:r0 + rq, :] = (acc[:, :MLA_V] / acc[:, MLA_V:MLA_V + 1]).astype(o_ref.dtype)


def _attention(q, kt, v, batch, seq, tq, rq, tk):
    t = q.shape[0]
    nq = seq // tq
    return pl.pallas_call(
        functools.partial(_attn_kernel, rq=rq, tk=tk),
        out_shape=jax.ShapeDtypeStruct((t, MLA_HEADS * MLA_V), BF16),
        grid=(batch, MLA_HEADS, nq),
        in_specs=[pl.BlockSpec((tq, 2 * LANES), lambda b, h, i: (b * nq + i, h)),
                  pl.BlockSpec((2 * LANES, seq), lambda b, h, i: (b * MLA_HEADS + h, 0)),
                  pl.BlockSpec((seq, 2 * MLA_V), lambda b, h, i: (b, h))],
        out_specs=pl.BlockSpec((tq, MLA_V), lambda b, h, i: (b * nq + i, h)),
        scratch_shapes=[pltpu.VMEM((tq, seq), F32)],
        compiler_params=_cparams("arbitrary", "arbitrary", "arbitrary"),
        name="mla_attention",
    )(q, kt, v)


def _ret_kernel(rq_ref, rk_ref, rv_ref, rg_ref, cos_ref, sin_ref, df_ref, db_ref, gn_ref,
                o_ref, krot_ref, sf_ref, st_ref, dmat_ref, xif_ref, zf_ref, xib_ref, zb_ref, *, ch):
    seq, dk = rq_ref.shape
    half = dk // 2
    n_ch = seq // ch
    lgf = jnp.log(jax.nn.sigmoid(df_ref[...]))[:1, :1]
    lgb = jnp.log(jax.nn.sigmoid(db_ref[...]))[:1, :1]

    ri = lax.broadcasted_iota(jnp.int32, (ch, ch), 0)
    ci = lax.broadcasted_iota(jnp.int32, (ch, ch), 1)
    diff = (ri - ci).astype(F32)
    dmat_ref[...] = jnp.where(diff >= 0, jnp.exp(lgf * jnp.maximum(diff, 0.0)),
                              jnp.exp(lgb * jnp.maximum(-diff, 0.0)))
    pos = lax.broadcasted_iota(jnp.int32, (ch, dk), 0).astype(F32)
    xif_ref[...] = jnp.exp(lgf * (pos + 1.0))
    zf_ref[...] = jnp.exp(lgf * (ch - 1.0 - pos))
    xib_ref[...] = jnp.exp(lgb * (ch - pos))
    zb_ref[...] = jnp.exp(lgb * pos)
    gcf = jnp.exp(lgf * float(ch))
    gcb = jnp.exp(lgb * float(ch))
    k_scale = dk ** -0.5

    def rot(x, c, s):
        x1, x2 = x[:, :half], x[:, half:]
        return jnp.concatenate([x1 * c - x2 * s, x1 * s + x2 * c], axis=-1)

    def kv_outer(kz, v):
        return lax.dot_general(kz, v, (((0,), (0,)), ((), ())), preferred_element_type=F32)

    def chunk_pairs(body):
        step = 2 if n_ch % 2 == 0 else 1

        def it(j, carry):
            for u in range(step):
                body(j * step + u, carry)
            return carry

        lax.fori_loop(0, n_ch // step, it, 0)

    st_ref[...] = jnp.zeros(st_ref.shape, F32)

    def fwd(n, carry):
        r0 = pl.multiple_of(n * ch, ch)
        rows = pl.ds(r0, ch)
        k = rot(rk_ref[rows, :].astype(F32), cos_ref[rows, :], sin_ref[rows, :]) * k_scale
        krot_ref[rows, :] = k.astype(BF16)
        sf_ref[n] = st_ref[...].astype(BF16)
        kv = kv_outer((k * zf_ref[...]).astype(BF16), rv_ref[rows, :])
        st_ref[...] = st_ref[...] * gcf + kv
        return carry

    chunk_pairs(fwd)

    st_ref[...] = jnp.zeros(st_ref.shape, F32)

    def bwd(i, carry):
        n = n_ch - 1 - i
        r0 = pl.multiple_of(n * ch, ch)
        rows = pl.ds(r0, ch)
        q = rot(rq_ref[rows, :].astype(F32), cos_ref[rows, :], sin_ref[rows, :])
        kb = krot_ref[rows, :]
        v = rv_ref[rows, :]
        s = lax.dot_general(q.astype(BF16), kb, (((1,), (1,)), ((), ())), preferred_element_type=F32)
        o = jnp.dot((s * dmat_ref[...]).astype(BF16), v, preferred_element_type=F32)
        o = o + jnp.dot((q * xif_ref[...]).astype(BF16), sf_ref[n], preferred_element_type=F32)
        o = o + jnp.dot((q * xib_ref[...]).astype(BF16), st_ref[...].astype(BF16),
                        preferred_element_type=F32)
        mu = jnp.mean(o, axis=-1, keepdims=True)
        oc = o - mu
        var = jnp.mean(oc * oc, axis=-1, keepdims=True)
        on = oc * lax.rsqrt(var + GN_EPS)
        g = rg_ref[rows, :].astype(F32)
        o_ref[rows, :] = (g * jax.nn.sigmoid(g) * (on * gn_ref[...])).astype(o_ref.dtype)
        kv = kv_outer((kb.astype(F32) * zb_ref[...]).astype(BF16), v)
        st_ref[...] = st_ref[...] * gcb + kv
        return carry

    chunk_pairs(bwd)


def _retention(proj, cos_r, sin_r, dec_f, dec_b, gn, batch, seq, blk, ch):
    t = proj.shape[0]
    dk = 2 * LANES
    n_ch = seq // ch
    col = lambda base: pl.BlockSpec((seq, dk), lambda b, h: (b, base + h))
    tab = pl.BlockSpec((seq, LANES), lambda b, h: (b, 0))
    dec = pl.BlockSpec((None, 8, LANES), lambda b, h: (h, 0, 0))
    return pl.pallas_call(
        functools.partial(_ret_kernel, ch=ch),
        out_shape=jax.ShapeDtypeStruct((t, RET_HEADS * dk), BF16),
        grid=(batch, RET_HEADS),
        in_specs=[col(blk["rq"]), col(blk["rk"]), col(blk["rv"]), col(blk["rg"]),
                  tab, tab, dec, dec,
                  pl.BlockSpec((1, dk), lambda b, h: (0, h))],
        out_specs=pl.BlockSpec((seq, dk), lambda b, h: (b, h)),
        scratch_shapes=[pltpu.VMEM((seq, dk), BF16),
                        pltpu.VMEM((n_ch, dk, dk), BF16),
                        pltpu.VMEM((dk, dk), F32),
                        pltpu.VMEM((ch, ch), F32),
                        pltpu.VMEM((ch, dk), F32), pltpu.VMEM((ch, dk), F32),
                        pltpu.VMEM((ch, dk), F32), pltpu.VMEM((ch, dk), F32)],
        compiler_params=_cparams("arbitrary", "arbitrary"),
        name="retention",
    )(proj, proj, proj, proj, cos_r, sin_r, dec_f, dec_b, gn)


def _mixgate_kernel(oa_ref, ob_ref, ga_ref, gb_ref, wa_ref, wb_ref, m_ref):
    ya = jnp.dot(oa_ref[...], wa_ref[...], preferred_element_type=F32)
    yb = jnp.dot(ob_ref[...], wb_ref[...], preferred_element_type=F32)
    m = jax.nn.sigmoid(ga_ref[...].astype(F32)) * ya + jax.nn.sigmoid(gb_ref[...].astype(F32)) * yb
    m_ref[...] = m.astype(m_ref.dtype)


def _mix_gate(o_mla, o_ret, proj, wa, wb, ga_blk, gb_blk, tm):
    t = o_mla.shape[0]
    d = wa.shape[1]
    return pl.pallas_call(
        _mixgate_kernel,
        out_shape=jax.ShapeDtypeStruct((t, d), BF16),
        grid=(t // tm,),
        in_specs=[pl.BlockSpec((tm, o_mla.shape[1]), lambda i: (i, 0)),
                  pl.BlockSpec((tm, o_ret.shape[1]), lambda i: (i, 0)),
                  pl.BlockSpec((tm, d), lambda i: (i, ga_blk)),
                  pl.BlockSpec((tm, d), lambda i: (i, gb_blk)),
                  pl.BlockSpec(wa.shape, lambda i: (0, 0)),
                  pl.BlockSpec(wb.shape, lambda i: (0, 0))],
        out_specs=pl.BlockSpec((tm, d), lambda i: (i, 0)),
        compiler_params=_cparams("arbitrary"),
        name="mix_gate",
    )(o_mla, o_ret, proj, proj, wa, wb)


def _mixout_kernel(m_ref, x_ref, wo_ref, gpost_ref, gate_ref, gpre_ref, sc_ref, sh_ref,
                   wr_ref, br_ref, x1_ref, h2_ref, lg_ref):
    mix = jnp.dot(m_ref[...], wo_ref[...], preferred_element_type=F32)
    x1 = x_ref[...] + gate_ref[...] * _rms(mix, gpost_ref[...])
    x1_ref[...] = x1
    h2 = _rms(x1, gpre_ref[...]) * (1.0 + sc_ref[...]) + sh_ref[...]
    half = h2.shape[1] // 2
    bits = pltpu.bitcast(h2.astype(BF16).astype(F32), jnp.uint32)
    h2_ref[...] = (bits[:, :half] >> 16) | (bits[:, half:] & jnp.uint32(0xFFFF0000))
    h_hi = h2.astype(BF16)
    h_lo = (h2 - h_hi.astype(F32)).astype(BF16)
    wr = wr_ref[...]
    hh = jnp.dot(h_hi, wr, preferred_element_type=F32)
    lh = jnp.dot(h_lo, wr[:, :LANES], preferred_element_type=F32)
    lg_ref[...] = hh[:, :LANES] + (hh[:, LANES:] + lh) + br_ref[...]


def _mix_out(m, x2, wo, gpost, gate_m, gpre, scale_f, shift_f, wr, br, seq, tm):
    t, d = x2.shape
    per_b = seq // tm
    row = pl.BlockSpec((tm, d), lambda i: (i, 0))
    vec = pl.BlockSpec((1, d), lambda i: (0, 0))
    mod = pl.BlockSpec((None, 1, d), lambda i: (i // per_b, 0, 0))
    return pl.pallas_call(
        _mixout_kernel,
        out_shape=(jax.ShapeDtypeStruct((t, d), F32), jax.ShapeDtypeStruct((t, d // 2), jnp.uint32),
                   jax.ShapeDtypeStruct((t, LANES), F32)),
        grid=(t // tm,),
        in_specs=[row, row, pl.BlockSpec((d, d), lambda i: (0, 0)), vec, mod, vec, mod, mod,
                  pl.BlockSpec((d, 2 * LANES), lambda i: (0, 0)),
                  pl.BlockSpec((1, LANES), lambda i: (0, 0))],
        out_specs=(row, pl.BlockSpec((tm, d // 2), lambda i: (i, 0)),
                   pl.BlockSpec((tm, LANES), lambda i: (i, 0))),
        compiler_params=_cparams("arbitrary"),
        name="mix_out",
    )(m, x2, wo, gpost, gate_m, gpre, scale_f, shift_f, wr, br)


def _route_kernel(lg_ref, o_ref, cnt_ref, carry_ref):
    @pl.when(pl.program_id(0) == 0)
    def _():
        carry_ref[...] = jnp.zeros(carry_ref.shape, F32)

    tm = lg_ref.shape[0]
    lane = lax.broadcasted_iota(jnp.int32, (tm, LANES), 1).astype(F32)
    l = jnp.where(lane < N_EXPERTS, lg_ref[...], -jnp.inf)
    vals, idxs = [], []
    for _ in range(TOP_K):
        mx = jnp.max(l, axis=-1, keepdims=True)
        ik = jnp.min(jnp.where(l == mx, lane, float(LANES)), axis=-1, keepdims=True)
        vals.append(mx)
        idxs.append(ik)
        l = jnp.where(lane == ik, -jnp.inf, l)
    es = [jnp.exp(v - vals[0]) for v in vals]
    den = es[0] + es[1] + es[2] + es[3]
    onehot = jnp.zeros((tm, LANES), F32)
    for ik in idxs:
        onehot = onehot + jnp.where(lane == ik, 1.0, 0.0)
    ri = lax.broadcasted_iota(jnp.int32, (tm, tm), 0)
    ci = lax.broadcasted_iota(jnp.int32, (tm, tm), 1)
    tri = jnp.where(ci < ri, 1.0, 0.0).astype(BF16)
    cum = jnp.dot(tri, onehot.astype(BF16), preferred_element_type=F32) + carry_ref[0:1, :]
    packed = jnp.zeros((tm, LANES), F32)
    for k in range(TOP_K):
        rank = jnp.sum(jnp.where(lane == idxs[k], cum, 0.0), axis=-1, keepdims=True)
        packed = packed + jnp.where(lane == float(k), idxs[k], 0.0)
        packed = packed + jnp.where(lane == float(TOP_K + k), rank, 0.0)
        packed = packed + jnp.where(lane == float(2 * TOP_K + k), es[k] / den, 0.0)
    o_ref[...] = packed
    carry_ref[...] = carry_ref[...] + jnp.sum(onehot, axis=0, keepdims=True)
    cnt_ref[...] = carry_ref[...]


def _route(logits, tm):
    t = logits.shape[0]
    return pl.pallas_call(
        _route_kernel,
        out_shape=(jax.ShapeDtypeStruct((t, LANES), F32), jax.ShapeDtypeStruct((8, LANES), F32)),
        grid=(t // tm,),
        in_specs=[pl.BlockSpec((tm, LANES), lambda i: (i, 0))],
        out_specs=(pl.BlockSpec((tm, LANES), lambda i: (i, 0)),
                   pl.BlockSpec((8, LANES), lambda i: (0, 0))),
        scratch_shapes=[pltpu.VMEM((8, LANES), F32)],
        compiler_params=_cparams("arbitrary"),
        name="route",
    )(logits)


def _dispatch_kernel(dest_ref, pad_pos_ref, pad_len_ref, tail_ref, h_ref, xs_ref, z_ref, sem, zsem,
                     *, tm_e):
    tm = h_ref.shape[0]
    zr = z_ref.shape[0]
    base = pl.program_id(0) * (tm * TOP_K)
    pieces = [p for p in (2 ** i for i in range(20)) if SUBLANES <= p <= zr and p < tm_e][::-1]

    def zero_copy(pos, p):
        return pltpu.make_async_copy(z_ref.at[pl.ds(0, p), :], xs_ref.at[pl.ds(pos, p), :], zsem)

    def pad_rows(act):
        for e in range(N_EXPERTS):
            pos = pad_pos_ref[e]
            head = (-pos) & (SUBLANES - 1)
            for j in range(SUBLANES - 1):
                @pl.when(j < head)
                def _(pos=pos, j=j):
                    act(zero_copy(pos + j, 1))

            pos = pos + head
            ln = pad_len_ref[e] - head
            for p in pieces:
                hit = (ln & p) != 0

                @pl.when(hit)
                def _(pos=pos, p=p):
                    act(zero_copy(pl.multiple_of(pos, SUBLANES), p))

                pos = pos + jnp.where(hit, p, 0)

    def tail_rows(act):
        def body(j, carry):
            for r0 in range(0, tm_e, zr):
                act(zero_copy(pl.multiple_of(j * tm_e + r0, SUBLANES), zr))
            return carry

        lax.fori_loop(tail_ref[0], tail_ref[1], body, 0)

    @pl.when(pl.program_id(0) == 0)
    def _():
        z_ref[...] = jnp.zeros(z_ref.shape, z_ref.dtype)
        pad_rows(lambda c: c.start())
        tail_rows(lambda c: c.start())

    def row_copy(t, d):
        return pltpu.make_async_copy(h_ref.at[pl.ds(t, 1), :], xs_ref.at[pl.ds(d, 1), :], sem)

    def issue(t, carry):
        for k in range(TOP_K):
            row_copy(t, dest_ref[base + t * TOP_K + k]).start(priority=k % 2)
        return carry

    lax.fori_loop(0, tm, issue, 0, unroll=8)

    for k in range(TOP_K):
        pltpu.make_async_copy(h_ref, xs_ref.at[pl.ds(0, tm), :], sem).wait()

    @pl.when(pl.program_id(0) == 0)
    def _():
        pad_rows(lambda c: c.wait())
        tail_rows(lambda c: c.wait())


def _dispatch(dest, pad_pos, pad_len, tail, h2, n_rows, tm, tm_e):
    t, d = h2.shape
    zr = min(256, tm_e)
    nsp = 4
    return pl.pallas_call(
        functools.partial(_dispatch_kernel, tm_e=tm_e),
        out_shape=jax.ShapeDtypeStruct((n_rows, d), h2.dtype),
        grid_spec=pltpu.PrefetchScalarGridSpec(
            num_scalar_prefetch=nsp,
            grid=(t // tm,),
            in_specs=[pl.BlockSpec((tm, d), lambda i, *_: (i, 0))],
            out_specs=pl.BlockSpec(memory_space=pl.ANY),
            scratch_shapes=[pltpu.VMEM((zr, d), h2.dtype), pltpu.SemaphoreType.DMA,
                            pltpu.SemaphoreType.DMA],
        ),
        compiler_params=_cparams("arbitrary"),
        name="dispatch",
    )(dest, pad_pos, pad_len, tail, h2)


def _expert_kernel(ge_ref, gs_ref, gn_ref, tail_ref, xs_ref, wg_ref, wu_ref, bg_ref, bu_ref,
                   wd_ref, bd_ref, ys_ref, xu_ref, xb_ref, ab_ref,
                   ob_ref, orem_ref, z_ref, flag_ref, xsem, osem, rsem, zsem, *, rs, pc, nf, nd):
    del ge_ref
    g = pl.program_id(0)
    s = pl.program_id(1)
    n = gn_ref[g]
    row0 = gs_ref[g]
    half = xu_ref.shape[1]
    tf = wg_ref.shape[1]
    td = wd_ref.shape[1]

    def rows(i):
        return pl.ds(pl.multiple_of(i * rs, rs), rs)

    def x_copy(base, i):
        src = xs_ref.at[pl.ds(pl.multiple_of(base + i * rs, rs), rs), :]
        return pltpu.make_async_copy(src, xu_ref.at[rows(i), :], xsem)

    def x_start(base, cnt):
        def start(i, c):
            x_copy(base, i).start()
            return c

        lax.fori_loop(0, cnt, start, 0)

    g_next = jnp.minimum(g + 1, pl.num_programs(0) - 1)
    n_next = jnp.where(g + 1 < pl.num_programs(0), gn_ref[g_next], 0)

    @pl.when(jnp.logical_and(n > 0, s == 1))
    def _():
        x_start(gs_ref[g_next], n_next)

    @pl.when(jnp.logical_and(n > 0, s == 0))
    def _():
        @pl.when(g == 0)
        def _():
            x_start(row0, n)

        def wait(i, c):
            x_copy(row0, i).wait()
            return c

        lax.fori_loop(0, n, wait, 0)

        def unpack(i, c):
            u = xu_ref[rows(i), :]
            lo = pltpu.bitcast(u << 16, F32)
            hi = pltpu.bitcast(u & jnp.uint32(0xFFFF0000), F32)
            xb_ref[rows(i), :half] = lo.astype(BF16)
            xb_ref[rows(i), half:] = hi.astype(BF16)
            return c

        lax.fori_loop(0, n, unpack, 0)

    small = [pc >> j for j in range(1, pc.bit_length())]
    out_bufs = [(ob_ref.at[0], osem.at[0], pc), (ob_ref.at[1], osem.at[1], pc)]
    off = 0
    for j, sz in enumerate(small):
        out_bufs.append((orem_ref.at[pl.ds(off, sz * rs), :], rsem.at[j], sz))
        off += sz * rs
    n_bufs = len(out_bufs)

    def out_wait(k):
        buf, sem, sz = out_bufs[k]
        pltpu.make_async_copy(buf, ys_ref.at[pl.ds(0, sz * rs), 0:td], sem).wait()

    def drain(k):
        @pl.when(flag_ref[k] == 1)
        def _():
            out_wait(k)
            flag_ref[k] = 0

    @pl.when(jnp.logical_and(g == 0, s == 0))
    def _():
        for k in range(n_bufs):
            flag_ref[k] = 0

    n_big = lax.shift_right_logical(n, (2 * pc).bit_length() - 1)
    leftovers = []
    r_next = n_big * (2 * pc * rs)
    for k, sz in [(0, pc)] + [(2 + j, sz) for j, sz in enumerate(small)]:
        present = (n & sz) != 0
        leftovers.append((present, r_next, sz, k))
        r_next = r_next + jnp.where(present, sz * rs, 0)

    def row_slice(r0, m):
        return pl.ds(pl.multiple_of(r0, rs), m)

    @pl.when(jnp.logical_and(n > 0, s < nf))
    def _():
        def act_rows(r0, m, wgb, wub):
            x = xb_ref[row_slice(r0, m), :]
            gt = jnp.dot(x, wgb, preferred_element_type=F32) + bg_ref[...]
            ut = jnp.dot(x, wub, preferred_element_type=F32) + bu_ref[...]
            gt = jnp.minimum(gt, SWIGLU_LIMIT)
            ut = jnp.clip(ut, -SWIGLU_LIMIT, SWIGLU_LIMIT)
            act = (ut + 1.0) * (gt * jax.nn.sigmoid(SWIGLU_ALPHA * gt))
            ab_ref[s, row_slice(r0, m), :] = act.astype(BF16)

        def chunks(r0, sizes):
            wgb = wg_ref[...].astype(BF16)
            wub = wu_ref[...].astype(BF16)
            for m in sizes:
                act_rows(r0, m, wgb, wub)
                r0 = r0 + m

        def big_body(i, c):
            chunks(i * (2 * pc * rs), (pc * rs, pc * rs))
            return c

        lax.fori_loop(0, n_big, big_body, 0)

        for present, r0, sz, _ in leftovers:
            @pl.when(present)
            def _(r0=r0, sz=sz):
                chunks(r0, (sz * rs,))

    @pl.when(jnp.logical_and(n > 0, s >= nf))
    def _():
        d = s - nf

        def out_copy(r0, m, buf, sem, j):
            dst = ys_ref.at[pl.ds(pl.multiple_of(row0 + r0, rs), m), j * td:(j + 1) * td]
            return pltpu.make_async_copy(buf, dst, sem)

        def emit(r0, k, wdb):
            buf, sem, sz = out_bufs[k]
            m = sz * rs
            drain(k)
            hm = m // 2 if sz >= 2 else m
            for h0 in range(0, m, hm):
                a_full = jnp.concatenate([ab_ref[f, row_slice(r0 + h0, hm), :] for f in range(nf)],
                                         axis=1)
                buf[h0:h0 + hm, :] = jnp.dot(a_full, wdb, preferred_element_type=F32) + bd_ref[...]
            for j in range(nd):
                @pl.when(d == j)
                def _(j=j):
                    out_copy(r0, m, buf, sem, j).start()

            flag_ref[k] = 1

        def w_down():
            return wd_ref[...].astype(BF16)

        for present, r0, sz, k in reversed(leftovers):
            @pl.when(present)
            def _(r0=r0, k=k):
                emit(r0, k, w_down())

        def big_body(i, c):
            wdb = w_down()
            emit(i * (2 * pc * rs), 0, wdb)
            emit(i * (2 * pc * rs) + pc * rs, 1, wdb)
            return c

        lax.fori_loop(0, n_big, big_body, 0)

    @pl.when(jnp.logical_and(g == pl.num_programs(0) - 1, s == nf + nd - 1))
    def _():
        for k in range(n_bufs):
            drain(k)
        z_ref[...] = jnp.zeros(z_ref.shape, z_ref.dtype)

        def z_copy(i):
            dst = ys_ref.at[pl.ds(pl.multiple_of(i * rs, rs), rs), :]
            return pltpu.make_async_copy(z_ref, dst, zsem)

        def start(i, c):
            z_copy(i).start()
            return c

        def wait(i, c):
            z_copy(i).wait()
            return c

        lax.fori_loop(tail_ref[0], tail_ref[1], start, 0)
        lax.fori_loop(tail_ref[0], tail_ref[1], wait, 0)


def _experts(ge, gs, gn, tail, xs, w_gu, b_gu, w_dn, b_dn, rs, pc, rmax, tf, td):
    p = xs.shape[0]
    n_e, d, f2 = w_gu.shape
    fdim = f2 // 2
    nf = fdim // tf
    nd = d // td
    n_groups = ge.shape[0]

    def f_idx(g, s, gn):
        return jnp.where(gn[g] > 0, jnp.minimum(s, nf - 1), nf - 1)

    def d_idx(g, s, gn):
        return jnp.where(gn[g] > 0, jnp.maximum(s - nf, 0), nd - 1)

    return pl.pallas_call(
        functools.partial(_expert_kernel, rs=rs, pc=pc, nf=nf, nd=nd),
        out_shape=jax.ShapeDtypeStruct((p, d), F32),
        grid_spec=pltpu.PrefetchScalarGridSpec(
            num_scalar_prefetch=4,
            grid=(n_groups, nf + nd),
            in_specs=[pl.BlockSpec(memory_space=pl.ANY),
                      pl.BlockSpec((None, d, tf), lambda g, s, ge, gs, gn, tl: (ge[g], 0, f_idx(g, s, gn))),
                      pl.BlockSpec((None, d, tf),
                                   lambda g, s, ge, gs, gn, tl: (ge[g], 0, nf + f_idx(g, s, gn))),
                      pl.BlockSpec((None, 1, tf), lambda g, s, ge, gs, gn, tl: (ge[g], 0, f_idx(g, s, gn))),
                      pl.BlockSpec((None, 1, tf),
                                   lambda g, s, ge, gs, gn, tl: (ge[g], 0, nf + f_idx(g, s, gn))),
                      pl.BlockSpec((None, fdim, td), lambda g, s, ge, gs, gn, tl: (ge[g], 0, d_idx(g, s, gn))),
                      pl.BlockSpec((None, 1, td), lambda g, s, ge, gs, gn, tl: (ge[g], 0, d_idx(g, s, gn)))],
            out_specs=pl.BlockSpec(memory_space=pl.ANY),
            scratch_shapes=[pltpu.VMEM((rmax, d // 2), jnp.uint32),
                            pltpu.VMEM((rmax, d), BF16),
                            pltpu.VMEM((nf, rmax, tf), BF16),
                            pltpu.VMEM((2, pc * rs, td), F32),
                            pltpu.VMEM((pc * rs, td), F32),
                            pltpu.VMEM((rs, d), F32),
                            pltpu.SMEM((2 + pc.bit_length(),), jnp.int32),
                            pltpu.SemaphoreType.DMA, pltpu.SemaphoreType.DMA((2,)),
                            pltpu.SemaphoreType.DMA((pc.bit_length(),)),
                            pltpu.SemaphoreType.DMA],
        ),
        compiler_params=_cparams("arbitrary", "arbitrary"),
        name="experts",
    )(ge, gs, gn, tail, xs, w_gu, w_gu, b_gu, b_gu, w_dn, b_dn)


def _combine_kernel(dest_ref, ys_ref, pk_ref, x1_ref, gpost_ref, gate_ref, o_ref, buf_ref, sem):
    tm = x1_ref.shape[0]
    base = pl.program_id(0) * (tm * TOP_K)

    def row_copy(t, k, d):
        return pltpu.make_async_copy(ys_ref.at[pl.ds(d, 1), :], buf_ref.at[k, pl.ds(t, 1), :], sem)

    def issue(t, carry):
        for k in range(TOP_K):
            row_copy(t, k, dest_ref[base + t * TOP_K + k]).start(priority=k % 2)
        return carry

    lax.fori_loop(0, tm, issue, 0, unroll=8)

    for k in range(TOP_K):
        pltpu.make_async_copy(ys_ref.at[pl.ds(0, tm), :], buf_ref.at[k], sem).wait()

    pk = pk_ref[...]
    f = jnp.zeros(x1_ref.shape, F32)
    for k in range(TOP_K):
        w = pk[:, 2 * TOP_K + k:2 * TOP_K + k + 1]
        f = f + buf_ref[k] * w
    o_ref[...] = x1_ref[...] + gate_ref[...] * _rms(f, gpost_ref[...])


def _combine(dest, ys, packed, x1, gpost, gate_f, seq, tm):
    t, d = x1.shape
    per_b = seq // tm
    return pl.pallas_call(
        _combine_kernel,
        out_shape=jax.ShapeDtypeStruct((t, d), F32),
        grid_spec=pltpu.PrefetchScalarGridSpec(
            num_scalar_prefetch=1,
            grid=(t // tm,),
            in_specs=[pl.BlockSpec(memory_space=pl.ANY),
                      pl.BlockSpec((tm, LANES), lambda i, dest: (i, 0)),
                      pl.BlockSpec((tm, d), lambda i, dest: (i, 0)),
                      pl.BlockSpec((1, d), lambda i, dest: (0, 0)),
                      pl.BlockSpec((None, 1, d), lambda i, dest: (i // per_b, 0, 0))],
            out_specs=pl.BlockSpec((tm, d), lambda i, dest: (i, 0)),
            scratch_shapes=[pltpu.VMEM((TOP_K, tm, d), F32), pltpu.SemaphoreType.DMA],
        ),
        compiler_params=_cparams("arbitrary"),
        name="combine",
    )(dest, ys, packed, x1, gpost, gate_f)


def _pad_heads(w, n_heads, width):
    k = w.shape[0]
    w = w.reshape(k, n_heads, width)
    return jnp.pad(w, ((0, 0), (0, 0), (0, LANES - width))).reshape(k, n_heads * LANES)


def _layer(x, c, positions, w_ada, b_ada, g_pre_mix, g_post_mix, g_pre_ffn, g_post_ffn,
           w_in, q_a_norm, w_q_b, kv_a_norm, w_kv_b, w_mla_o, ret_decay_fwd, ret_decay_bwd,
           ret_gn, w_ret_o, w_out, w_router, b_router, w_gate_up, b_gate_up, w_down, b_down):
    batch, seq, d = x.shape
    t = batch * seq
    q_rank = q_a_norm.shape[0]
    kv_rank = kv_a_norm.shape[0]
    hd = RET_HEADS * 2 * LANES
    x2 = x.reshape(t, d)

    c_pad = jnp.pad(c, ((0, 8 - batch), (0, 0)))
    ada = _ada(c_pad, w_ada, b_ada[None, :])[:batch]
    shift_m, scale_m, gate_m, shift_f, scale_f, gate_f = [
        ada[:, i * d:(i + 1) * d].reshape(batch, 1, d) for i in range(N_MOD)]

    o_q, o_kv = 0, q_rank
    o_r = q_rank + kv_rank + MLA_ROPE
    w_wide = w_in[:, o_r:]
    w_main = jnp.concatenate([w_wide, w_in[:, o_q:o_q + q_rank], w_in[:, o_kv:o_kv + kv_rank]],
                             axis=1).astype(BF16)
    w_kr = jnp.pad(w_in[:, o_kv + kv_rank:o_r], ((0, 0), (0, LANES - MLA_ROPE))).astype(BF16)
    n_wide = w_wide.shape[1]
    proj, kr_raw = _in_proj(x2, g_pre_mix[None, :], scale_m, shift_m, w_main, w_kr, seq,
                            tm=min(1024, seq), tn=1024)
    blk_w = 2 * LANES
    blk = {"rq": 0, "rk": hd // blk_w, "rv": 2 * hd // blk_w, "rg": 3 * hd // blk_w}
    ga_blk, gb_blk = (4 * hd) // d, (4 * hd + d) // d
    ql_blk, ckv_blk = n_wide // q_rank, (n_wide + q_rank) // kv_rank

    cos_r, sin_r, cm, sm = _rope_tables(positions.reshape(t, 1), tm=min(512, seq))

    wq = w_q_b.reshape(q_rank, MLA_HEADS, MLA_QK)
    wqn = wq[:, :, :MLA_NOPE].reshape(q_rank, MLA_HEADS * MLA_NOPE).astype(BF16)
    wqr = _pad_heads(wq[:, :, MLA_NOPE:].reshape(q_rank, MLA_HEADS * MLA_ROPE), MLA_HEADS,
                     MLA_ROPE).astype(BF16)
    wkv = w_kv_b.reshape(kv_rank, MLA_HEADS, MLA_NOPE + MLA_V)
    wk = wkv[:, :, :MLA_NOPE].reshape(kv_rank, MLA_HEADS * MLA_NOPE).astype(BF16)
    wv = wkv[:, :, MLA_NOPE:].reshape(kv_rank, MLA_HEADS * MLA_V).astype(BF16)
    q, k, v = _mla_prep(proj, kr_raw, cm, sm, q_a_norm[None, :], kv_a_norm[None, :],
                        wqn, wqr, wk, wv, ql_blk, ckv_blk, seq, tm=min(512, seq))
    o_mla = _attention(q, k, v, batch, seq, tq=min(2048, seq), rq=128, tk=min(512, seq))

    dec_f = jnp.broadcast_to(ret_decay_fwd.astype(F32)[:, None, None], (RET_HEADS, 8, LANES))
    dec_b = jnp.broadcast_to(ret_decay_bwd.astype(F32)[:, None, None], (RET_HEADS, 8, LANES))
    o_ret = _retention(proj, cos_r, sin_r, dec_f, dec_b, ret_gn[None, :], batch, seq, blk,
                       ch=min(256, seq))

    m = _mix_gate(o_mla, o_ret, proj, w_mla_o.astype(BF16), w_ret_o.astype(BF16), ga_blk, gb_blk,
                  tm=min(512, seq))
    wr32 = jnp.pad(w_router, ((0, 0), (0, LANES - N_EXPERTS)))
    wr_hi = wr32.astype(BF16)
    wr = jnp.concatenate([wr_hi, (wr32 - wr_hi.astype(F32)).astype(BF16)], axis=1)
    br = jnp.pad(b_router, (0, LANES - N_EXPERTS))[None, :]
    x1, h2, logits = _mix_out(m, x2, w_out.astype(BF16), g_post_mix[None, :], gate_m,
                              g_pre_ffn[None, :], scale_f, shift_f, wr, br, seq, tm=min(256, seq))

    packed, cnt = _route(logits, tm=min(256, seq))
    e_idx = packed[:, 0:TOP_K].astype(jnp.int32)
    rank = packed[:, TOP_K:2 * TOP_K].astype(jnp.int32)
    counts = cnt[0, :N_EXPERTS].astype(jnp.int32)
    a = t * TOP_K
    rs = min(EXPERT_SUB_ROWS, a // N_EXPERTS)
    rmax = GROUP_SUB * rs
    n_sub = (a + N_EXPERTS * (rs - 1) + rs - 1) // rs
    n_groups = N_EXPERTS + a // rmax
    padded = ((counts + rs - 1) // rs) * rs
    pend = jnp.cumsum(padded)
    pstart = pend - padded
    dest = (pstart[e_idx] + rank).reshape(a).astype(jnp.int32)
    used_sub = (pend[-1] // rs).astype(jnp.int32).reshape(1)
    tail = jnp.concatenate([used_sub, jnp.full((1,), n_sub, jnp.int32)])
    e_sub = padded // rs
    e_grp = (e_sub + GROUP_SUB - 1) // GROUP_SUB
    g_end = jnp.cumsum(e_grp)
    slot = jnp.arange(n_groups, dtype=jnp.int32)
    g_e = jnp.minimum(jnp.sum((g_end[None, :] <= slot[:, None]).astype(jnp.int32), axis=1),
                      N_EXPERTS - 1)
    g_k = slot - (g_end - e_grp)[g_e]
    live = slot < g_end[-1]
    last_e = g_e[jnp.maximum(g_end[-1] - 1, 0)]
    g_e = jnp.where(live, g_e, last_e).astype(jnp.int32)
    e_chunk = (e_sub + jnp.maximum(e_grp, 1) - 1) // jnp.maximum(e_grp, 1)
    g_s = jnp.where(live, pstart[g_e] + g_k * e_chunk[g_e] * rs, 0).astype(jnp.int32)
    g_n = jnp.where(live, jnp.clip(e_sub[g_e] - e_chunk[g_e] * g_k, 0, e_chunk[g_e]), 0).astype(jnp.int32)

    xs = _dispatch(dest, (pstart + counts).astype(jnp.int32), (padded - counts).astype(jnp.int32),
                   tail, h2, n_sub * rs, tm=min(256, seq), tm_e=rs)
    fdim = w_down.shape[1]
    ys = _experts(g_e, g_s, g_n, tail, xs, w_gate_up, b_gate_up[:, None, :], w_down,
                  b_down[:, None, :], rs=rs, pc=EXPERT_PIECE_SUB, rmax=rmax, tf=min(256, fdim), td=min(256, d))
    out = _combine(dest, ys, packed, x1, g_post_ffn[None, :], gate_f, seq, tm=min(256, seq))
    return out.reshape(batch, seq, d)


def kernel(x, c, positions, w_ada, b_ada, g_pre_mix, g_post_mix, g_pre_ffn, g_post_ffn, w_in,
           q_a_norm, w_q_b, kv_a_norm, w_kv_b, w_mla_o, ret_decay_fwd, ret_decay_bwd, ret_gn,
           w_ret_o, w_out, w_router, b_router, w_gate_up, b_gate_up, w_down, b_down):
    params = (w_ada, b_ada, g_pre_mix, g_post_mix, g_pre_ffn, g_post_ffn, w_in, q_a_norm, w_q_b,
              kv_a_norm, w_kv_b, w_mla_o, ret_decay_fwd, ret_decay_bwd, ret_gn, w_ret_o, w_out,
              w_router, b_router, w_gate_up, b_gate_up, w_down, b_down)
    for l in range(w_ada.shape[0]):
        x = _layer(x, c, positions, *[p[l] for p in params])
    return x
```

```python
import functools

import numpy as np
import jax
import jax.numpy as jnp
from jax import lax
from jax.experimental import pallas as pl
from jax.experimental.pallas import tpu as pltpu

F32 = jnp.float32
BF16 = jnp.bfloat16

MLA_HEADS = 8
MLA_NOPE = 128
MLA_ROPE = 64
MLA_V = 128
MLA_QK = MLA_NOPE + MLA_ROPE
RET_HEADS = 8
N_EXPERTS = 32
TOP_K = 4
SWIGLU_LIMIT = 7.0
SWIGLU_ALPHA = 1.702
ROPE_THETA = 10000.0
RMS_EPS = 1e-6
GN_EPS = 1e-6
LOG2E = 1.4426950408889634
N_MOD = 6

LANES = 128
SUBLANES = 8
VMEM_LIMIT = 56 * 1024 * 1024
EXPERT_SUB_ROWS = 256
EXPERT_PIECE_SUB = 4
GROUP_SUB = 9


def _cparams(*sem):
    return pltpu.CompilerParams(dimension_semantics=sem, vmem_limit_bytes=VMEM_LIMIT)


def _rms(x, gain):
    return x * lax.rsqrt(jnp.mean(x * x, axis=-1, keepdims=True) + RMS_EPS) * gain


def _ada_kernel(c_ref, w_ref, b_ref, o_ref):
    o_ref[...] = jnp.dot(c_ref[...], w_ref[...], preferred_element_type=F32,
                         precision=lax.Precision.HIGHEST) + b_ref[...]


def _ada(c_pad, w_ada, b_ada, tn=1024):
    m, d = c_pad.shape
    n = w_ada.shape[1]
    return pl.pallas_call(
        _ada_kernel,
        out_shape=jax.ShapeDtypeStruct((m, n), F32),
        grid=(n // tn,),
        in_specs=[pl.BlockSpec((m, d), lambda j: (0, 0)),
                  pl.BlockSpec((d, tn), lambda j: (0, j)),
                  pl.BlockSpec((1, tn), lambda j: (0, j))],
        out_specs=pl.BlockSpec((m, tn), lambda j: (0, j)),
        compiler_params=_cparams("arbitrary"),
        name="ada",
    )(c_pad, w_ada, b_ada)


def _inproj_kernel(x_ref, g_ref, sc_ref, sh_ref, w_ref, wkr_ref, o_ref, okr_ref, h_ref):
    @pl.when(pl.program_id(1) == 0)
    def _():
        h = _rms(x_ref[...], g_ref[...]) * (1.0 + sc_ref[...]) + sh_ref[...]
        hb = h.astype(BF16)
        h_ref[...] = hb
        okr_ref[...] = jnp.dot(hb, wkr_ref[...], preferred_element_type=F32)

    o_ref[...] = jnp.dot(h_ref[...], w_ref[...], preferred_element_type=F32).astype(o_ref.dtype)


def _in_proj(x2, g, scale, shift, w_main, w_kr, seq, tm, tn):
    t, d = x2.shape
    n = w_main.shape[1]
    per_b = seq // tm
    return pl.pallas_call(
        _inproj_kernel,
        out_shape=(jax.ShapeDtypeStruct((t, n), BF16), jax.ShapeDtypeStruct((t, LANES), F32)),
        grid=(t // tm, n // tn),
        in_specs=[pl.BlockSpec((tm, d), lambda i, j: (i, 0)),
                  pl.BlockSpec((1, d), lambda i, j: (0, 0)),
                  pl.BlockSpec((None, 1, d), lambda i, j: (i // per_b, 0, 0)),
                  pl.BlockSpec((None, 1, d), lambda i, j: (i // per_b, 0, 0)),
                  pl.BlockSpec((d, tn), lambda i, j: (0, j)),
                  pl.BlockSpec((d, LANES), lambda i, j: (0, 0))],
        out_specs=(pl.BlockSpec((tm, tn), lambda i, j: (i, j)),
                   pl.BlockSpec((tm, LANES), lambda i, j: (i, 0))),
        scratch_shapes=[pltpu.VMEM((tm, d), BF16)],
        compiler_params=_cparams("arbitrary", "arbitrary"),
        name="in_proj",
    )(x2, g, scale, shift, w_main, w_kr)


def _rope_kernel(pos_ref, invr_ref, invm_ref, cr_ref, sr_ref, cm_ref, sm_ref):
    pos = pos_ref[...].astype(F32)
    ang_r = pos * invr_ref[...]
    cr_ref[...] = jnp.cos(ang_r)
    sr_ref[...] = jnp.sin(ang_r)
    ang_m = pos * invm_ref[...]
    lane = lax.broadcasted_iota(jnp.int32, ang_m.shape, 1)
    half = MLA_ROPE // 2
    cm_ref[...] = jnp.where(lane < MLA_ROPE, jnp.cos(ang_m), 0.0)
    s = jnp.sin(ang_m)
    sm_ref[...] = jnp.where(lane < half, -s, jnp.where(lane < MLA_ROPE, s, 0.0))


def _rope_tables(pos_col, tm):
    t = pos_col.shape[0]
    half_r = LANES
    inv_r = (1.0 / (np.float32(ROPE_THETA) ** (np.arange(half_r, dtype=np.float32) / np.float32(half_r)))
             ).astype(np.float32)
    half_m = MLA_ROPE // 2
    inv_m32 = (1.0 / (np.float32(ROPE_THETA) ** (np.arange(half_m, dtype=np.float32) / np.float32(half_m)))
               ).astype(np.float32)
    inv_m = np.zeros((LANES,), np.float32)
    inv_m[:half_m] = inv_m32
    inv_m[half_m:2 * half_m] = inv_m32
    tab = jax.ShapeDtypeStruct((t, LANES), F32)
    row = pl.BlockSpec((tm, LANES), lambda i: (i, 0))
    return pl.pallas_call(
        _rope_kernel,
        out_shape=(tab, tab, tab, tab),
        grid=(t // tm,),
        in_specs=[pl.BlockSpec((tm, 1), lambda i: (i, 0)),
                  pl.BlockSpec((1, LANES), lambda i: (0, 0)),
                  pl.BlockSpec((1, LANES), lambda i: (0, 0))],
        out_specs=(row, row, row, row),
        compiler_params=_cparams("arbitrary"),
        name="rope_tables",
    )(pos_col, jnp.asarray(inv_r)[None, :], jnp.asarray(inv_m)[None, :])


def _rot64(x, c, s):
    return x * c + (pltpu.roll(x, 32, 1) + pltpu.roll(x, 96, 1)) * s


def _mla_prep_kernel(ql_ref, ckv_ref, kr_ref, cm_ref, sm_ref, gq_ref, gkv_ref,
                     wqn_ref, wqr_ref, wk_ref, wv_ref, q_ref, k_ref, v_ref):
    c = cm_ref[...]
    s = sm_ref[...]
    qn = _rms(ql_ref[...].astype(F32), gq_ref[...]).astype(BF16)
    q_nope = jnp.dot(qn, wqn_ref[...], preferred_element_type=F32)
    q_rope = jnp.dot(qn, wqr_ref[...], preferred_element_type=F32)
    cn = _rms(ckv_ref[...].astype(F32), gkv_ref[...]).astype(BF16)
    k_nope = jnp.dot(cn, wk_ref[...], preferred_element_type=F32)
    v = jnp.dot(cn, wv_ref[...], preferred_element_type=F32)
    k_rot_t = _rot64(kr_ref[...], c, s).T.astype(BF16)
    lane = lax.broadcasted_iota(jnp.int32, c.shape, 1)
    ones_col = jnp.where(lane == 0, 1.0, 0.0).astype(BF16)
    scale = MLA_QK ** -0.5 * LOG2E
    for h in range(MLA_HEADS):
        lo, hi = h * LANES, (h + 1) * LANES
        q_ref[:, 2 * lo:2 * lo + LANES] = (q_nope[:, lo:hi] * scale).astype(BF16)
        q_ref[:, 2 * lo + LANES:2 * hi] = (_rot64(q_rope[:, lo:hi], c, s) * scale).astype(BF16)
        k_ref[2 * lo:2 * lo + LANES, :] = k_nope[:, lo:hi].T.astype(BF16)
        k_ref[2 * lo + LANES:2 * hi, :] = k_rot_t
        v_ref[:, 2 * lo:2 * lo + LANES] = v[:, lo:hi].astype(BF16)
        v_ref[:, 2 * lo + LANES:2 * hi] = ones_col


def _mla_prep(proj, kr_raw, cm, sm, gq, gkv, wqn, wqr, wk, wv, ql_blk, ckv_blk, seq, tm):
    t = proj.shape[0]
    rank = wqn.shape[0]
    hd = MLA_HEADS * LANES
    per_b = seq // tm
    full = lambda shape: pl.BlockSpec(shape, lambda i: (0, 0))
    row = lambda w: pl.BlockSpec((tm, w), lambda i: (i, 0))
    return pl.pallas_call(
        _mla_prep_kernel,
        out_shape=(jax.ShapeDtypeStruct((t, 2 * hd), BF16),
                   jax.ShapeDtypeStruct(((t // seq) * 2 * hd, seq), BF16),
                   jax.ShapeDtypeStruct((t, 2 * hd), BF16)),
        grid=(t // tm,),
        in_specs=[pl.BlockSpec((tm, rank), lambda i: (i, ql_blk)),
                  pl.BlockSpec((tm, rank), lambda i: (i, ckv_blk)),
                  row(LANES), row(LANES), row(LANES),
                  full((1, rank)), full((1, rank)),
                  full((rank, hd)), full((rank, hd)), full((rank, hd)), full((rank, hd))],
        out_specs=(row(2 * hd),
                   pl.BlockSpec((2 * hd, tm), lambda i: (i // per_b, i % per_b)),
                   row(2 * hd)),
        compiler_params=_cparams("arbitrary"),
        name="mla_prep",
    )(proj, proj, kr_raw, cm, sm, gq, gkv, wqn, wqr, wk, wv)


def _attn_kernel(q_ref, kt_ref, v_ref, o_ref, s_ref, *, rq, tk):
    tq = q_ref.shape[0]
    seq = kt_ref.shape[1]
    for r0 in range(0, tq, rq):
        q = q_ref[r0:r0 + rq, :]
        m_lane = None
        for c0 in range(0, seq, tk):
            s = jnp.dot(q, kt_ref[:, c0:c0 + tk], preferred_element_type=F32)
            s_ref[r0:r0 + rq, c0:c0 + tk] = s
            for l0 in range(0, tk, LANES):
                blk = s[:, l0:l0 + LANES]
                m_lane = blk if m_lane is None else jnp.maximum(m_lane, blk)
        m = jnp.max(m_lane, axis=-1, keepdims=True)
        acc = jnp.zeros((rq, 2 * MLA_V), F32)
        for c0 in range(0, seq, tk):
            p = jnp.exp2(s_ref[r0:r0 + rq, c0:c0 + tk] - m).astype(BF16)
            acc = acc + jnp.dot(p, v_ref[c0:c0 + tk, :], preferred_element_type=F32)
        o_ref[r0:r0 + rq, :] = (acc[:, :MLA_V] / acc[:, MLA_V:MLA_V + 1]).astype(o_ref.dtype)


def _attention(q, kt, v, batch, seq, tq, rq, tk):
    t = q.shape[0]
    nq = seq // tq
    return pl.pallas_call(
        functools.partial(_attn_kernel, rq=rq, tk=tk),
        out_shape=jax.ShapeDtypeStruct((t, MLA_HEADS * MLA_V), BF16),
        grid=(batch, MLA_HEADS, nq),
        in_specs=[pl.BlockSpec((tq, 2 * LANES), lambda b, h, i: (b * nq + i, h)),
                  pl.BlockSpec((2 * LANES, seq), lambda b, h, i: (b * MLA_HEADS + h, 0)),
                  pl.BlockSpec((seq, 2 * MLA_V), lambda b, h, i: (b, h))],
        out_specs=pl.BlockSpec((tq, MLA_V), lambda b, h, i: (b * nq + i, h)),
        scratch_shapes=[pltpu.VMEM((tq, seq), F32)],
        compiler_params=_cparams("arbitrary", "arbitrary", "arbitrary"),
        name="mla_attention",
    )(q, kt, v)


def _ret_kernel(rq_ref, rk_ref, rv_ref, rg_ref, cos_ref, sin_ref, df_ref, db_ref, gn_ref,
                o_ref, krot_ref, sf_ref, st_ref, dmat_ref, xif_ref, zf_ref, xib_ref, zb_ref, *, ch):
    seq, dk = rq_ref.shape
    half = dk // 2
    n_ch = seq // ch
    lgf = jnp.log(jax.nn.sigmoid(df_ref[...]))[:1, :1]
    lgb = jnp.log(jax.nn.sigmoid(db_ref[...]))[:1, :1]

    ri = lax.broadcasted_iota(jnp.int32, (ch, ch), 0)
    ci = lax.broadcasted_iota(jnp.int32, (ch, ch), 1)
    diff = (ri - ci).astype(F32)
    dmat_ref[...] = jnp.where(diff >= 0, jnp.exp(lgf * jnp.maximum(diff, 0.0)),
                              jnp.exp(lgb * jnp.maximum(-diff, 0.0)))
    pos = lax.broadcasted_iota(jnp.int32, (ch, dk), 0).astype(F32)
    xif_ref[...] = jnp.exp(lgf * (pos + 1.0))
    zf_ref[...] = jnp.exp(lgf * (ch - 1.0 - pos))
    xib_ref[...] = jnp.exp(lgb * (ch - pos))
    zb_ref[...] = jnp.exp(lgb * pos)
    gcf = jnp.exp(lgf * float(ch))
    gcb = jnp.exp(lgb * float(ch))
    k_scale = dk ** -0.5

    def rot(x, c, s):
        x1, x2 = x[:, :half], x[:, half:]
        return jnp.concatenate([x1 * c - x2 * s, x1 * s + x2 * c], axis=-1)

    def kv_outer(kz, v):
        return lax.dot_general(kz, v, (((0,), (0,)), ((), ())), preferred_element_type=F32)

    def chunk_pairs(body):
        step = 2 if n_ch % 2 == 0 else 1

        def it(j, carry):
            for u in range(step):
                body(j * step + u, carry)
            return carry

        lax.fori_loop(0, n_ch // step, it, 0)

    st_ref[...] = jnp.zeros(st_ref.shape, F32)

    def fwd(n, carry):
        r0 = pl.multiple_of(n * ch, ch)
        rows = pl.ds(r0, ch)
        k = rot(rk_ref[rows, :].astype(F32), cos_ref[rows, :], sin_ref[rows, :]) * k_scale
        krot_ref[rows, :] = k.astype(BF16)
        sf_ref[n] = st_ref[...].astype(BF16)
        kv = kv_outer((k * zf_ref[...]).astype(BF16), rv_ref[rows, :])
        st_ref[...] = st_ref[...] * gcf + kv
        return carry

    chunk_pairs(fwd)

    st_ref[...] = jnp.zeros(st_ref.shape, F32)

    def bwd(i, carry):
        n = n_ch - 1 - i
        r0 = pl.multiple_of(n * ch, ch)
        rows = pl.ds(r0, ch)
        q = rot(rq_ref[rows, :].astype(F32), cos_ref[rows, :], sin_ref[rows, :])
        kb = krot_ref[rows, :]
        v = rv_ref[rows, :]
        s = lax.dot_general(q.astype(BF16), kb, (((1,), (1,)), ((), ())), preferred_element_type=F32)
        o = jnp.dot((s * dmat_ref[...]).astype(BF16), v, preferred_element_type=F32)
        o = o + jnp.dot((q * xif_ref[...]).astype(BF16), sf_ref[n], preferred_element_type=F32)
        o = o + jnp.dot((q * xib_ref[...]).astype(BF16), st_ref[...].astype(BF16),
                        preferred_element_type=F32)
        mu = jnp.mean(o, axis=-1, keepdims=True)
        oc = o - mu
        var = jnp.mean(oc * oc, axis=-1, keepdims=True)
        on = oc * lax.rsqrt(var + GN_EPS)
        g = rg_ref[rows, :].astype(F32)
        o_ref[rows, :] = (g * jax.nn.sigmoid(g) * (on * gn_ref[...])).astype(o_ref.dtype)
        kv = kv_outer((kb.astype(F32) * zb_ref[...]).astype(BF16), v)
        st_ref[...] = st_ref[...] * gcb + kv
        return carry

    chunk_pairs(bwd)


def _retention(proj, cos_r, sin_r, dec_f, dec_b, gn, batch, seq, blk, ch):
    t = proj.shape[0]
    dk = 2 * LANES
    n_ch = seq // ch
    col = lambda base: pl.BlockSpec((seq, dk), lambda b, h: (b, base + h))
    tab = pl.BlockSpec((seq, LANES), lambda b, h: (b, 0))
    dec = pl.BlockSpec((None, 8, LANES), lambda b, h: (h, 0, 0))
    return pl.pallas_call(
        functools.partial(_ret_kernel, ch=ch),
        out_shape=jax.ShapeDtypeStruct((t, RET_HEADS * dk), BF16),
        grid=(batch, RET_HEADS),
        in_specs=[col(blk["rq"]), col(blk["rk"]), col(blk["rv"]), col(blk["rg"]),
                  tab, tab, dec, dec,
                  pl.BlockSpec((1, dk), lambda b, h: (0, h))],
        out_specs=pl.BlockSpec((seq, dk), lambda b, h: (b, h)),
        scratch_shapes=[pltpu.VMEM((seq, dk), BF16),
                        pltpu.VMEM((n_ch, dk, dk), BF16),
                        pltpu.VMEM((dk, dk), F32),
                        pltpu.VMEM((ch, ch), F32),
                        pltpu.VMEM((ch, dk), F32), pltpu.VMEM((ch, dk), F32),
                        pltpu.VMEM((ch, dk), F32), pltpu.VMEM((ch, dk), F32)],
        compiler_params=_cparams("arbitrary", "arbitrary"),
        name="retention",
    )(proj, proj, proj, proj, cos_r, sin_r, dec_f, dec_b, gn)


def _mixgate_kernel(oa_ref, ob_ref, ga_ref, gb_ref, wa_ref, wb_ref, m_ref):
    ya = jnp.dot(oa_ref[...], wa_ref[...], preferred_element_type=F32)
    yb = jnp.dot(ob_ref[...], wb_ref[...], preferred_element_type=F32)
    m = jax.nn.sigmoid(ga_ref[...].astype(F32)) * ya + jax.nn.sigmoid(gb_ref[...].astype(F32)) * yb
    m_ref[...] = m.astype(m_ref.dtype)


def _mix_gate(o_mla, o_ret, proj, wa, wb, ga_blk, gb_blk, tm):
    t = o_mla.shape[0]
    d = wa.shape[1]
    return pl.pallas_call(
        _mixgate_kernel,
        out_shape=jax.ShapeDtypeStruct((t, d), BF16),
        grid=(t // tm,),
        in_specs=[pl.BlockSpec((tm, o_mla.shape[1]), lambda i: (i, 0)),
                  pl.BlockSpec((tm, o_ret.shape[1]), lambda i: (i, 0)),
                  pl.BlockSpec((tm, d), lambda i: (i, ga_blk)),
                  pl.BlockSpec((tm, d), lambda i: (i, gb_blk)),
                  pl.BlockSpec(wa.shape, lambda i: (0, 0)),
                  pl.BlockSpec(wb.shape, lambda i: (0, 0))],
        out_specs=pl.BlockSpec((tm, d), lambda i: (i, 0)),
        compiler_params=_cparams("arbitrary"),
        name="mix_gate",
    )(o_mla, o_ret, proj, proj, wa, wb)


def _mixout_kernel(m_ref, x_ref, wo_ref, gpost_ref, gate_ref, gpre_ref, sc_ref, sh_ref,
                   wr_ref, br_ref, x1_ref, h2_ref, pk_ref, cnt_ref, carry_ref):
    mix = jnp.dot(m_ref[...], wo_ref[...], preferred_element_type=F32)
    x1 = x_ref[...] + gate_ref[...] * _rms(mix, gpost_ref[...])
    x1_ref[...] = x1
    h2 = _rms(x1, gpre_ref[...]) * (1.0 + sc_ref[...]) + sh_ref[...]
    half = h2.shape[1] // 2
    bits = pltpu.bitcast(h2.astype(BF16).astype(F32), jnp.uint32)
    h2_ref[...] = (bits[:, :half] >> 16) | (bits[:, half:] & jnp.uint32(0xFFFF0000))
    h_hi = h2.astype(BF16)
    h_lo = (h2 - h_hi.astype(F32)).astype(BF16)
    wr = wr_ref[...]
    hh = jnp.dot(h_hi, wr, preferred_element_type=F32)
    lh = jnp.dot(h_lo, wr[:, :LANES], preferred_element_type=F32)
    logits = hh[:, :LANES] + (hh[:, LANES:] + lh) + br_ref[...]

    @pl.when(pl.program_id(0) == 0)
    def _():
        carry_ref[...] = jnp.zeros(carry_ref.shape, F32)

    pk_ref[...] = _route_tile(logits, carry_ref)
    cnt_ref[...] = carry_ref[...]


def _mix_out(m, x2, wo, gpost, gate_m, gpre, scale_f, shift_f, wr, br, seq, tm):
    t, d = x2.shape
    per_b = seq // tm
    row = pl.BlockSpec((tm, d), lambda i: (i, 0))
    vec = pl.BlockSpec((1, d), lambda i: (0, 0))
    mod = pl.BlockSpec((None, 1, d), lambda i: (i // per_b, 0, 0))
    return pl.pallas_call(
        _mixout_kernel,
        out_shape=(jax.ShapeDtypeStruct((t, d), F32), jax.ShapeDtypeStruct((t, d // 2), jnp.uint32),
                   jax.ShapeDtypeStruct((t, LANES), F32), jax.ShapeDtypeStruct((8, LANES), F32)),
        grid=(t // tm,),
        in_specs=[row, row, pl.BlockSpec((d, d), lambda i: (0, 0)), vec, mod, vec, mod, mod,
                  pl.BlockSpec((d, 2 * LANES), lambda i: (0, 0)),
                  pl.BlockSpec((1, LANES), lambda i: (0, 0))],
        out_specs=(row, pl.BlockSpec((tm, d // 2), lambda i: (i, 0)),
                   pl.BlockSpec((tm, LANES), lambda i: (i, 0)),
                   pl.BlockSpec((8, LANES), lambda i: (0, 0))),
        scratch_shapes=[pltpu.VMEM((8, LANES), F32)],
        compiler_params=_cparams("arbitrary"),
        name="mix_out",
    )(m, x2, wo, gpost, gate_m, gpre, scale_f, shift_f, wr, br)


def _route_tile(logits, carry_ref):
    tm = logits.shape[0]
    lane = lax.broadcasted_iota(jnp.int32, (tm, LANES), 1).astype(F32)
    l = jnp.where(lane < N_EXPERTS, logits, -jnp.inf)
    vals, idxs = [], []
    for _ in range(TOP_K):
        mx = jnp.max(l, axis=-1, keepdims=True)
        ik = jnp.min(jnp.where(l == mx, lane, float(LANES)), axis=-1, keepdims=True)
        vals.append(mx)
        idxs.append(ik)
        l = jnp.where(lane == ik, -jnp.inf, l)
    es = [jnp.exp(v - vals[0]) for v in vals]
    den = es[0] + es[1] + es[2] + es[3]
    onehot = jnp.zeros((tm, LANES), F32)
    for ik in idxs:
        onehot = onehot + jnp.where(lane == ik, 1.0, 0.0)
    ri = lax.broadcasted_iota(jnp.int32, (tm, tm), 0)
    ci = lax.broadcasted_iota(jnp.int32, (tm, tm), 1)
    tri = jnp.where(ci < ri, 1.0, 0.0).astype(BF16)
    cum = jnp.dot(tri, onehot.astype(BF16), preferred_element_type=F32) + carry_ref[0:1, :]
    packed = jnp.zeros((tm, LANES), F32)
    for k in range(TOP_K):
        rank = jnp.sum(jnp.where(lane == idxs[k], cum, 0.0), axis=-1, keepdims=True)
        packed = packed + jnp.where(lane == float(k), idxs[k], 0.0)
        packed = packed + jnp.where(lane == float(TOP_K + k), rank, 0.0)
        packed = packed + jnp.where(lane == float(2 * TOP_K + k), es[k] / den, 0.0)
    carry_ref[...] = carry_ref[...] + jnp.sum(onehot, axis=0, keepdims=True)
    return packed


def _dispatch_kernel(dest_ref, pad_pos_ref, pad_len_ref, tail_ref, h_ref, xs_ref, z_ref, sem, zsem,
                     *, tm_e):
    tm = h_ref.shape[0]
    zr = z_ref.shape[0]
    base = pl.program_id(0) * (tm * TOP_K)
    pieces = [p for p in (2 ** i for i in range(20)) if SUBLANES <= p <= zr and p < tm_e][::-1]

    def zero_copy(pos, p):
        return pltpu.make_async_copy(z_ref.at[pl.ds(0, p), :], xs_ref.at[pl.ds(pos, p), :], zsem)

    def pad_rows(act):
        for e in range(N_EXPERTS):
            pos = pad_pos_ref[e]
            head = (-pos) & (SUBLANES - 1)
            for j in range(SUBLANES - 1):
                @pl.when(j < head)
                def _(pos=pos, j=j):
                    act(zero_copy(pos + j, 1))

            pos = pos + head
            ln = pad_len_ref[e] - head
            for p in pieces:
                hit = (ln & p) != 0

                @pl.when(hit)
                def _(pos=pos, p=p):
                    act(zero_copy(pl.multiple_of(pos, SUBLANES), p))

                pos = pos + jnp.where(hit, p, 0)

    def tail_rows(act):
        def body(j, carry):
            for r0 in range(0, tm_e, zr):
                act(zero_copy(pl.multiple_of(j * tm_e + r0, SUBLANES), zr))
            return carry

        lax.fori_loop(tail_ref[0], tail_ref[1], body, 0)

    @pl.when(pl.program_id(0) == 0)
    def _():
        z_ref[...] = jnp.zeros(z_ref.shape, z_ref.dtype)
        pad_rows(lambda c: c.start())
        tail_rows(lambda c: c.start())

    def row_copy(t, d):
        return pltpu.make_async_copy(h_ref.at[pl.ds(t, 1), :], xs_ref.at[pl.ds(d, 1), :], sem)

    def issue(t, carry):
        for k in range(TOP_K):
            row_copy(t, dest_ref[base + t * TOP_K + k]).start(priority=k % 2)
        return carry

    lax.fori_loop(0, tm, issue, 0, unroll=8)

    for k in range(TOP_K):
        pltpu.make_async_copy(h_ref, xs_ref.at[pl.ds(0, tm), :], sem).wait()

    @pl.when(pl.program_id(0) == 0)
    def _():
        pad_rows(lambda c: c.wait())
        tail_rows(lambda c: c.wait())


def _dispatch(dest, pad_pos, pad_len, tail, h2, n_rows, tm, tm_e):
    t, d = h2.shape
    zr = min(256, tm_e)
    nsp = 4
    return pl.pallas_call(
        functools.partial(_dispatch_kernel, tm_e=tm_e),
        out_shape=jax.ShapeDtypeStruct((n_rows, d), h2.dtype),
        grid_spec=pltpu.PrefetchScalarGridSpec(
            num_scalar_prefetch=nsp,
            grid=(t // tm,),
            in_specs=[pl.BlockSpec((tm, d), lambda i, *_: (i, 0))],
            out_specs=pl.BlockSpec(memory_space=pl.ANY),
            scratch_shapes=[pltpu.VMEM((zr, d), h2.dtype), pltpu.SemaphoreType.DMA,
                            pltpu.SemaphoreType.DMA],
        ),
        compiler_params=_cparams("arbitrary"),
        name="dispatch",
    )(dest, pad_pos, pad_len, tail, h2)


def _expert_kernel(ge_ref, gs_ref, gn_ref, tail_ref, xs_ref, wg_ref, wu_ref, bg_ref, bu_ref,
                   wd_ref, bd_ref, ys_ref, xu_ref, xb_ref, ab_ref,
                   ob_ref, orem_ref, z_ref, flag_ref, xsem, osem, rsem, zsem, *, rs, pc, nf, nd):
    del ge_ref
    g = pl.program_id(0)
    s = pl.program_id(1)
    n = gn_ref[g]
    row0 = gs_ref[g]
    half = xu_ref.shape[1]
    tf = wg_ref.shape[1]
    td = wd_ref.shape[1]

    def rows(i):
        return pl.ds(pl.multiple_of(i * rs, rs), rs)

    def x_copy(base, i):
        src = xs_ref.at[pl.ds(pl.multiple_of(base + i * rs, rs), rs), :]
        return pltpu.make_async_copy(src, xu_ref.at[rows(i), :], xsem)

    def x_start(base, cnt):
        def start(i, c):
            x_copy(base, i).start()
            return c

        lax.fori_loop(0, cnt, start, 0)

    g_next = jnp.minimum(g + 1, pl.num_programs(0) - 1)
    n_next = jnp.where(g + 1 < pl.num_programs(0), gn_ref[g_next], 0)

    @pl.when(jnp.logical_and(n > 0, s == 1))
    def _():
        x_start(gs_ref[g_next], n_next)

    @pl.when(jnp.logical_and(n > 0, s == 0))
    def _():
        @pl.when(g == 0)
        def _():
            x_start(row0, n)

        def wait(i, c):
            x_copy(row0, i).wait()
            return c

        lax.fori_loop(0, n, wait, 0)

        def unpack(i, c):
            u = xu_ref[rows(i), :]
            lo = pltpu.bitcast(u << 16, F32)
            hi = pltpu.bitcast(u & jnp.uint32(0xFFFF0000), F32)
            xb_ref[rows(i), :half] = lo.astype(BF16)
            xb_ref[rows(i), half:] = hi.astype(BF16)
            return c

        lax.fori_loop(0, n, unpack, 0)

    small = [pc >> j for j in range(1, pc.bit_length())]
    out_bufs = [(ob_ref.at[0], osem.at[0], pc), (ob_ref.at[1], osem.at[1], pc)]
    off = 0
    for j, sz in enumerate(small):
        out_bufs.append((orem_ref.at[pl.ds(off, sz * rs), :], rsem.at[j], sz))
        off += sz * rs
    n_bufs = len(out_bufs)

    def out_wait(k):
        buf, sem, sz = out_bufs[k]
        pltpu.make_async_copy(buf, ys_ref.at[pl.ds(0, sz * rs), 0:td], sem).wait()

    def drain(k):
        @pl.when(flag_ref[k] == 1)
        def _():
            out_wait(k)
            flag_ref[k] = 0

    @pl.when(jnp.logical_and(g == 0, s == 0))
    def _():
        for k in range(n_bufs):
            flag_ref[k] = 0

    n_big = lax.shift_right_logical(n, (2 * pc).bit_length() - 1)
    leftovers = []
    r_next = n_big * (2 * pc * rs)
    for k, sz in [(0, pc)] + [(2 + j, sz) for j, sz in enumerate(small)]:
        present = (n & sz) != 0
        leftovers.append((present, r_next, sz, k))
        r_next = r_next + jnp.where(present, sz * rs, 0)

    def row_slice(r0, m):
        return pl.ds(pl.multiple_of(r0, rs), m)

    @pl.when(jnp.logical_and(n > 0, s < nf))
    def _():
        def act_rows(r0, m, wgb, wub):
            x = xb_ref[row_slice(r0, m), :]
            gt = jnp.dot(x, wgb, preferred_element_type=F32) + bg_ref[...]
            ut = jnp.dot(x, wub, preferred_element_type=F32) + bu_ref[...]
            gt = jnp.minimum(gt, SWIGLU_LIMIT)
            ut = jnp.clip(ut, -SWIGLU_LIMIT, SWIGLU_LIMIT)
            act = (ut + 1.0) * (gt * jax.nn.sigmoid(SWIGLU_ALPHA * gt))
            ab_ref[s, row_slice(r0, m), :] = act.astype(BF16)

        def chunks(r0, sizes):
            wgb = wg_ref[...].astype(BF16)
            wub = wu_ref[...].astype(BF16)
            for m in sizes:
                act_rows(r0, m, wgb, wub)
                r0 = r0 + m

        def big_body(i, c):
            chunks(i * (2 * pc * rs), (pc * rs, pc * rs))
            return c

        lax.fori_loop(0, n_big, big_body, 0)

        for present, r0, sz, _ in leftovers:
            @pl.when(present)
            def _(r0=r0, sz=sz):
                chunks(r0, (sz * rs,))

    @pl.when(jnp.logical_and(n > 0, s >= nf))
    def _():
        d = s - nf

        def out_copy(r0, m, buf, sem, j):
            dst = ys_ref.at[pl.ds(pl.multiple_of(row0 + r0, rs), m), j * td:(j + 1) * td]
            return pltpu.make_async_copy(buf, dst, sem)

        def emit(r0, k, wdb):
            buf, sem, sz = out_bufs[k]
            m = sz * rs
            drain(k)
            hm = m // 2 if sz >= 2 else m
            for h0 in range(0, m, hm):
                a_full = jnp.concatenate([ab_ref[f, row_slice(r0 + h0, hm), :] for f in range(nf)],
                                         axis=1)
                buf[h0:h0 + hm, :] = jnp.dot(a_full, wdb, preferred_element_type=F32) + bd_ref[...]
            for j in range(nd):
                @pl.when(d == j)
                def _(j=j):
                    out_copy(r0, m, buf, sem, j).start()

            flag_ref[k] = 1

        def w_down():
            return wd_ref[...].astype(BF16)

        for present, r0, sz, k in reversed(leftovers):
            @pl.when(present)
            def _(r0=r0, k=k):
                emit(r0, k, w_down())

        def big_body(i, c):
            wdb = w_down()
            emit(i * (2 * pc * rs), 0, wdb)
            emit(i * (2 * pc * rs) + pc * rs, 1, wdb)
            return c

        lax.fori_loop(0, n_big, big_body, 0)

    @pl.when(jnp.logical_and(g == pl.num_programs(0) - 1, s == nf + nd - 1))
    def _():
        for k in range(n_bufs):
            drain(k)
        z_ref[...] = jnp.zeros(z_ref.shape, z_ref.dtype)

        def z_copy(i):
            dst = ys_ref.at[pl.ds(pl.multiple_of(i * rs, rs), rs), :]
            return pltpu.make_async_copy(z_ref, dst, zsem)

        def start(i, c):
            z_copy(i).start()
            return c

        def wait(i, c):
            z_copy(i).wait()
            return c

        lax.fori_loop(tail_ref[0], tail_ref[1], start, 0)
        lax.fori_loop(tail_ref[0], tail_ref[1], wait, 0)


def _experts(ge, gs, gn, tail, xs, w_gu, b_gu, w_dn, b_dn, rs, pc, rmax, tf, td):
    p = xs.shape[0]
    n_e, d, f2 = w_gu.shape
    fdim = f2 // 2
    nf = fdim // tf
    nd = d // td
    n_groups = ge.shape[0]

    def f_idx(g, s, gn):
        return jnp.where(gn[g] > 0, jnp.minimum(s, nf - 1), nf - 1)

    def d_idx(g, s, gn):
        return jnp.where(gn[g] > 0, jnp.maximum(s - nf, 0), nd - 1)

    return pl.pallas_call(
        functools.partial(_expert_kernel, rs=rs, pc=pc, nf=nf, nd=nd),
        out_shape=jax.ShapeDtypeStruct((p, d), F32),
        grid_spec=pltpu.PrefetchScalarGridSpec(
            num_scalar_prefetch=4,
            grid=(n_groups, nf + nd),
            in_specs=[pl.BlockSpec(memory_space=pl.ANY),
                      pl.BlockSpec((None, d, tf), lambda g, s, ge, gs, gn, tl: (ge[g], 0, f_idx(g, s, gn))),
                      pl.BlockSpec((None, d, tf),
                                   lambda g, s, ge, gs, gn, tl: (ge[g], 0, nf + f_idx(g, s, gn))),
                      pl.BlockSpec((None, 1, tf), lambda g, s, ge, gs, gn, tl: (ge[g], 0, f_idx(g, s, gn))),
                      pl.BlockSpec((None, 1, tf),
                                   lambda g, s, ge, gs, gn, tl: (ge[g], 0, nf + f_idx(g, s, gn))),
                      pl.BlockSpec((None, fdim, td), lambda g, s, ge, gs, gn, tl: (ge[g], 0, d_idx(g, s, gn))),
                      pl.BlockSpec((None, 1, td), lambda g, s, ge, gs, gn, tl: (ge[g], 0, d_idx(g, s, gn)))],
            out_specs=pl.BlockSpec(memory_space=pl.ANY),
            scratch_shapes=[pltpu.VMEM((rmax, d // 2), jnp.uint32),
                            pltpu.VMEM((rmax, d), BF16),
                            pltpu.VMEM((nf, rmax, tf), BF16),
                            pltpu.VMEM((2, pc * rs, td), F32),
                            pltpu.VMEM((pc * rs, td), F32),
                            pltpu.VMEM((rs, d), F32),
                            pltpu.SMEM((2 + pc.bit_length(),), jnp.int32),
                            pltpu.SemaphoreType.DMA, pltpu.SemaphoreType.DMA((2,)),
                            pltpu.SemaphoreType.DMA((pc.bit_length(),)),
                            pltpu.SemaphoreType.DMA],
        ),
        compiler_params=_cparams("arbitrary", "arbitrary"),
        name="experts",
    )(ge, gs, gn, tail, xs, w_gu, w_gu, b_gu, b_gu, w_dn, b_dn)


def _combine_kernel(dest_ref, ys_ref, pk_ref, x1_ref, gpost_ref, gate_ref, o_ref, buf_ref, sem):
    tm = x1_ref.shape[0]
    base = pl.program_id(0) * (tm * TOP_K)

    def row_copy(t, k, d):
        return pltpu.make_async_copy(ys_ref.at[pl.ds(d, 1), :], buf_ref.at[k, pl.ds(t, 1), :], sem)

    def issue(t, carry):
        for k in range(TOP_K):
            row_copy(t, k, dest_ref[base + t * TOP_K + k]).start(priority=k % 2)
        return carry

    lax.fori_loop(0, tm, issue, 0, unroll=8)

    for k in range(TOP_K):
        pltpu.make_async_copy(ys_ref.at[pl.ds(0, tm), :], buf_ref.at[k], sem).wait()

    pk = pk_ref[...]
    f = jnp.zeros(x1_ref.shape, F32)
    for k in range(TOP_K):
        w = pk[:, 2 * TOP_K + k:2 * TOP_K + k + 1]
        f = f + buf_ref[k] * w
    o_ref[...] = x1_ref[...] + gate_ref[...] * _rms(f, gpost_ref[...])


def _combine(dest, ys, packed, x1, gpost, gate_f, seq, tm):
    t, d = x1.shape
    per_b = seq // tm
    return pl.pallas_call(
        _combine_kernel,
        out_shape=jax.ShapeDtypeStruct((t, d), F32),
        grid_spec=pltpu.PrefetchScalarGridSpec(
            num_scalar_prefetch=1,
            grid=(t // tm,),
            in_specs=[pl.BlockSpec(memory_space=pl.ANY),
                      pl.BlockSpec((tm, LANES), lambda i, dest: (i, 0)),
                      pl.BlockSpec((tm, d), lambda i, dest: (i, 0)),
                      pl.BlockSpec((1, d), lambda i, dest: (0, 0)),
                      pl.BlockSpec((None, 1, d), lambda i, dest: (i // per_b, 0, 0))],
            out_specs=pl.BlockSpec((tm, d), lambda i, dest: (i, 0)),
            scratch_shapes=[pltpu.VMEM((TOP_K, tm, d), F32), pltpu.SemaphoreType.DMA],
        ),
        compiler_params=_cparams("arbitrary"),
        name="combine",
    )(dest, ys, packed, x1, gpost, gate_f)


def _pad_heads(w, n_heads, width):
    k = w.shape[0]
    w = w.reshape(k, n_heads, width)
    return jnp.pad(w, ((0, 0), (0, 0), (0, LANES - width))).reshape(k, n_heads * LANES)


def _layer(x, c, positions, w_ada, b_ada, g_pre_mix, g_post_mix, g_pre_ffn, g_post_ffn,
           w_in, q_a_norm, w_q_b, kv_a_norm, w_kv_b, w_mla_o, ret_decay_fwd, ret_decay_bwd,
           ret_gn, w_ret_o, w_out, w_router, b_router, w_gate_up, b_gate_up, w_down, b_down):
    batch, seq, d = x.shape
    t = batch * seq
    q_rank = q_a_norm.shape[0]
    kv_rank = kv_a_norm.shape[0]
    hd = RET_HEADS * 2 * LANES
    x2 = x.reshape(t, d)

    c_pad = jnp.pad(c, ((0, 8 - batch), (0, 0)))
    ada = _ada(c_pad, w_ada, b_ada[None, :])[:batch]
    shift_m, scale_m, gate_m, shift_f, scale_f, gate_f = [
        ada[:, i * d:(i + 1) * d].reshape(batch, 1, d) for i in range(N_MOD)]

    o_q, o_kv = 0, q_rank
    o_r = q_rank + kv_rank + MLA_ROPE
    w_wide = w_in[:, o_r:]
    w_main = jnp.concatenate([w_wide, w_in[:, o_q:o_q + q_rank], w_in[:, o_kv:o_kv + kv_rank]],
                             axis=1).astype(BF16)
    w_kr = jnp.pad(w_in[:, o_kv + kv_rank:o_r], ((0, 0), (0, LANES - MLA_ROPE))).astype(BF16)
    n_wide = w_wide.shape[1]
    proj, kr_raw = _in_proj(x2, g_pre_mix[None, :], scale_m, shift_m, w_main, w_kr, seq,
                            tm=min(1024, seq), tn=1024)
    blk_w = 2 * LANES
    blk = {"rq": 0, "rk": hd // blk_w, "rv": 2 * hd // blk_w, "rg": 3 * hd // blk_w}
    ga_blk, gb_blk = (4 * hd) // d, (4 * hd + d) // d
    ql_blk, ckv_blk = n_wide // q_rank, (n_wide + q_rank) // kv_rank

    cos_r, sin_r, cm, sm = _rope_tables(positions.reshape(t, 1), tm=min(512, seq))

    wq = w_q_b.reshape(q_rank, MLA_HEADS, MLA_QK)
    wqn = wq[:, :, :MLA_NOPE].reshape(q_rank, MLA_HEADS * MLA_NOPE).astype(BF16)
    wqr = _pad_heads(wq[:, :, MLA_NOPE:].reshape(q_rank, MLA_HEADS * MLA_ROPE), MLA_HEADS,
                     MLA_ROPE).astype(BF16)
    wkv = w_kv_b.reshape(kv_rank, MLA_HEADS, MLA_NOPE + MLA_V)
    wk = wkv[:, :, :MLA_NOPE].reshape(kv_rank, MLA_HEADS * MLA_NOPE).astype(BF16)
    wv = wkv[:, :, MLA_NOPE:].reshape(kv_rank, MLA_HEADS * MLA_V).astype(BF16)
    q, k, v = _mla_prep(proj, kr_raw, cm, sm, q_a_norm[None, :], kv_a_norm[None, :],
                        wqn, wqr, wk, wv, ql_blk, ckv_blk, seq, tm=min(512, seq))
    o_mla = _attention(q, k, v, batch, seq, tq=min(2048, seq), rq=128, tk=min(512, seq))

    dec_f = jnp.broadcast_to(ret_decay_fwd.astype(F32)[:, None, None], (RET_HEADS, 8, LANES))
    dec_b = jnp.broadcast_to(ret_decay_bwd.astype(F32)[:, None, None], (RET_HEADS, 8, LANES))
    o_ret = _retention(proj, cos_r, sin_r, dec_f, dec_b, ret_gn[None, :], batch, seq, blk,
                       ch=min(256, seq))

    m = _mix_gate(o_mla, o_ret, proj, w_mla_o.astype(BF16), w_ret_o.astype(BF16), ga_blk, gb_blk,
                  tm=min(512, seq))
    wr32 = jnp.pad(w_router, ((0, 0), (0, LANES - N_EXPERTS)))
    wr_hi = wr32.astype(BF16)
    wr = jnp.concatenate([wr_hi, (wr32 - wr_hi.astype(F32)).astype(BF16)], axis=1)
    br = jnp.pad(b_router, (0, LANES - N_EXPERTS))[None, :]
    x1, h2, packed, cnt = _mix_out(m, x2, w_out.astype(BF16), g_post_mix[None, :], gate_m,
                              g_pre_ffn[None, :], scale_f, shift_f, wr, br, seq, tm=min(256, seq))

    e_idx = packed[:, 0:TOP_K].astype(jnp.int32)
    rank = packed[:, TOP_K:2 * TOP_K].astype(jnp.int32)
    counts = cnt[0, :N_EXPERTS].astype(jnp.int32)
    a = t * TOP_K
    rs = min(EXPERT_SUB_ROWS, a // N_EXPERTS)
    rmax = GROUP_SUB * rs
    n_sub = (a + N_EXPERTS * (rs - 1) + rs - 1) // rs
    n_groups = N_EXPERTS + a // rmax
    padded = ((counts + rs - 1) // rs) * rs
    pend = jnp.cumsum(padded)
    pstart = pend - padded
    dest = (pstart[e_idx] + rank).reshape(a).astype(jnp.int32)
    used_sub = (pend[-1] // rs).astype(jnp.int32).reshape(1)
    tail = jnp.concatenate([used_sub, jnp.full((1,), n_sub, jnp.int32)])
    e_sub = padded // rs
    e_grp = (e_sub + GROUP_SUB - 1) // GROUP_SUB
    g_end = jnp.cumsum(e_grp)
    slot = jnp.arange(n_groups, dtype=jnp.int32)
    g_e = jnp.minimum(jnp.sum((g_end[None, :] <= slot[:, None]).astype(jnp.int32), axis=1),
                      N_EXPERTS - 1)
    g_k = slot - (g_end - e_grp)[g_e]
    live = slot < g_end[-1]
    last_e = g_e[jnp.maximum(g_end[-1] - 1, 0)]
    g_e = jnp.where(live, g_e, last_e).astype(jnp.int32)
    e_chunk = (e_sub + jnp.maximum(e_grp, 1) - 1) // jnp.maximum(e_grp, 1)
    g_s = jnp.where(live, pstart[g_e] + g_k * e_chunk[g_e] * rs, 0).astype(jnp.int32)
    g_n = jnp.where(live, jnp.clip(e_sub[g_e] - e_chunk[g_e] * g_k, 0, e_chunk[g_e]), 0).astype(jnp.int32)

    xs = _dispatch(dest, (pstart + counts).astype(jnp.int32), (padded - counts).astype(jnp.int32),
                   tail, h2, n_sub * rs, tm=min(256, seq), tm_e=rs)
    fdim = w_down.shape[1]
    ys = _experts(g_e, g_s, g_n, tail, xs, w_gate_up, b_gate_up[:, None, :], w_down,
                  b_down[:, None, :], rs=rs, pc=EXPERT_PIECE_SUB, rmax=rmax, tf=min(256, fdim),
                  td=min(512, d))
    out = _combine(dest, ys, packed, x1, g_post_ffn[None, :], gate_f, seq, tm=min(256, seq))
    return out.reshape(batch, seq, d)


def kernel(x, c, positions, w_ada, b_ada, g_pre_mix, g_post_mix, g_pre_ffn, g_post_ffn, w_in,
           q_a_norm, w_q_b, kv_a_norm, w_kv_b, w_mla_o, ret_decay_fwd, ret_decay_bwd, ret_gn,
           w_ret_o, w_out, w_router, b_router, w_gate_up, b_gate_up, w_down, b_down):
    params = (w_ada, b_ada, g_pre_mix, g_post_mix, g_pre_ffn, g_post_ffn, w_in, q_a_norm, w_q_b,
              kv_a_norm, w_kv_b, w_mla_o, ret_decay_fwd, ret_decay_bwd, ret_gn, w_ret_o, w_out,
              w_router, b_router, w_gate_up, b_gate_up, w_down, b_down)
    for l in range(w_ada.shape[0]):
        x = _layer(x, c, positions, *[p[l] for p in params])
    return x
```

```python
import functools

import numpy as np
import jax
import jax.numpy as jnp
from jax import lax
from jax.experimental import pallas as pl
from jax.experimental.pallas import tpu as pltpu

F32 = jnp.float32
BF16 = jnp.bfloat16

MLA_HEADS = 8
MLA_NOPE = 128
MLA_ROPE = 64
MLA_V = 128
MLA_QK = MLA_NOPE + MLA_ROPE
RET_HEADS = 8
N_EXPERTS = 32
TOP_K = 4
SWIGLU_LIMIT = 7.0
SWIGLU_ALPHA = 1.702
ROPE_THETA = 10000.0
RMS_EPS = 1e-6
GN_EPS = 1e-6
LOG2E = 1.4426950408889634
N_MOD = 6

LANES = 128
SUBLANES = 8
VMEM_LIMIT = 56 * 1024 * 1024
EXPERT_SUB_ROWS = 256
EXPERT_PIECE_SUB = 4
GROUP_SUB = 9


def _cparams(*sem):
    return pltpu.CompilerParams(dimension_semantics=sem, vmem_limit_bytes=VMEM_LIMIT)


def _rms(x, gain):
    return x * lax.rsqrt(jnp.mean(x * x, axis=-1, keepdims=True) + RMS_EPS) * gain


def _ada_kernel(c_ref, w_ref, b_ref, o_ref):
    o_ref[...] = jnp.dot(c_ref[...], w_ref[...], preferred_element_type=F32,
                         precision=lax.Precision.HIGHEST) + b_ref[...]


def _ada(c_pad, w_ada, b_ada, tn=1024):
    m, d = c_pad.shape
    n = w_ada.shape[1]
    return pl.pallas_call(
        _ada_kernel,
        out_shape=jax.ShapeDtypeStruct((m, n), F32),
        grid=(n // tn,),
        in_specs=[pl.BlockSpec((m, d), lambda j: (0, 0)),
                  pl.BlockSpec((d, tn), lambda j: (0, j)),
                  pl.BlockSpec((1, tn), lambda j: (0, j))],
        out_specs=pl.BlockSpec((m, tn), lambda j: (0, j)),
        compiler_params=_cparams("arbitrary"),
        name="ada",
    )(c_pad, w_ada, b_ada)


def _inproj_kernel(x_ref, g_ref, sc_ref, sh_ref, w_ref, wkr_ref, o_ref, okr_ref, h_ref):
    @pl.when(pl.program_id(1) == 0)
    def _():
        h = _rms(x_ref[...], g_ref[...]) * (1.0 + sc_ref[...]) + sh_ref[...]
        hb = h.astype(BF16)
        h_ref[...] = hb
        okr_ref[...] = jnp.dot(hb, wkr_ref[...], preferred_element_type=F32)

    o_ref[...] = jnp.dot(h_ref[...], w_ref[...], preferred_element_type=F32).astype(o_ref.dtype)


def _in_proj(x2, g, scale, shift, w_main, w_kr, seq, tm, tn):
    t, d = x2.shape
    n = w_main.shape[1]
    per_b = seq // tm
    return pl.pallas_call(
        _inproj_kernel,
        out_shape=(jax.ShapeDtypeStruct((t, n), BF16), jax.ShapeDtypeStruct((t, LANES), F32)),
        grid=(t // tm, n // tn),
        in_specs=[pl.BlockSpec((tm, d), lambda i, j: (i, 0)),
                  pl.BlockSpec((1, d), lambda i, j: (0, 0)),
                  pl.BlockSpec((None, 1, d), lambda i, j: (i // per_b, 0, 0)),
                  pl.BlockSpec((None, 1, d), lambda i, j: (i // per_b, 0, 0)),
                  pl.BlockSpec((d, tn), lambda i, j: (0, j)),
                  pl.BlockSpec((d, LANES), lambda i, j: (0, 0))],
        out_specs=(pl.BlockSpec((tm, tn), lambda i, j: (i, j)),
                   pl.BlockSpec((tm, LANES), lambda i, j: (i, 0))),
        scratch_shapes=[pltpu.VMEM((tm, d), BF16)],
        compiler_params=_cparams("arbitrary", "arbitrary"),
        name="in_proj",
    )(x2, g, scale, shift, w_main, w_kr)


def _rope_kernel(pos_ref, invr_ref, invm_ref, cr_ref, sr_ref, cm_ref, sm_ref):
    pos = pos_ref[...].astype(F32)
    ang_r = pos * invr_ref[...]
    cr_ref[...] = jnp.cos(ang_r)
    sr_ref[...] = jnp.sin(ang_r)
    ang_m = pos * invm_ref[...]
    lane = lax.broadcasted_iota(jnp.int32, ang_m.shape, 1)
    half = MLA_ROPE // 2
    cm_ref[...] = jnp.where(lane < MLA_ROPE, jnp.cos(ang_m), 0.0)
    s = jnp.sin(ang_m)
    sm_ref[...] = jnp.where(lane < half, -s, jnp.where(lane < MLA_ROPE, s, 0.0))


def _rope_tables(pos_col, tm):
    t = pos_col.shape[0]
    half_r = LANES
    inv_r = (1.0 / (np.float32(ROPE_THETA) ** (np.arange(half_r, dtype=np.float32) / np.float32(half_r)))
             ).astype(np.float32)
    half_m = MLA_ROPE // 2
    inv_m32 = (1.0 / (np.float32(ROPE_THETA) ** (np.arange(half_m, dtype=np.float32) / np.float32(half_m)))
               ).astype(np.float32)
    inv_m = np.zeros((LANES,), np.float32)
    inv_m[:half_m] = inv_m32
    inv_m[half_m:2 * half_m] = inv_m32
    tab = jax.ShapeDtypeStruct((t, LANES), F32)
    row = pl.BlockSpec((tm, LANES), lambda i: (i, 0))
    return pl.pallas_call(
        _rope_kernel,
        out_shape=(tab, tab, tab, tab),
        grid=(t // tm,),
        in_specs=[pl.BlockSpec((tm, 1), lambda i: (i, 0)),
                  pl.BlockSpec((1, LANES), lambda i: (0, 0)),
                  pl.BlockSpec((1, LANES), lambda i: (0, 0))],
        out_specs=(row, row, row, row),
        compiler_params=_cparams("arbitrary"),
        name="rope_tables",
    )(pos_col, jnp.asarray(inv_r)[None, :], jnp.asarray(inv_m)[None, :])


def _rot64(x, c, s):
    return x * c + (pltpu.roll(x, 32, 1) + pltpu.roll(x, 96, 1)) * s


def _mla_prep_kernel(ql_ref, ckv_ref, kr_ref, cm_ref, sm_ref, gq_ref, gkv_ref,
                     wqn_ref, wqr_ref, wk_ref, wv_ref, q_ref, k_ref, v_ref):
    c = cm_ref[...]
    s = sm_ref[...]
    qn = _rms(ql_ref[...].astype(F32), gq_ref[...]).astype(BF16)
    q_nope = jnp.dot(qn, wqn_ref[...], preferred_element_type=F32)
    q_rope = jnp.dot(qn, wqr_ref[...], preferred_element_type=F32)
    cn = _rms(ckv_ref[...].astype(F32), gkv_ref[...]).astype(BF16)
    k_nope = jnp.dot(cn, wk_ref[...], preferred_element_type=F32)
    v = jnp.dot(cn, wv_ref[...], preferred_element_type=F32)
    k_rot_t = _rot64(kr_ref[...], c, s).T.astype(BF16)
    lane = lax.broadcasted_iota(jnp.int32, c.shape, 1)
    ones_col = jnp.where(lane == 0, 1.0, 0.0).astype(BF16)
    scale = MLA_QK ** -0.5 * LOG2E
    for h in range(MLA_HEADS):
        lo, hi = h * LANES, (h + 1) * LANES
        q_ref[:, 2 * lo:2 * lo + LANES] = (q_nope[:, lo:hi] * scale).astype(BF16)
        q_ref[:, 2 * lo + LANES:2 * hi] = (_rot64(q_rope[:, lo:hi], c, s) * scale).astype(BF16)
        k_ref[2 * lo:2 * lo + LANES, :] = k_nope[:, lo:hi].T.astype(BF16)
        k_ref[2 * lo + LANES:2 * hi, :] = k_rot_t
        v_ref[:, 2 * lo:2 * lo + LANES] = v[:, lo:hi].astype(BF16)
        v_ref[:, 2 * lo + LANES:2 * hi] = ones_col


def _mla_prep(proj, kr_raw, cm, sm, gq, gkv, wqn, wqr, wk, wv, ql_blk, ckv_blk, seq, tm):
    t = proj.shape[0]
    rank = wqn.shape[0]
    hd = MLA_HEADS * LANES
    per_b = seq // tm
    full = lambda shape: pl.BlockSpec(shape, lambda i: (0, 0))
    row = lambda w: pl.BlockSpec((tm, w), lambda i: (i, 0))
    return pl.pallas_call(
        _mla_prep_kernel,
        out_shape=(jax.ShapeDtypeStruct((t, 2 * hd), BF16),
                   jax.ShapeDtypeStruct(((t // seq) * 2 * hd, seq), BF16),
                   jax.ShapeDtypeStruct((t, 2 * hd), BF16)),
        grid=(t // tm,),
        in_specs=[pl.BlockSpec((tm, rank), lambda i: (i, ql_blk)),
                  pl.BlockSpec((tm, rank), lambda i: (i, ckv_blk)),
                  row(LANES), row(LANES), row(LANES),
                  full((1, rank)), full((1, rank)),
                  full((rank, hd)), full((rank, hd)), full((rank, hd)), full((rank, hd))],
        out_specs=(row(2 * hd),
                   pl.BlockSpec((2 * hd, tm), lambda i: (i // per_b, i % per_b)),
                   row(2 * hd)),
        compiler_params=_cparams("arbitrary"),
        name="mla_prep",
    )(proj, proj, kr_raw, cm, sm, gq, gkv, wqn, wqr, wk, wv)


def _attn_kernel(q_ref, kt_ref, v_ref, o_ref, s_ref, *, rq, tk):
    tq = q_ref.shape[0]
    seq = kt_ref.shape[1]
    for r0 in range(0, tq, rq):
        q = q_ref[r0:r0 + rq, :]
        m_lane = None
        for c0 in range(0, seq, tk):
            s = jnp.dot(q, kt_ref[:, c0:c0 + tk], preferred_element_type=F32)
            s_ref[r0:r0 + rq, c0:c0 + tk] = s
            for l0 in range(0, tk, LANES):
                blk = s[:, l0:l0 + LANES]
                m_lane = blk if m_lane is None else jnp.maximum(m_lane, blk)
        m = jnp.max(m_lane, axis=-1, keepdims=True)
        acc = jnp.zeros((rq, 2 * MLA_V), F32)
        for c0 in range(0, seq, tk):
            p = jnp.exp2(s_ref[r0:r0 + rq, c0:c0 + tk] - m).astype(BF16)
            acc = acc + jnp.dot(p, v_ref[c0:c0 + tk, :], preferred_element_type=F32)
        o_ref[r0:r0 + rq, :] = (acc[:, :MLA_V] / acc[:, MLA_V:MLA_V + 1]).astype(o_ref.dtype)


def _attention(q, kt, v, batch, seq, tq, rq, tk):
    t = q.shape[0]
    nq = seq // tq
    return pl.pallas_call(
        functools.partial(_attn_kernel, rq=rq, tk=tk),
        out_shape=jax.ShapeDtypeStruct((t, MLA_HEADS * MLA_V), BF16),
        grid=(batch, MLA_HEADS, nq),
        in_specs=[pl.BlockSpec((tq, 2 * LANES), lambda b, h, i: (b * nq + i, h)),
                  pl.BlockSpec((2 * LANES, seq), lambda b, h, i: (b * MLA_HEADS + h, 0)),
                  pl.BlockSpec((seq, 2 * MLA_V), lambda b, h, i: (b, h))],
        out_specs=pl.BlockSpec((tq, MLA_V), lambda b, h, i: (b * nq + i, h)),
        scratch_shapes=[pltpu.VMEM((tq, seq), F32)],
        compiler_params=_cparams("arbitrary", "arbitrary", "arbitrary"),
        name="mla_attention",
    )(q, kt, v)


def _ret_kernel(rq_ref, rk_ref, rv_ref, rg_ref, cos_ref, sin_ref, df_ref, db_ref, gn_ref,
                o_ref, krot_ref, sf_ref, st_ref, dmat_ref, xif_ref, zf_ref, xib_ref, zb_ref, *, ch):
    seq, dk = rq_ref.shape
    half = dk // 2
    n_ch = seq // ch
    lgf = jnp.log(jax.nn.sigmoid(df_ref[...]))[:1, :1]
    lgb = jnp.log(jax.nn.sigmoid(db_ref[...]))[:1, :1]

    ri = lax.broadcasted_iota(jnp.int32, (ch, ch), 0)
    ci = lax.broadcasted_iota(jnp.int32, (ch, ch), 1)
    diff = (ri - ci).astype(F32)
    dmat_ref[...] = jnp.where(diff >= 0, jnp.exp(lgf * jnp.maximum(diff, 0.0)),
                              jnp.exp(lgb * jnp.maximum(-diff, 0.0)))
    pos = lax.broadcasted_iota(jnp.int32, (ch, dk), 0).astype(F32)
    xif_ref[...] = jnp.exp(lgf * (pos + 1.0))
    zf_ref[...] = jnp.exp(lgf * (ch - 1.0 - pos))
    xib_ref[...] = jnp.exp(lgb * (ch - pos))
    zb_ref[...] = jnp.exp(lgb * pos)
    gcf = jnp.exp(lgf * float(ch))
    gcb = jnp.exp(lgb * float(ch))
    k_scale = dk ** -0.5

    def rot(x, c, s):
        x1, x2 = x[:, :half], x[:, half:]
        return jnp.concatenate([x1 * c - x2 * s, x1 * s + x2 * c], axis=-1)

    def kv_outer(kz, v):
        return lax.dot_general(kz, v, (((0,), (0,)), ((), ())), preferred_element_type=F32)

    def chunk_pairs(body):
        step = 4 if n_ch % 4 == 0 else (2 if n_ch % 2 == 0 else 1)

        def it(j, carry):
            for u in range(step):
                body(j * step + u, carry)
            return carry

        lax.fori_loop(0, n_ch // step, it, 0)

    st_ref[...] = jnp.zeros(st_ref.shape, F32)

    def fwd(n, carry):
        r0 = pl.multiple_of(n * ch, ch)
        rows = pl.ds(r0, ch)
        k = rot(rk_ref[rows, :].astype(F32), cos_ref[rows, :], sin_ref[rows, :]) * k_scale
        krot_ref[rows, :] = k.astype(BF16)
        sf_ref[n] = st_ref[...].astype(BF16)
        kv = kv_outer((k * zf_ref[...]).astype(BF16), rv_ref[rows, :])
        st_ref[...] = st_ref[...] * gcf + kv
        return carry

    chunk_pairs(fwd)

    st_ref[...] = jnp.zeros(st_ref.shape, F32)

    def bwd(i, carry):
        n = n_ch - 1 - i
        r0 = pl.multiple_of(n * ch, ch)
        rows = pl.ds(r0, ch)
        q = rot(rq_ref[rows, :].astype(F32), cos_ref[rows, :], sin_ref[rows, :])
        kb = krot_ref[rows, :]
        v = rv_ref[rows, :]
        s = lax.dot_general(q.astype(BF16), kb, (((1,), (1,)), ((), ())), preferred_element_type=F32)
        o = jnp.dot((s * dmat_ref[...]).astype(BF16), v, preferred_element_type=F32)
        o = o + jnp.dot((q * xif_ref[...]).astype(BF16), sf_ref[n], preferred_element_type=F32)
        o = o + jnp.dot((q * xib_ref[...]).astype(BF16), st_ref[...].astype(BF16),
                        preferred_element_type=F32)
        mu = jnp.mean(o, axis=-1, keepdims=True)
        oc = o - mu
        var = jnp.mean(oc * oc, axis=-1, keepdims=True)
        on = oc * lax.rsqrt(var + GN_EPS)
        g = rg_ref[rows, :].astype(F32)
        o_ref[rows, :] = (g * jax.nn.sigmoid(g) * (on * gn_ref[...])).astype(o_ref.dtype)
        kv = kv_outer((kb.astype(F32) * zb_ref[...]).astype(BF16), v)
        st_ref[...] = st_ref[...] * gcb + kv
        return carry

    chunk_pairs(bwd)


def _retention(proj, cos_r, sin_r, dec_f, dec_b, gn, batch, seq, blk, ch):
    t = proj.shape[0]
    dk = 2 * LANES
    n_ch = seq // ch
    col = lambda base: pl.BlockSpec((seq, dk), lambda b, h: (b, base + h))
    tab = pl.BlockSpec((seq, LANES), lambda b, h: (b, 0))
    dec = pl.BlockSpec((None, 8, LANES), lambda b, h: (h, 0, 0))
    return pl.pallas_call(
        functools.partial(_ret_kernel, ch=ch),
        out_shape=jax.ShapeDtypeStruct((t, RET_HEADS * dk), BF16),
        grid=(batch, RET_HEADS),
        in_specs=[col(blk["rq"]), col(blk["rk"]), col(blk["rv"]), col(blk["rg"]),
                  tab, tab, dec, dec,
                  pl.BlockSpec((1, dk), lambda b, h: (0, h))],
        out_specs=pl.BlockSpec((seq, dk), lambda b, h: (b, h)),
        scratch_shapes=[pltpu.VMEM((seq, dk), BF16),
                        pltpu.VMEM((n_ch, dk, dk), BF16),
                        pltpu.VMEM((dk, dk), F32),
                        pltpu.VMEM((ch, ch), F32),
                        pltpu.VMEM((ch, dk), F32), pltpu.VMEM((ch, dk), F32),
                        pltpu.VMEM((ch, dk), F32), pltpu.VMEM((ch, dk), F32)],
        compiler_params=_cparams("arbitrary", "arbitrary"),
        name="retention",
    )(proj, proj, proj, proj, cos_r, sin_r, dec_f, dec_b, gn)


def _mixgate_kernel(oa_ref, ob_ref, ga_ref, gb_ref, wa_ref, wb_ref, m_ref):
    ya = jnp.dot(oa_ref[...], wa_ref[...], preferred_element_type=F32)
    yb = jnp.dot(ob_ref[...], wb_ref[...], preferred_element_type=F32)
    m = jax.nn.sigmoid(ga_ref[...].astype(F32)) * ya + jax.nn.sigmoid(gb_ref[...].astype(F32)) * yb
    m_ref[...] = m.astype(m_ref.dtype)


def _mix_gate(o_mla, o_ret, proj, wa, wb, ga_blk, gb_blk, tm):
    t = o_mla.shape[0]
    d = wa.shape[1]
    return pl.pallas_call(
        _mixgate_kernel,
        out_shape=jax.ShapeDtypeStruct((t, d), BF16),
        grid=(t // tm,),
        in_specs=[pl.BlockSpec((tm, o_mla.shape[1]), lambda i: (i, 0)),
                  pl.BlockSpec((tm, o_ret.shape[1]), lambda i: (i, 0)),
                  pl.BlockSpec((tm, d), lambda i: (i, ga_blk)),
                  pl.BlockSpec((tm, d), lambda i: (i, gb_blk)),
                  pl.BlockSpec(wa.shape, lambda i: (0, 0)),
                  pl.BlockSpec(wb.shape, lambda i: (0, 0))],
        out_specs=pl.BlockSpec((tm, d), lambda i: (i, 0)),
        compiler_params=_cparams("arbitrary"),
        name="mix_gate",
    )(o_mla, o_ret, proj, proj, wa, wb)


def _mixout_kernel(m_ref, x_ref, wo_ref, gpost_ref, gate_ref, gpre_ref, sc_ref, sh_ref,
                   wr_ref, br_ref, x1_ref, h2_ref, pk_ref, cnt_ref, carry_ref):
    mix = jnp.dot(m_ref[...], wo_ref[...], preferred_element_type=F32)
    x1 = x_ref[...] + gate_ref[...] * _rms(mix, gpost_ref[...])
    x1_ref[...] = x1
    h2 = _rms(x1, gpre_ref[...]) * (1.0 + sc_ref[...]) + sh_ref[...]
    half = h2.shape[1] // 2
    bits = pltpu.bitcast(h2.astype(BF16).astype(F32), jnp.uint32)
    h2_ref[...] = (bits[:, :half] >> 16) | (bits[:, half:] & jnp.uint32(0xFFFF0000))
    h_hi = h2.astype(BF16)
    h_lo = (h2 - h_hi.astype(F32)).astype(BF16)
    wr = wr_ref[...]
    hh = jnp.dot(h_hi, wr, preferred_element_type=F32)
    lh = jnp.dot(h_lo, wr[:, :LANES], preferred_element_type=F32)
    logits = hh[:, :LANES] + (hh[:, LANES:] + lh) + br_ref[...]

    @pl.when(pl.program_id(0) == 0)
    def _():
        carry_ref[...] = jnp.zeros(carry_ref.shape, F32)

    pk_ref[...] = _route_tile(logits, carry_ref)
    cnt_ref[...] = carry_ref[...]


def _mix_out(m, x2, wo, gpost, gate_m, gpre, scale_f, shift_f, wr, br, seq, tm):
    t, d = x2.shape
    per_b = seq // tm
    row = pl.BlockSpec((tm, d), lambda i: (i, 0))
    vec = pl.BlockSpec((1, d), lambda i: (0, 0))
    mod = pl.BlockSpec((None, 1, d), lambda i: (i // per_b, 0, 0))
    return pl.pallas_call(
        _mixout_kernel,
        out_shape=(jax.ShapeDtypeStruct((t, d), F32), jax.ShapeDtypeStruct((t, d // 2), jnp.uint32),
                   jax.ShapeDtypeStruct((t, LANES), F32), jax.ShapeDtypeStruct((8, LANES), F32)),
        grid=(t // tm,),
        in_specs=[row, row, pl.BlockSpec((d, d), lambda i: (0, 0)), vec, mod, vec, mod, mod,
                  pl.BlockSpec((d, 2 * LANES), lambda i: (0, 0)),
                  pl.BlockSpec((1, LANES), lambda i: (0, 0))],
        out_specs=(row, pl.BlockSpec((tm, d // 2), lambda i: (i, 0)),
                   pl.BlockSpec((tm, LANES), lambda i: (i, 0)),
                   pl.BlockSpec((8, LANES), lambda i: (0, 0))),
        scratch_shapes=[pltpu.VMEM((8, LANES), F32)],
        compiler_params=_cparams("arbitrary"),
        name="mix_out",
    )(m, x2, wo, gpost, gate_m, gpre, scale_f, shift_f, wr, br)


def _route_tile(logits, carry_ref):
    tm = logits.shape[0]
    lane = lax.broadcasted_iota(jnp.int32, (tm, LANES), 1).astype(F32)
    l = jnp.where(lane < N_EXPERTS, logits, -jnp.inf)
    vals, idxs = [], []
    for _ in range(TOP_K):
        mx = jnp.max(l, axis=-1, keepdims=True)
        ik = jnp.min(jnp.where(l == mx, lane, float(LANES)), axis=-1, keepdims=True)
        vals.append(mx)
        idxs.append(ik)
        l = jnp.where(lane == ik, -jnp.inf, l)
    es = [jnp.exp(v - vals[0]) for v in vals]
    den = es[0] + es[1] + es[2] + es[3]
    onehot = jnp.zeros((tm, LANES), F32)
    for ik in idxs:
        onehot = onehot + jnp.where(lane == ik, 1.0, 0.0)
    ri = lax.broadcasted_iota(jnp.int32, (tm, tm), 0)
    ci = lax.broadcasted_iota(jnp.int32, (tm, tm), 1)
    tri = jnp.where(ci < ri, 1.0, 0.0).astype(BF16)
    cum = jnp.dot(tri, onehot.astype(BF16), preferred_element_type=F32) + carry_ref[0:1, :]
    packed = jnp.zeros((tm, LANES), F32)
    for k in range(TOP_K):
        rank = jnp.sum(jnp.where(lane == idxs[k], cum, 0.0), axis=-1, keepdims=True)
        packed = packed + jnp.where(lane == float(k), idxs[k], 0.0)
        packed = packed + jnp.where(lane == float(TOP_K + k), rank, 0.0)
        packed = packed + jnp.where(lane == float(2 * TOP_K + k), es[k] / den, 0.0)
    carry_ref[...] = carry_ref[...] + jnp.sum(onehot, axis=0, keepdims=True)
    return packed


def _dispatch_kernel(dest_ref, pad_pos_ref, pad_len_ref, tail_ref, h_ref, xs_ref, z_ref, sem, zsem,
                     *, tm_e):
    tm = h_ref.shape[0]
    zr = z_ref.shape[0]
    base = pl.program_id(0) * (tm * TOP_K)
    pieces = [p for p in (2 ** i for i in range(20)) if SUBLANES <= p <= zr and p < tm_e][::-1]

    def zero_copy(pos, p):
        return pltpu.make_async_copy(z_ref.at[pl.ds(0, p), :], xs_ref.at[pl.ds(pos, p), :], zsem)

    def pad_rows(act):
        for e in range(N_EXPERTS):
            pos = pad_pos_ref[e]
            head = (-pos) & (SUBLANES - 1)
            for j in range(SUBLANES - 1):
                @pl.when(j < head)
                def _(pos=pos, j=j):
                    act(zero_copy(pos + j, 1))

            pos = pos + head
            ln = pad_len_ref[e] - head
            for p in pieces:
                hit = (ln & p) != 0

                @pl.when(hit)
                def _(pos=pos, p=p):
                    act(zero_copy(pl.multiple_of(pos, SUBLANES), p))

                pos = pos + jnp.where(hit, p, 0)

    def tail_rows(act):
        def body(j, carry):
            for r0 in range(0, tm_e, zr):
                act(zero_copy(pl.multiple_of(j * tm_e + r0, SUBLANES), zr))
            return carry

        lax.fori_loop(tail_ref[0], tail_ref[1], body, 0)

    @pl.when(pl.program_id(0) == 0)
    def _():
        z_ref[...] = jnp.zeros(z_ref.shape, z_ref.dtype)
        pad_rows(lambda c: c.start())
        tail_rows(lambda c: c.start())

    def row_copy(t, d):
        return pltpu.make_async_copy(h_ref.at[pl.ds(t, 1), :], xs_ref.at[pl.ds(d, 1), :], sem)

    def issue(t, carry):
        for k in range(TOP_K):
            row_copy(t, dest_ref[base + t * TOP_K + k]).start(priority=k % 2)
        return carry

    lax.fori_loop(0, tm, issue, 0, unroll=8)

    for k in range(TOP_K):
        pltpu.make_async_copy(h_ref, xs_ref.at[pl.ds(0, tm), :], sem).wait()

    @pl.when(pl.program_id(0) == 0)
    def _():
        pad_rows(lambda c: c.wait())
        tail_rows(lambda c: c.wait())


def _dispatch(dest, pad_pos, pad_len, tail, h2, n_rows, tm, tm_e):
    t, d = h2.shape
    zr = min(256, tm_e)
    nsp = 4
    return pl.pallas_call(
        functools.partial(_dispatch_kernel, tm_e=tm_e),
        out_shape=jax.ShapeDtypeStruct((n_rows, d), h2.dtype),
        grid_spec=pltpu.PrefetchScalarGridSpec(
            num_scalar_prefetch=nsp,
            grid=(t // tm,),
            in_specs=[pl.BlockSpec((tm, d), lambda i, *_: (i, 0))],
            out_specs=pl.BlockSpec(memory_space=pl.ANY),
            scratch_shapes=[pltpu.VMEM((zr, d), h2.dtype), pltpu.SemaphoreType.DMA,
                            pltpu.SemaphoreType.DMA],
        ),
        compiler_params=_cparams("arbitrary"),
        name="dispatch",
    )(dest, pad_pos, pad_len, tail, h2)


def _expert_kernel(ge_ref, gs_ref, gn_ref, tail_ref, xs_ref, wg_ref, wu_ref, bg_ref, bu_ref,
                   wd_ref, bd_ref, ys_ref, xu_ref, xb_ref, ab_ref,
                   ob_ref, orem_ref, z_ref, flag_ref, xsem, osem, rsem, zsem, *, rs, pc, nf, nd):
    del ge_ref
    g = pl.program_id(0)
    s = pl.program_id(1)
    n = gn_ref[g]
    row0 = gs_ref[g]
    half = xu_ref.shape[1]
    tf = wg_ref.shape[1]
    td = wd_ref.shape[1]

    def rows(i):
        return pl.ds(pl.multiple_of(i * rs, rs), rs)

    def x_copy(base, i):
        src = xs_ref.at[pl.ds(pl.multiple_of(base + i * rs, rs), rs), :]
        return pltpu.make_async_copy(src, xu_ref.at[rows(i), :], xsem)

    def x_start(base, cnt):
        def start(i, c):
            x_copy(base, i).start()
            return c

        lax.fori_loop(0, cnt, start, 0)

    g_next = jnp.minimum(g + 1, pl.num_programs(0) - 1)
    n_next = jnp.where(g + 1 < pl.num_programs(0), gn_ref[g_next], 0)

    @pl.when(jnp.logical_and(n > 0, s == 1))
    def _():
        x_start(gs_ref[g_next], n_next)

    @pl.when(jnp.logical_and(n > 0, s == 0))
    def _():
        @pl.when(g == 0)
        def _():
            x_start(row0, n)

        def wait(i, c):
            x_copy(row0, i).wait()
            return c

        lax.fori_loop(0, n, wait, 0)

        def unpack(i, c):
            u = xu_ref[rows(i), :]
            lo = pltpu.bitcast(u << 16, F32)
            hi = pltpu.bitcast(u & jnp.uint32(0xFFFF0000), F32)
            xb_ref[rows(i), :half] = lo.astype(BF16)
            xb_ref[rows(i), half:] = hi.astype(BF16)
            return c

        lax.fori_loop(0, n, unpack, 0)

    small = [pc >> j for j in range(1, pc.bit_length())]
    out_bufs = [(ob_ref.at[0], osem.at[0], pc), (ob_ref.at[1], osem.at[1], pc)]
    off = 0
    for j, sz in enumerate(small):
        out_bufs.append((orem_ref.at[pl.ds(off, sz * rs), :], rsem.at[j], sz))
        off += sz * rs
    n_bufs = len(out_bufs)

    def out_wait(k):
        buf, sem, sz = out_bufs[k]
        pltpu.make_async_copy(buf, ys_ref.at[pl.ds(0, sz * rs), 0:td], sem).wait()

    def drain(k):
        @pl.when(flag_ref[k] == 1)
        def _():
            out_wait(k)
            flag_ref[k] = 0

    @pl.when(jnp.logical_and(g == 0, s == 0))
    def _():
        for k in range(n_bufs):
            flag_ref[k] = 0

    n_big = lax.shift_right_logical(n, (2 * pc).bit_length() - 1)
    leftovers = []
    r_next = n_big * (2 * pc * rs)
    for k, sz in [(0, pc)] + [(2 + j, sz) for j, sz in enumerate(small)]:
        present = (n & sz) != 0
        leftovers.append((present, r_next, sz, k))
        r_next = r_next + jnp.where(present, sz * rs, 0)

    def row_slice(r0, m):
        return pl.ds(pl.multiple_of(r0, rs), m)

    @pl.when(jnp.logical_and(n > 0, s < nf))
    def _():
        def act_rows(r0, m, wgb, wub):
            x = xb_ref[row_slice(r0, m), :]
            gt = jnp.dot(x, wgb, preferred_element_type=F32) + bg_ref[...]
            ut = jnp.dot(x, wub, preferred_element_type=F32) + bu_ref[...]
            gt = jnp.minimum(gt, SWIGLU_LIMIT)
            ut = jnp.clip(ut, -SWIGLU_LIMIT, SWIGLU_LIMIT)
            act = (ut + 1.0) * (gt * jax.nn.sigmoid(SWIGLU_ALPHA * gt))
            ab_ref[s, row_slice(r0, m), :] = act.astype(BF16)

        def chunks(r0, sizes):
            wgb = wg_ref[...].astype(BF16)
            wub = wu_ref[...].astype(BF16)
            for m in sizes:
                act_rows(r0, m, wgb, wub)
                r0 = r0 + m

        def big_body(i, c):
            chunks(i * (2 * pc * rs), (pc * rs, pc * rs))
            return c

        lax.fori_loop(0, n_big, big_body, 0)

        for present, r0, sz, _ in leftovers:
            @pl.when(present)
            def _(r0=r0, sz=sz):
                chunks(r0, (sz * rs,))

    @pl.when(jnp.logical_and(n > 0, s >= nf))
    def _():
        d = s - nf

        def out_copy(r0, m, buf, sem, j):
            dst = ys_ref.at[pl.ds(pl.multiple_of(row0 + r0, rs), m), j * td:(j + 1) * td]
            return pltpu.make_async_copy(buf, dst, sem)

        def emit(r0, k, wdb):
            buf, sem, sz = out_bufs[k]
            m = sz * rs
            drain(k)
            hm = m // 2 if sz >= 2 else m
            for h0 in range(0, m, hm):
                a_full = jnp.concatenate([ab_ref[f, row_slice(r0 + h0, hm), :] for f in range(nf)],
                                         axis=1)
                buf[h0:h0 + hm, :] = jnp.dot(a_full, wdb, preferred_element_type=F32) + bd_ref[...]
            for j in range(nd):
                @pl.when(d == j)
                def _(j=j):
                    out_copy(r0, m, buf, sem, j).start()

            flag_ref[k] = 1

        def w_down():
            return wd_ref[...].astype(BF16)

        for present, r0, sz, k in reversed(leftovers):
            @pl.when(present)
            def _(r0=r0, k=k):
                emit(r0, k, w_down())

        def big_body(i, c):
            wdb = w_down()
            emit(i * (2 * pc * rs), 0, wdb)
            emit(i * (2 * pc * rs) + pc * rs, 1, wdb)
            return c

        lax.fori_loop(0, n_big, big_body, 0)

    @pl.when(jnp.logical_and(g == pl.num_programs(0) - 1, s == nf + nd - 1))
    def _():
        for k in range(n_bufs):
            drain(k)
        z_ref[...] = jnp.zeros(z_ref.shape, z_ref.dtype)

        def z_copy(i):
            dst = ys_ref.at[pl.ds(pl.multiple_of(i * rs, rs), rs), :]
            return pltpu.make_async_copy(z_ref, dst, zsem)

        def start(i, c):
            z_copy(i).start()
            return c

        def wait(i, c):
            z_copy(i).wait()
            return c

        lax.fori_loop(tail_ref[0], tail_ref[1], start, 0)
        lax.fori_loop(tail_ref[0], tail_ref[1], wait, 0)


def _experts(ge, gs, gn, tail, xs, w_gu, b_gu, w_dn, b_dn, rs, pc, rmax, tf, td):
    p = xs.shape[0]
    n_e, d, f2 = w_gu.shape
    fdim = f2 // 2
    nf = fdim // tf
    nd = d // td
    n_groups = ge.shape[0]

    def f_idx(g, s, gn):
        return jnp.where(gn[g] > 0, jnp.minimum(s, nf - 1), nf - 1)

    def d_idx(g, s, gn):
        return jnp.where(gn[g] > 0, jnp.maximum(s - nf, 0), nd - 1)

    return pl.pallas_call(
        functools.partial(_expert_kernel, rs=rs, pc=pc, nf=nf, nd=nd),
        out_shape=jax.ShapeDtypeStruct((p, d), F32),
        grid_spec=pltpu.PrefetchScalarGridSpec(
            num_scalar_prefetch=4,
            grid=(n_groups, nf + nd),
            in_specs=[pl.BlockSpec(memory_space=pl.ANY),
                      pl.BlockSpec((None, d, tf), lambda g, s, ge, gs, gn, tl: (ge[g], 0, f_idx(g, s, gn))),
                      pl.BlockSpec((None, d, tf),
                                   lambda g, s, ge, gs, gn, tl: (ge[g], 0, nf + f_idx(g, s, gn))),
                      pl.BlockSpec((None, 1, tf), lambda g, s, ge, gs, gn, tl: (ge[g], 0, f_idx(g, s, gn))),
                      pl.BlockSpec((None, 1, tf),
                                   lambda g, s, ge, gs, gn, tl: (ge[g], 0, nf + f_idx(g, s, gn))),
                      pl.BlockSpec((None, fdim, td), lambda g, s, ge, gs, gn, tl: (ge[g], 0, d_idx(g, s, gn))),
                      pl.BlockSpec((None, 1, td), lambda g, s, ge, gs, gn, tl: (ge[g], 0, d_idx(g, s, gn)))],
            out_specs=pl.BlockSpec(memory_space=pl.ANY),
            scratch_shapes=[pltpu.VMEM((rmax, d // 2), jnp.uint32),
                            pltpu.VMEM((rmax, d), BF16),
                            pltpu.VMEM((nf, rmax, tf), BF16),
                            pltpu.VMEM((2, pc * rs, td), F32),
                            pltpu.VMEM((pc * rs, td), F32),
                            pltpu.VMEM((rs, d), F32),
                            pltpu.SMEM((2 + pc.bit_length(),), jnp.int32),
                            pltpu.SemaphoreType.DMA, pltpu.SemaphoreType.DMA((2,)),
                            pltpu.SemaphoreType.DMA((pc.bit_length(),)),
                            pltpu.SemaphoreType.DMA],
        ),
        compiler_params=_cparams("arbitrary", "arbitrary"),
        name="experts",
    )(ge, gs, gn, tail, xs, w_gu, w_gu, b_gu, b_gu, w_dn, b_dn)


def _combine_kernel(dest_ref, ys_ref, pk_ref, x1_ref, gpost_ref, gate_ref, o_ref, buf_ref, sem):
    tm = x1_ref.shape[0]
    base = pl.program_id(0) * (tm * TOP_K)

    def row_copy(t, k, d):
        return pltpu.make_async_copy(ys_ref.at[pl.ds(d, 1), :], buf_ref.at[k, pl.ds(t, 1), :], sem)

    def issue(t, carry):
        for k in range(TOP_K):
            row_copy(t, k, dest_ref[base + t * TOP_K + k]).start(priority=k % 2)
        return carry

    lax.fori_loop(0, tm, issue, 0, unroll=8)

    for k in range(TOP_K):
        pltpu.make_async_copy(ys_ref.at[pl.ds(0, tm), :], buf_ref.at[k], sem).wait()

    pk = pk_ref[...]
    f = jnp.zeros(x1_ref.shape, F32)
    for k in range(TOP_K):
        w = pk[:, 2 * TOP_K + k:2 * TOP_K + k + 1]
        f = f + buf_ref[k] * w
    o_ref[...] = x1_ref[...] + gate_ref[...] * _rms(f, gpost_ref[...])


def _combine(dest, ys, packed, x1, gpost, gate_f, seq, tm):
    t, d = x1.shape
    per_b = seq // tm
    return pl.pallas_call(
        _combine_kernel,
        out_shape=jax.ShapeDtypeStruct((t, d), F32),
        grid_spec=pltpu.PrefetchScalarGridSpec(
            num_scalar_prefetch=1,
            grid=(t // tm,),
            in_specs=[pl.BlockSpec(memory_space=pl.ANY),
                      pl.BlockSpec((tm, LANES), lambda i, dest: (i, 0)),
                      pl.BlockSpec((tm, d), lambda i, dest: (i, 0)),
                      pl.BlockSpec((1, d), lambda i, dest: (0, 0)),
                      pl.BlockSpec((None, 1, d), lambda i, dest: (i // per_b, 0, 0))],
            out_specs=pl.BlockSpec((tm, d), lambda i, dest: (i, 0)),
            scratch_shapes=[pltpu.VMEM((TOP_K, tm, d), F32), pltpu.SemaphoreType.DMA],
        ),
        compiler_params=_cparams("arbitrary"),
        name="combine",
    )(dest, ys, packed, x1, gpost, gate_f)


def _pad_heads(w, n_heads, width):
    k = w.shape[0]
    w = w.reshape(k, n_heads, width)
    return jnp.pad(w, ((0, 0), (0, 0), (0, LANES - width))).reshape(k, n_heads * LANES)


def _layer(x, c, positions, w_ada, b_ada, g_pre_mix, g_post_mix, g_pre_ffn, g_post_ffn,
           w_in, q_a_norm, w_q_b, kv_a_norm, w_kv_b, w_mla_o, ret_decay_fwd, ret_decay_bwd,
           ret_gn, w_ret_o, w_out, w_router, b_router, w_gate_up, b_gate_up, w_down, b_down):
    batch, seq, d = x.shape
    t = batch * seq
    q_rank = q_a_norm.shape[0]
    kv_rank = kv_a_norm.shape[0]
    hd = RET_HEADS * 2 * LANES
    x2 = x.reshape(t, d)

    c_pad = jnp.pad(c, ((0, 8 - batch), (0, 0)))
    ada = _ada(c_pad, w_ada, b_ada[None, :])[:batch]
    shift_m, scale_m, gate_m, shift_f, scale_f, gate_f = [
        ada[:, i * d:(i + 1) * d].reshape(batch, 1, d) for i in range(N_MOD)]

    o_q, o_kv = 0, q_rank
    o_r = q_rank + kv_rank + MLA_ROPE
    w_wide = w_in[:, o_r:]
    w_main = jnp.concatenate([w_wide, w_in[:, o_q:o_q + q_rank], w_in[:, o_kv:o_kv + kv_rank]],
                             axis=1).astype(BF16)
    w_kr = jnp.pad(w_in[:, o_kv + kv_rank:o_r], ((0, 0), (0, LANES - MLA_ROPE))).astype(BF16)
    n_wide = w_wide.shape[1]
    proj, kr_raw = _in_proj(x2, g_pre_mix[None, :], scale_m, shift_m, w_main, w_kr, seq,
                            tm=min(1024, seq), tn=1024)
    blk_w = 2 * LANES
    blk = {"rq": 0, "rk": hd // blk_w, "rv": 2 * hd // blk_w, "rg": 3 * hd // blk_w}
    ga_blk, gb_blk = (4 * hd) // d, (4 * hd + d) // d
    ql_blk, ckv_blk = n_wide // q_rank, (n_wide + q_rank) // kv_rank

    cos_r, sin_r, cm, sm = _rope_tables(positions.reshape(t, 1), tm=min(512, seq))

    wq = w_q_b.reshape(q_rank, MLA_HEADS, MLA_QK)
    wqn = wq[:, :, :MLA_NOPE].reshape(q_rank, MLA_HEADS * MLA_NOPE).astype(BF16)
    wqr = _pad_heads(wq[:, :, MLA_NOPE:].reshape(q_rank, MLA_HEADS * MLA_ROPE), MLA_HEADS,
                     MLA_ROPE).astype(BF16)
    wkv = w_kv_b.reshape(kv_rank, MLA_HEADS, MLA_NOPE + MLA_V)
    wk = wkv[:, :, :MLA_NOPE].reshape(kv_rank, MLA_HEADS * MLA_NOPE).astype(BF16)
    wv = wkv[:, :, MLA_NOPE:].reshape(kv_rank, MLA_HEADS * MLA_V).astype(BF16)
    q, k, v = _mla_prep(proj, kr_raw, cm, sm, q_a_norm[None, :], kv_a_norm[None, :],
                        wqn, wqr, wk, wv, ql_blk, ckv_blk, seq, tm=min(512, seq))
    o_mla = _attention(q, k, v, batch, seq, tq=min(2048, seq), rq=128, tk=min(512, seq))

    dec_f = jnp.broadcast_to(ret_decay_fwd.astype(F32)[:, None, None], (RET_HEADS, 8, LANES))
    dec_b = jnp.broadcast_to(ret_decay_bwd.astype(F32)[:, None, None], (RET_HEADS, 8, LANES))
    o_ret = _retention(proj, cos_r, sin_r, dec_f, dec_b, ret_gn[None, :], batch, seq, blk,
                       ch=min(256, seq))

    m = _mix_gate(o_mla, o_ret, proj, w_mla_o.astype(BF16), w_ret_o.astype(BF16), ga_blk, gb_blk,
                  tm=min(512, seq))
    wr32 = jnp.pad(w_router, ((0, 0), (0, LANES - N_EXPERTS)))
    wr_hi = wr32.astype(BF16)
    wr = jnp.concatenate([wr_hi, (wr32 - wr_hi.astype(F32)).astype(BF16)], axis=1)
    br = jnp.pad(b_router, (0, LANES - N_EXPERTS))[None, :]
    x1, h2, packed, cnt = _mix_out(m, x2, w_out.astype(BF16), g_post_mix[None, :], gate_m,
                              g_pre_ffn[None, :], scale_f, shift_f, wr, br, seq, tm=min(256, seq))

    e_idx = packed[:, 0:TOP_K].astype(jnp.int32)
    rank = packed[:, TOP_K:2 * TOP_K].astype(jnp.int32)
    counts = cnt[0, :N_EXPERTS].astype(jnp.int32)
    a = t * TOP_K
    rs = min(EXPERT_SUB_ROWS, a // N_EXPERTS)
    rmax = GROUP_SUB * rs
    n_sub = (a + N_EXPERTS * (rs - 1) + rs - 1) // rs
    n_groups = N_EXPERTS + a // rmax
    padded = ((counts + rs - 1) // rs) * rs
    pend = jnp.cumsum(padded)
    pstart = pend - padded
    dest = (pstart[e_idx] + rank).reshape(a).astype(jnp.int32)
    used_sub = (pend[-1] // rs).astype(jnp.int32).reshape(1)
    tail = jnp.concatenate([used_sub, jnp.full((1,), n_sub, jnp.int32)])
    e_sub = padded // rs
    e_grp = (e_sub + GROUP_SUB - 1) // GROUP_SUB
    g_end = jnp.cumsum(e_grp)
    slot = jnp.arange(n_groups, dtype=jnp.int32)
    g_e = jnp.minimum(jnp.sum((g_end[None, :] <= slot[:, None]).astype(jnp.int32), axis=1),
                      N_EXPERTS - 1)
    g_k = slot - (g_end - e_grp)[g_e]
    live = slot < g_end[-1]
    last_e = g_e[jnp.maximum(g_end[-1] - 1, 0)]
    g_e = jnp.where(live, g_e, last_e).astype(jnp.int32)
    e_chunk = (e_sub + jnp.maximum(e_grp, 1) - 1) // jnp.maximum(e_grp, 1)
    g_s = jnp.where(live, pstart[g_e] + g_k * e_chunk[g_e] * rs, 0).astype(jnp.int32)
    g_n = jnp.where(live, jnp.clip(e_sub[g_e] - e_chunk[g_e] * g_k, 0, e_chunk[g_e]), 0).astype(jnp.int32)

    xs = _dispatch(dest, (pstart + counts).astype(jnp.int32), (padded - counts).astype(jnp.int32),
                   tail, h2, n_sub * rs, tm=min(256, seq), tm_e=rs)
    fdim = w_down.shape[1]
    ys = _experts(g_e, g_s, g_n, tail, xs, w_gate_up, b_gate_up[:, None, :], w_down,
                  b_down[:, None, :], rs=rs, pc=EXPERT_PIECE_SUB, rmax=rmax, tf=min(256, fdim),
                  td=min(512, d))
    out = _combine(dest, ys, packed, x1, g_post_ffn[None, :], gate_f, seq, tm=min(256, seq))
    return out.reshape(batch, seq, d)


def kernel(x, c, positions, w_ada, b_ada, g_pre_mix, g_post_mix, g_pre_ffn, g_post_ffn, w_in,
           q_a_norm, w_q_b, kv_a_norm, w_kv_b, w_mla_o, ret_decay_fwd, ret_decay_bwd, ret_gn,
           w_ret_o, w_out, w_router, b_router, w_gate_up, b_gate_up, w_down, b_down):
    params = (w_ada, b_ada, g_pre_mix, g_post_mix, g_pre_ffn, g_post_ffn, w_in, q_a_norm, w_q_b,
              kv_a_norm, w_kv_b, w_mla_o, ret_decay_fwd, ret_decay_bwd, ret_gn, w_ret_o, w_out,
              w_router, b_router, w_gate_up, b_gate_up, w_down, b_down)
    for l in range(w_ada.shape[0]):
        x = _layer(x, c, positions, *[p[l] for p in params])
    return x
```

```python
import functools

import numpy as np
import jax
import jax.numpy as jnp
from jax import lax
from jax.experimental import pallas as pl
from jax.experimental.pallas import tpu as pltpu

F32 = jnp.float32
BF16 = jnp.bfloat16

MLA_HEADS = 8
MLA_NOPE = 128
MLA_ROPE = 64
MLA_V = 128
MLA_QK = MLA_NOPE + MLA_ROPE
RET_HEADS = 8
N_EXPERTS = 32
TOP_K = 4
SWIGLU_LIMIT = 7.0
SWIGLU_ALPHA = 1.702
ROPE_THETA = 10000.0
RMS_EPS = 1e-6
GN_EPS = 1e-6
LOG2E = 1.4426950408889634
N_MOD = 6

LANES = 128
SUBLANES = 8
VMEM_LIMIT = 56 * 1024 * 1024
EXPERT_SUB_ROWS = 256
EXPERT_PIECE_SUB = 4
GROUP_SUB = 9


def _cparams(*sem):
    return pltpu.CompilerParams(dimension_semantics=sem, vmem_limit_bytes=VMEM_LIMIT)


def _rms(x, gain):
    return x * lax.rsqrt(jnp.mean(x * x, axis=-1, keepdims=True) + RMS_EPS) * gain


def _ada_kernel(c_ref, w_ref, b_ref, o_ref):
    o_ref[...] = jnp.dot(c_ref[...], w_ref[...], preferred_element_type=F32,
                         precision=lax.Precision.HIGHEST) + b_ref[...]


def _ada(c_pad, w_ada, b_ada, tn=1024):
    m, d = c_pad.shape
    n = w_ada.shape[1]
    return pl.pallas_call(
        _ada_kernel,
        out_shape=jax.ShapeDtypeStruct((m, n), F32),
        grid=(n // tn,),
        in_specs=[pl.BlockSpec((m, d), lambda j: (0, 0)),
                  pl.BlockSpec((d, tn), lambda j: (0, j)),
                  pl.BlockSpec((1, tn), lambda j: (0, j))],
        out_specs=pl.BlockSpec((m, tn), lambda j: (0, j)),
        compiler_params=_cparams("arbitrary"),
        name="ada",
    )(c_pad, w_ada, b_ada)


def _inproj_kernel(x_ref, g_ref, sc_ref, sh_ref, w_ref, wkr_ref, o_ref, okr_ref, h_ref):
    @pl.when(pl.program_id(1) == 0)
    def _():
        h = _rms(x_ref[...], g_ref[...]) * (1.0 + sc_ref[...]) + sh_ref[...]
        hb = h.astype(BF16)
        h_ref[...] = hb
        okr_ref[...] = jnp.dot(hb, wkr_ref[...], preferred_element_type=F32)

    o_ref[...] = jnp.dot(h_ref[...], w_ref[...], preferred_element_type=F32).astype(o_ref.dtype)


def _in_proj(x2, g, scale, shift, w_main, w_kr, seq, tm, tn):
    t, d = x2.shape
    n = w_main.shape[1]
    per_b = seq // tm
    return pl.pallas_call(
        _inproj_kernel,
        out_shape=(jax.ShapeDtypeStruct((t, n), BF16), jax.ShapeDtypeStruct((t, LANES), F32)),
        grid=(t // tm, n // tn),
        in_specs=[pl.BlockSpec((tm, d), lambda i, j: (i, 0)),
                  pl.BlockSpec((1, d), lambda i, j: (0, 0)),
                  pl.BlockSpec((None, 1, d), lambda i, j: (i // per_b, 0, 0)),
                  pl.BlockSpec((None, 1, d), lambda i, j: (i // per_b, 0, 0)),
                  pl.BlockSpec((d, tn), lambda i, j: (0, j)),
                  pl.BlockSpec((d, LANES), lambda i, j: (0, 0))],
        out_specs=(pl.BlockSpec((tm, tn), lambda i, j: (i, j)),
                   pl.BlockSpec((tm, LANES), lambda i, j: (i, 0))),
        scratch_shapes=[pltpu.VMEM((tm, d), BF16)],
        compiler_params=_cparams("arbitrary", "arbitrary"),
        name="in_proj",
    )(x2, g, scale, shift, w_main, w_kr)


def _rope_kernel(pos_ref, invr_ref, invm_ref, cr_ref, sr_ref, cm_ref, sm_ref):
    pos = pos_ref[...].astype(F32)
    ang_r = pos * invr_ref[...]
    cr_ref[...] = jnp.cos(ang_r)
    sr_ref[...] = jnp.sin(ang_r)
    ang_m = pos * invm_ref[...]
    lane = lax.broadcasted_iota(jnp.int32, ang_m.shape, 1)
    half = MLA_ROPE // 2
    cm_ref[...] = jnp.where(lane < MLA_ROPE, jnp.cos(ang_m), 0.0)
    s = jnp.sin(ang_m)
    sm_ref[...] = jnp.where(lane < half, -s, jnp.where(lane < MLA_ROPE, s, 0.0))


def _rope_tables(pos_col, tm):
    t = pos_col.shape[0]
    half_r = LANES
    inv_r = (1.0 / (np.float32(ROPE_THETA) ** (np.arange(half_r, dtype=np.float32) / np.float32(half_r)))
             ).astype(np.float32)
    half_m = MLA_ROPE // 2
    inv_m32 = (1.0 / (np.float32(ROPE_THETA) ** (np.arange(half_m, dtype=np.float32) / np.float32(half_m)))
               ).astype(np.float32)
    inv_m = np.zeros((LANES,), np.float32)
    inv_m[:half_m] = inv_m32
    inv_m[half_m:2 * half_m] = inv_m32
    tab = jax.ShapeDtypeStruct((t, LANES), F32)
    row = pl.BlockSpec((tm, LANES), lambda i: (i, 0))
    return pl.pallas_call(
        _rope_kernel,
        out_shape=(tab, tab, tab, tab),
        grid=(t // tm,),
        in_specs=[pl.BlockSpec((tm, 1), lambda i: (i, 0)),
                  pl.BlockSpec((1, LANES), lambda i: (0, 0)),
                  pl.BlockSpec((1, LANES), lambda i: (0, 0))],
        out_specs=(row, row, row, row),
        compiler_params=_cparams("arbitrary"),
        name="rope_tables",
    )(pos_col, jnp.asarray(inv_r)[None, :], jnp.asarray(inv_m)[None, :])


def _rot64(x, c, s):
    return x * c + (pltpu.roll(x, 32, 1) + pltpu.roll(x, 96, 1)) * s


def _mla_prep_kernel(ql_ref, ckv_ref, kr_ref, cm_ref, sm_ref, gq_ref, gkv_ref,
                     wqn_ref, wqr_ref, wk_ref, wv_ref, q_ref, k_ref, v_ref):
    c = cm_ref[...]
    s = sm_ref[...]
    qn = _rms(ql_ref[...].astype(F32), gq_ref[...]).astype(BF16)
    q_nope = jnp.dot(qn, wqn_ref[...], preferred_element_type=F32)
    q_rope = jnp.dot(qn, wqr_ref[...], preferred_element_type=F32)
    cn = _rms(ckv_ref[...].astype(F32), gkv_ref[...]).astype(BF16)
    k_nope = jnp.dot(cn, wk_ref[...], preferred_element_type=F32)
    v = jnp.dot(cn, wv_ref[...], preferred_element_type=F32)
    k_rot_t = _rot64(kr_ref[...], c, s).T.astype(BF16)
    lane = lax.broadcasted_iota(jnp.int32, c.shape, 1)
    ones_col = jnp.where(lane == 0, 1.0, 0.0).astype(BF16)
    scale = MLA_QK ** -0.5 * LOG2E
    for h in range(MLA_HEADS):
        lo, hi = h * LANES, (h + 1) * LANES
        q_ref[:, 2 * lo:2 * lo + LANES] = (q_nope[:, lo:hi] * scale).astype(BF16)
        q_ref[:, 2 * lo + LANES:2 * hi] = (_rot64(q_rope[:, lo:hi], c, s) * scale).astype(BF16)
        k_ref[2 * lo:2 * lo + LANES, :] = k_nope[:, lo:hi].T.astype(BF16)
        k_ref[2 * lo + LANES:2 * hi, :] = k_rot_t
        v_ref[:, 2 * lo:2 * lo + LANES] = v[:, lo:hi].astype(BF16)
        v_ref[:, 2 * lo + LANES:2 * hi] = ones_col


def _mla_prep(proj, kr_raw, cm, sm, gq, gkv, wqn, wqr, wk, wv, ql_blk, ckv_blk, seq, tm):
    t = proj.shape[0]
    rank = wqn.shape[0]
    hd = MLA_HEADS * LANES
    per_b = seq // tm
    full = lambda shape: pl.BlockSpec(shape, lambda i: (0, 0))
    row = lambda w: pl.BlockSpec((tm, w), lambda i: (i, 0))
    return pl.pallas_call(
        _mla_prep_kernel,
        out_shape=(jax.ShapeDtypeStruct((t, 2 * hd), BF16),
                   jax.ShapeDtypeStruct(((t // seq) * 2 * hd, seq), BF16),
                   jax.ShapeDtypeStruct((t, 2 * hd), BF16)),
        grid=(t // tm,),
        in_specs=[pl.BlockSpec((tm, rank), lambda i: (i, ql_blk)),
                  pl.BlockSpec((tm, rank), lambda i: (i, ckv_blk)),
                  row(LANES), row(LANES), row(LANES),
                  full((1, rank)), full((1, rank)),
                  full((rank, hd)), full((rank, hd)), full((rank, hd)), full((rank, hd))],
        out_specs=(row(2 * hd),
                   pl.BlockSpec((2 * hd, tm), lambda i: (i // per_b, i % per_b)),
                   row(2 * hd)),
        compiler_params=_cparams("arbitrary"),
        name="mla_prep",
    )(proj, proj, kr_raw, cm, sm, gq, gkv, wqn, wqr, wk, wv)


def _attn_kernel(q_ref, kt_ref, v_ref, o_ref, s_ref, *, rq, tk):
    tq = q_ref.shape[0]
    seq = kt_ref.shape[1]
    for r0 in range(0, tq, rq):
        q = q_ref[r0:r0 + rq, :]
        m_lane = None
        for c0 in range(0, seq, tk):
            s = jnp.dot(q, kt_ref[:, c0:c0 + tk], preferred_element_type=F32)
            s_ref[r0:r0 + rq, c0:c0 + tk] = s
            for l0 in range(0, tk, LANES):
                blk = s[:, l0:l0 + LANES]
                m_lane = blk if m_lane is None else jnp.maximum(m_lane, blk)
        m = jnp.max(m_lane, axis=-1, keepdims=True)
        acc = jnp.zeros((rq, 2 * MLA_V), F32)
        for c0 in range(0, seq, tk):
            p = jnp.exp2(s_ref[r0:r0 + rq, c0:c0 + tk] - m).astype(BF16)
            acc = acc + jnp.dot(p, v_ref[c0:c0 + tk, :], preferred_element_type=F32)
        o_ref[r0:r0 + rq, :] = (acc[:, :MLA_V] / acc[:, MLA_V:MLA_V + 1]).astype(o_ref.dtype)


def _attention(q, kt, v, batch, seq, tq, rq, tk):
    t = q.shape[0]
    nq = seq // tq
    return pl.pallas_call(
        functools.partial(_attn_kernel, rq=rq, tk=tk),
        out_shape=jax.ShapeDtypeStruct((t, MLA_HEADS * MLA_V), BF16),
        grid=(batch, MLA_HEADS, nq),
        in_specs=[pl.BlockSpec((tq, 2 * LANES), lambda b, h, i: (b * nq + i, h)),
                  pl.BlockSpec((2 * LANES, seq), lambda b, h, i: (b * MLA_HEADS + h, 0)),
                  pl.BlockSpec((seq, 2 * MLA_V), lambda b, h, i: (b, h))],
        out_specs=pl.BlockSpec((tq, MLA_V), lambda b, h, i: (b * nq + i, h)),
        scratch_shapes=[pltpu.VMEM((tq, seq), F32)],
        compiler_params=_cparams("arbitrary", "arbitrary", "arbitrary"),
        name="mla_attention",
    )(q, kt, v)


def _ret_kernel(rq_ref, rk_ref, rv_ref, rg_ref, cos_ref, sin_ref, df_ref, db_ref, gn_ref,
                o_ref, krot_ref, sf_ref, st_ref, dmat_ref, xif_ref, zf_ref, xib_ref, zb_ref, *, ch):
    seq, dk = rq_ref.shape
    half = dk // 2
    n_ch = seq // ch
    lgf = jnp.log(jax.nn.sigmoid(df_ref[...]))[:1, :1]
    lgb = jnp.log(jax.nn.sigmoid(db_ref[...]))[:1, :1]

    ri = lax.broadcasted_iota(jnp.int32, (ch, ch), 0)
    ci = lax.broadcasted_iota(jnp.int32, (ch, ch), 1)
    diff = (ri - ci).astype(F32)
    dmat_ref[...] = jnp.where(diff >= 0, jnp.exp(lgf * jnp.maximum(diff, 0.0)),
                              jnp.exp(lgb * jnp.maximum(-diff, 0.0)))
    pos = lax.broadcasted_iota(jnp.int32, (ch, dk), 0).astype(F32)
    xif_ref[...] = jnp.exp(lgf * (pos + 1.0))
    zf_ref[...] = jnp.exp(lgf * (ch - 1.0 - pos))
    xib_ref[...] = jnp.exp(lgb * (ch - pos))
    zb_ref[...] = jnp.exp(lgb * pos)
    gcf = jnp.exp(lgf * float(ch))
    gcb = jnp.exp(lgb * float(ch))
    k_scale = dk ** -0.5

    def rot(x, c, s):
        x1, x2 = x[:, :half], x[:, half:]
        return jnp.concatenate([x1 * c - x2 * s, x1 * s + x2 * c], axis=-1)

    def kv_outer(kz, v):
        return lax.dot_general(kz, v, (((0,), (0,)), ((), ())), preferred_element_type=F32)

    def chunk_pairs(body):
        step = 4 if n_ch % 4 == 0 else (2 if n_ch % 2 == 0 else 1)

        def it(j, carry):
            for u in range(step):
                body(j * step + u, carry)
            return carry

        lax.fori_loop(0, n_ch // step, it, 0)

    st_ref[...] = jnp.zeros(st_ref.shape, F32)

    def fwd(n, carry):
        r0 = pl.multiple_of(n * ch, ch)
        rows = pl.ds(r0, ch)
        k = rot(rk_ref[rows, :].astype(F32), cos_ref[rows, :], sin_ref[rows, :]) * k_scale
        krot_ref[rows, :] = k.astype(BF16)
        sf_ref[n] = st_ref[...].astype(BF16)
        kv = kv_outer((k * zf_ref[...]).astype(BF16), rv_ref[rows, :])
        st_ref[...] = st_ref[...] * gcf + kv
        return carry

    chunk_pairs(fwd)

    st_ref[...] = jnp.zeros(st_ref.shape, F32)

    def bwd(i, carry):
        n = n_ch - 1 - i
        r0 = pl.multiple_of(n * ch, ch)
        rows = pl.ds(r0, ch)
        q = rot(rq_ref[rows, :].astype(F32), cos_ref[rows, :], sin_ref[rows, :])
        kb = krot_ref[rows, :]
        v = rv_ref[rows, :]
        s = lax.dot_general(q.astype(BF16), kb, (((1,), (1,)), ((), ())), preferred_element_type=F32)
        o = jnp.dot((s * dmat_ref[...]).astype(BF16), v, preferred_element_type=F32)
        o = o + jnp.dot((q * xif_ref[...]).astype(BF16), sf_ref[n], preferred_element_type=F32)
        o = o + jnp.dot((q * xib_ref[...]).astype(BF16), st_ref[...].astype(BF16),
                        preferred_element_type=F32)
        mu = jnp.mean(o, axis=-1, keepdims=True)
        oc = o - mu
        var = jnp.mean(oc * oc, axis=-1, keepdims=True)
        on = oc * lax.rsqrt(var + GN_EPS)
        g = rg_ref[rows, :].astype(F32)
        o_ref[rows, :] = (g * jax.nn.sigmoid(g) * (on * gn_ref[...])).astype(o_ref.dtype)
        kv = kv_outer((kb.astype(F32) * zb_ref[...]).astype(BF16), v)
        st_ref[...] = st_ref[...] * gcb + kv
        return carry

    chunk_pairs(bwd)


def _retention(proj, cos_r, sin_r, dec_f, dec_b, gn, batch, seq, blk, ch):
    t = proj.shape[0]
    dk = 2 * LANES
    n_ch = seq // ch
    col = lambda base: pl.BlockSpec((seq, dk), lambda b, h: (b, base + h))
    tab = pl.BlockSpec((seq, LANES), lambda b, h: (b, 0))
    dec = pl.BlockSpec((None, 8, LANES), lambda b, h: (h, 0, 0))
    return pl.pallas_call(
        functools.partial(_ret_kernel, ch=ch),
        out_shape=jax.ShapeDtypeStruct((t, RET_HEADS * dk), BF16),
        grid=(batch, RET_HEADS),
        in_specs=[col(blk["rq"]), col(blk["rk"]), col(blk["rv"]), col(blk["rg"]),
                  tab, tab, dec, dec,
                  pl.BlockSpec((1, dk), lambda b, h: (0, h))],
        out_specs=pl.BlockSpec((seq, dk), lambda b, h: (b, h)),
        scratch_shapes=[pltpu.VMEM((seq, dk), BF16),
                        pltpu.VMEM((n_ch, dk, dk), BF16),
                        pltpu.VMEM((dk, dk), F32),
                        pltpu.VMEM((ch, ch), F32),
                        pltpu.VMEM((ch, dk), F32), pltpu.VMEM((ch, dk), F32),
                        pltpu.VMEM((ch, dk), F32), pltpu.VMEM((ch, dk), F32)],
        compiler_params=_cparams("arbitrary", "arbitrary"),
        name="retention",
    )(proj, proj, proj, proj, cos_r, sin_r, dec_f, dec_b, gn)


def _mixgate_kernel(oa_ref, ob_ref, ga_ref, gb_ref, wa_ref, wb_ref, m_ref):
    ya = jnp.dot(oa_ref[...], wa_ref[...], preferred_element_type=F32)
    yb = jnp.dot(ob_ref[...], wb_ref[...], preferred_element_type=F32)
    m = jax.nn.sigmoid(ga_ref[...].astype(F32)) * ya + jax.nn.sigmoid(gb_ref[...].astype(F32)) * yb
    m_ref[...] = m.astype(m_ref.dtype)


def _mix_gate(o_mla, o_ret, proj, wa, wb, ga_blk, gb_blk, tm):
    t = o_mla.shape[0]
    d = wa.shape[1]
    return pl.pallas_call(
        _mixgate_kernel,
        out_shape=jax.ShapeDtypeStruct((t, d), BF16),
        grid=(t // tm,),
        in_specs=[pl.BlockSpec((tm, o_mla.shape[1]), lambda i: (i, 0)),
                  pl.BlockSpec((tm, o_ret.shape[1]), lambda i: (i, 0)),
                  pl.BlockSpec((tm, d), lambda i: (i, ga_blk)),
                  pl.BlockSpec((tm, d), lambda i: (i, gb_blk)),
                  pl.BlockSpec(wa.shape, lambda i: (0, 0)),
                  pl.BlockSpec(wb.shape, lambda i: (0, 0))],
        out_specs=pl.BlockSpec((tm, d), lambda i: (i, 0)),
        compiler_params=_cparams("arbitrary"),
        name="mix_gate",
    )(o_mla, o_ret, proj, proj, wa, wb)


def _mixout_kernel(m_ref, x_ref, wo_ref, gpost_ref, gate_ref, gpre_ref, sc_ref, sh_ref,
                   wr_ref, br_ref, x1_ref, h2_ref, pk_ref, cnt_ref, carry_ref):
    mix = jnp.dot(m_ref[...], wo_ref[...], preferred_element_type=F32)
    x1 = x_ref[...] + gate_ref[...] * _rms(mix, gpost_ref[...])
    x1_ref[...] = x1
    h2 = _rms(x1, gpre_ref[...]) * (1.0 + sc_ref[...]) + sh_ref[...]
    half = h2.shape[1] // 2
    bits = pltpu.bitcast(h2.astype(BF16).astype(F32), jnp.uint32)
    h2_ref[...] = (bits[:, :half] >> 16) | (bits[:, half:] & jnp.uint32(0xFFFF0000))
    h_hi = h2.astype(BF16)
    h_lo = (h2 - h_hi.astype(F32)).astype(BF16)
    wr = wr_ref[...]
    hh = jnp.dot(h_hi, wr, preferred_element_type=F32)
    lh = jnp.dot(h_lo, wr[:, :LANES], preferred_element_type=F32)
    logits = hh[:, :LANES] + (hh[:, LANES:] + lh) + br_ref[...]

    @pl.when(pl.program_id(0) == 0)
    def _():
        carry_ref[...] = jnp.zeros(carry_ref.shape, F32)

    pk_ref[...] = _route_tile(logits, carry_ref)
    cnt_ref[...] = carry_ref[...]


def _mix_out(m, x2, wo, gpost, gate_m, gpre, scale_f, shift_f, wr, br, seq, tm):
    t, d = x2.shape
    per_b = seq // tm
    row = pl.BlockSpec((tm, d), lambda i: (i, 0))
    vec = pl.BlockSpec((1, d), lambda i: (0, 0))
    mod = pl.BlockSpec((None, 1, d), lambda i: (i // per_b, 0, 0))
    return pl.pallas_call(
        _mixout_kernel,
        out_shape=(jax.ShapeDtypeStruct((t, d), F32), jax.ShapeDtypeStruct((t, d // 2), jnp.uint32),
                   jax.ShapeDtypeStruct((t, LANES), F32), jax.ShapeDtypeStruct((8, LANES), F32)),
        grid=(t // tm,),
        in_specs=[row, row, pl.BlockSpec((d, d), lambda i: (0, 0)), vec, mod, vec, mod, mod,
                  pl.BlockSpec((d, 2 * LANES), lambda i: (0, 0)),
                  pl.BlockSpec((1, LANES), lambda i: (0, 0))],
        out_specs=(row, pl.BlockSpec((tm, d // 2), lambda i: (i, 0)),
                   pl.BlockSpec((tm, LANES), lambda i: (i, 0)),
                   pl.BlockSpec((8, LANES), lambda i: (0, 0))),
        scratch_shapes=[pltpu.VMEM((8, LANES), F32)],
        compiler_params=_cparams("arbitrary"),
        name="mix_out",
    )(m, x2, wo, gpost, gate_m, gpre, scale_f, shift_f, wr, br)


def _route_tile(logits, carry_ref):
    tm = logits.shape[0]
    lane = lax.broadcasted_iota(jnp.int32, (tm, LANES), 1).astype(F32)
    l = jnp.where(lane < N_EXPERTS, logits, -jnp.inf)
    vals, idxs = [], []
    for _ in range(TOP_K):
        mx = jnp.max(l, axis=-1, keepdims=True)
        ik = jnp.min(jnp.where(l == mx, lane, float(LANES)), axis=-1, keepdims=True)
        vals.append(mx)
        idxs.append(ik)
        l = jnp.where(lane == ik, -jnp.inf, l)
    es = [jnp.exp(v - vals[0]) for v in vals]
    den = es[0] + es[1] + es[2] + es[3]
    onehot = jnp.zeros((tm, LANES), F32)
    for ik in idxs:
        onehot = onehot + jnp.where(lane == ik, 1.0, 0.0)
    ri = lax.broadcasted_iota(jnp.int32, (tm, tm), 0)
    ci = lax.broadcasted_iota(jnp.int32, (tm, tm), 1)
    tri = jnp.where(ci < ri, 1.0, 0.0).astype(BF16)
    cum = jnp.dot(tri, onehot.astype(BF16), preferred_element_type=F32) + carry_ref[0:1, :]
    packed = jnp.zeros((tm, LANES), F32)
    for k in range(TOP_K):
        rank = jnp.sum(jnp.where(lane == idxs[k], cum, 0.0), axis=-1, keepdims=True)
        packed = packed + jnp.where(lane == float(k), idxs[k], 0.0)
        packed = packed + jnp.where(lane == float(TOP_K + k), rank, 0.0)
        packed = packed + jnp.where(lane == float(2 * TOP_K + k), es[k] / den, 0.0)
    carry_ref[...] = carry_ref[...] + jnp.sum(onehot, axis=0, keepdims=True)
    return packed


def _dispatch_kernel(dest_ref, pad_pos_ref, pad_len_ref, tail_ref, h_ref, xs_ref, z_ref, sem, zsem,
                     *, tm_e):
    tm = h_ref.shape[0]
    zr = z_ref.shape[0]
    base = pl.program_id(0) * (tm * TOP_K)
    pieces = [p for p in (2 ** i for i in range(20)) if SUBLANES <= p <= zr and p < tm_e][::-1]

    def zero_copy(pos, p):
        return pltpu.make_async_copy(z_ref.at[pl.ds(0, p), :], xs_ref.at[pl.ds(pos, p), :], zsem)

    def pad_rows(act):
        for e in range(N_EXPERTS):
            pos = pad_pos_ref[e]
            head = (-pos) & (SUBLANES - 1)
            for j in range(SUBLANES - 1):
                @pl.when(j < head)
                def _(pos=pos, j=j):
                    act(zero_copy(pos + j, 1))

            pos = pos + head
            ln = pad_len_ref[e] - head
            for p in pieces:
                hit = (ln & p) != 0

                @pl.when(hit)
                def _(pos=pos, p=p):
                    act(zero_copy(pl.multiple_of(pos, SUBLANES), p))

                pos = pos + jnp.where(hit, p, 0)

    def tail_rows(act):
        def body(j, carry):
            for r0 in range(0, tm_e, zr):
                act(zero_copy(pl.multiple_of(j * tm_e + r0, SUBLANES), zr))
            return carry

        lax.fori_loop(tail_ref[0], tail_ref[1], body, 0)

    @pl.when(pl.program_id(0) == 0)
    def _():
        z_ref[...] = jnp.zeros(z_ref.shape, z_ref.dtype)
        pad_rows(lambda c: c.start())
        tail_rows(lambda c: c.start())

    def row_copy(t, d):
        return pltpu.make_async_copy(h_ref.at[pl.ds(t, 1), :], xs_ref.at[pl.ds(d, 1), :], sem)

    def issue(t, carry):
        for k in range(TOP_K):
            row_copy(t, dest_ref[base + t * TOP_K + k]).start(priority=k % 2)
        return carry

    lax.fori_loop(0, tm, issue, 0, unroll=8)

    for k in range(TOP_K):
        pltpu.make_async_copy(h_ref, xs_ref.at[pl.ds(0, tm), :], sem).wait()

    @pl.when(pl.program_id(0) == 0)
    def _():
        pad_rows(lambda c: c.wait())
        tail_rows(lambda c: c.wait())


def _dispatch(dest, pad_pos, pad_len, tail, h2, n_rows, tm, tm_e):
    t, d = h2.shape
    zr = min(256, tm_e)
    nsp = 4
    return pl.pallas_call(
        functools.partial(_dispatch_kernel, tm_e=tm_e),
        out_shape=jax.ShapeDtypeStruct((n_rows, d), h2.dtype),
        grid_spec=pltpu.PrefetchScalarGridSpec(
            num_scalar_prefetch=nsp,
            grid=(t // tm,),
            in_specs=[pl.BlockSpec((tm, d), lambda i, *_: (i, 0))],
            out_specs=pl.BlockSpec(memory_space=pl.ANY),
            scratch_shapes=[pltpu.VMEM((zr, d), h2.dtype), pltpu.SemaphoreType.DMA,
                            pltpu.SemaphoreType.DMA],
        ),
        compiler_params=_cparams("arbitrary"),
        name="dispatch",
    )(dest, pad_pos, pad_len, tail, h2)


def _expert_kernel(ge_ref, gs_ref, gn_ref, tail_ref, xs_ref, wg_ref, wu_ref, bg_ref, bu_ref,
                   wd_ref, bd_ref, ys_ref, xu_ref, xb_ref, ab_ref,
                   ob_ref, orem_ref, z_ref, flag_ref, xsem, osem, rsem, zsem, *, rs, pc, nf, nd):
    del ge_ref
    g = pl.program_id(0)
    s = pl.program_id(1)
    n = gn_ref[g]
    row0 = gs_ref[g]
    half = xu_ref.shape[1]
    tf = wg_ref.shape[1]
    td = wd_ref.shape[1]

    def rows(i):
        return pl.ds(pl.multiple_of(i * rs, rs), rs)

    def x_copy(base, i):
        src = xs_ref.at[pl.ds(pl.multiple_of(base + i * rs, rs), rs), :]
        return pltpu.make_async_copy(src, xu_ref.at[rows(i), :], xsem)

    def x_start(base, cnt):
        def start(i, c):
            x_copy(base, i).start()
            return c

        lax.fori_loop(0, cnt, start, 0)

    g_next = jnp.minimum(g + 1, pl.num_programs(0) - 1)
    n_next = jnp.where(g + 1 < pl.num_programs(0), gn_ref[g_next], 0)

    @pl.when(jnp.logical_and(n > 0, s == 1))
    def _():
        x_start(gs_ref[g_next], n_next)

    @pl.when(jnp.logical_and(n > 0, s == 0))
    def _():
        @pl.when(g == 0)
        def _():
            x_start(row0, n)

        def wait(i, c):
            x_copy(row0, i).wait()
            return c

        lax.fori_loop(0, n, wait, 0)

        def unpack(i, c):
            u = xu_ref[rows(i), :]
            lo = pltpu.bitcast(u << 16, F32)
            hi = pltpu.bitcast(u & jnp.uint32(0xFFFF0000), F32)
            xb_ref[rows(i), :half] = lo.astype(BF16)
            xb_ref[rows(i), half:] = hi.astype(BF16)
            return c

        lax.fori_loop(0, n, unpack, 0)

    small = [pc >> j for j in range(1, pc.bit_length())]
    out_bufs = [(ob_ref.at[0], osem.at[0], pc), (ob_ref.at[1], osem.at[1], pc)]
    off = 0
    for j, sz in enumerate(small):
        out_bufs.append((orem_ref.at[pl.ds(off, sz * rs), :], rsem.at[j], sz))
        off += sz * rs
    n_bufs = len(out_bufs)

    def out_wait(k):
        buf, sem, sz = out_bufs[k]
        pltpu.make_async_copy(buf, ys_ref.at[pl.ds(0, sz * rs), 0:td], sem).wait()

    def drain(k):
        @pl.when(flag_ref[k] == 1)
        def _():
            out_wait(k)
            flag_ref[k] = 0

    @pl.when(jnp.logical_and(g == 0, s == 0))
    def _():
        for k in range(n_bufs):
            flag_ref[k] = 0

    n_big = lax.shift_right_logical(n, (2 * pc).bit_length() - 1)
    leftovers = []
    r_next = n_big * (2 * pc * rs)
    for k, sz in [(0, pc)] + [(2 + j, sz) for j, sz in enumerate(small)]:
        present = (n & sz) != 0
        leftovers.append((present, r_next, sz, k))
        r_next = r_next + jnp.where(present, sz * rs, 0)

    def row_slice(r0, m):
        return pl.ds(pl.multiple_of(r0, rs), m)

    @pl.when(jnp.logical_and(n > 0, s < nf))
    def _():
        def act_rows(r0, m, wgb, wub):
            x = xb_ref[row_slice(r0, m), :]
            gt = jnp.dot(x, wgb, preferred_element_type=F32) + bg_ref[...]
            ut = jnp.dot(x, wub, preferred_element_type=F32) + bu_ref[...]
            gt = jnp.minimum(gt, SWIGLU_LIMIT)
            ut = jnp.clip(ut, -SWIGLU_LIMIT, SWIGLU_LIMIT)
            act = (ut + 1.0) * (gt * jax.nn.sigmoid(SWIGLU_ALPHA * gt))
            ab_ref[s, row_slice(r0, m), :] = act.astype(BF16)

        def chunks(r0, sizes):
            wgb = wg_ref[...].astype(BF16)
            wub = wu_ref[...].astype(BF16)
            for m in sizes:
                act_rows(r0, m, wgb, wub)
                r0 = r0 + m

        def big_body(i, c):
            chunks(i * (2 * pc * rs), (pc * rs, pc * rs))
            return c

        lax.fori_loop(0, n_big, big_body, 0)

        for present, r0, sz, _ in leftovers:
            @pl.when(present)
            def _(r0=r0, sz=sz):
                chunks(r0, (sz * rs,))

    @pl.when(jnp.logical_and(n > 0, s >= nf))
    def _():
        d = s - nf

        def out_copy(r0, m, buf, sem, j):
            dst = ys_ref.at[pl.ds(pl.multiple_of(row0 + r0, rs), m), j * td:(j + 1) * td]
            return pltpu.make_async_copy(buf, dst, sem)

        def emit(r0, k, wdb):
            buf, sem, sz = out_bufs[k]
            m = sz * rs
            drain(k)
            hm = m // 2 if sz >= 2 else m
            for h0 in range(0, m, hm):
                a_full = jnp.concatenate([ab_ref[f, row_slice(r0 + h0, hm), :] for f in range(nf)],
                                         axis=1)
                buf[h0:h0 + hm, :] = jnp.dot(a_full, wdb, preferred_element_type=F32) + bd_ref[...]
            for j in range(nd):
                @pl.when(d == j)
                def _(j=j):
                    out_copy(r0, m, buf, sem, j).start()

            flag_ref[k] = 1

        def w_down():
            return wd_ref[...].astype(BF16)

        for present, r0, sz, k in reversed(leftovers):
            @pl.when(present)
            def _(r0=r0, k=k):
                emit(r0, k, w_down())

        def big_body(i, c):
            wdb = w_down()
            emit(i * (2 * pc * rs), 0, wdb)
            emit(i * (2 * pc * rs) + pc * rs, 1, wdb)
            return c

        lax.fori_loop(0, n_big, big_body, 0)

    @pl.when(jnp.logical_and(g == pl.num_programs(0) - 1, s == nf + nd - 1))
    def _():
        for k in range(n_bufs):
            drain(k)
        z_ref[...] = jnp.zeros(z_ref.shape, z_ref.dtype)

        def z_copy(i):
            dst = ys_ref.at[pl.ds(pl.multiple_of(i * rs, rs), rs), :]
            return pltpu.make_async_copy(z_ref, dst, zsem)

        def start(i, c):
            z_copy(i).start()
            return c

        def wait(i, c):
            z_copy(i).wait()
            return c

        lax.fori_loop(tail_ref[0], tail_ref[1], start, 0)
        lax.fori_loop(tail_ref[0], tail_ref[1], wait, 0)


def _experts(ge, gs, gn, tail, xs, w_gu, b_gu, w_dn, b_dn, rs, pc, rmax, tf, td):
    p = xs.shape[0]
    n_e, d, f2 = w_gu.shape
    fdim = f2 // 2
    nf = fdim // tf
    nd = d // td
    n_groups = ge.shape[0]

    def f_idx(g, s, gn):
        return jnp.where(gn[g] > 0, jnp.minimum(s, nf - 1), nf - 1)

    def d_idx(g, s, gn):
        return jnp.where(gn[g] > 0, jnp.maximum(s - nf, 0), nd - 1)

    return pl.pallas_call(
        functools.partial(_expert_kernel, rs=rs, pc=pc, nf=nf, nd=nd),
        out_shape=jax.ShapeDtypeStruct((p, d), F32),
        grid_spec=pltpu.PrefetchScalarGridSpec(
            num_scalar_prefetch=4,
            grid=(n_groups, nf + nd),
            in_specs=[pl.BlockSpec(memory_space=pl.ANY),
                      pl.BlockSpec((None, d, tf), lambda g, s, ge, gs, gn, tl: (ge[g], 0, f_idx(g, s, gn))),
                      pl.BlockSpec((None, d, tf),
                                   lambda g, s, ge, gs, gn, tl: (ge[g], 0, nf + f_idx(g, s, gn))),
                      pl.BlockSpec((None, 1, tf), lambda g, s, ge, gs, gn, tl: (ge[g], 0, f_idx(g, s, gn))),
                      pl.BlockSpec((None, 1, tf),
                                   lambda g, s, ge, gs, gn, tl: (ge[g], 0, nf + f_idx(g, s, gn))),
                      pl.BlockSpec((None, fdim, td), lambda g, s, ge, gs, gn, tl: (ge[g], 0, d_idx(g, s, gn))),
                      pl.BlockSpec((None, 1, td), lambda g, s, ge, gs, gn, tl: (ge[g], 0, d_idx(g, s, gn)))],
            out_specs=pl.BlockSpec(memory_space=pl.ANY),
            scratch_shapes=[pltpu.VMEM((rmax, d // 2), jnp.uint32),
                            pltpu.VMEM((rmax, d), BF16),
                            pltpu.VMEM((nf, rmax, tf), BF16),
                            pltpu.VMEM((2, pc * rs, td), F32),
                            pltpu.VMEM((pc * rs, td), F32),
                            pltpu.VMEM((rs, d), F32),
                            pltpu.SMEM((2 + pc.bit_length(),), jnp.int32),
                            pltpu.SemaphoreType.DMA, pltpu.SemaphoreType.DMA((2,)),
                            pltpu.SemaphoreType.DMA((pc.bit_length(),)),
                            pltpu.SemaphoreType.DMA],
        ),
        compiler_params=_cparams("arbitrary", "arbitrary"),
        name="experts",
    )(ge, gs, gn, tail, xs, w_gu, w_gu, b_gu, b_gu, w_dn, b_dn)


def _combine_kernel(dest_ref, ys_ref, pk_ref, x1_ref, gpost_ref, gate_ref, o_ref, buf_ref, sem):
    tm = x1_ref.shape[0]
    base = pl.program_id(0) * (tm * TOP_K)

    def row_copy(t, k, d):
        return pltpu.make_async_copy(ys_ref.at[pl.ds(d, 1), :], buf_ref.at[k, pl.ds(t, 1), :], sem)

    def issue(t, carry):
        for k in range(TOP_K):
            row_copy(t, k, dest_ref[base + t * TOP_K + k]).start(priority=k % 2)
        return carry

    lax.fori_loop(0, tm, issue, 0, unroll=8)

    for k in range(TOP_K):
        pltpu.make_async_copy(ys_ref.at[pl.ds(0, tm), :], buf_ref.at[k], sem).wait()

    pk = pk_ref[...]
    f = jnp.zeros(x1_ref.shape, F32)
    for k in range(TOP_K):
        w = pk[:, 2 * TOP_K + k:2 * TOP_K + k + 1]
        f = f + buf_ref[k] * w
    o_ref[...] = x1_ref[...] + gate_ref[...] * _rms(f, gpost_ref[...])


def _combine(dest, ys, packed, x1, gpost, gate_f, seq, tm):
    t, d = x1.shape
    per_b = seq // tm
    return pl.pallas_call(
        _combine_kernel,
        out_shape=jax.ShapeDtypeStruct((t, d), F32),
        grid_spec=pltpu.PrefetchScalarGridSpec(
            num_scalar_prefetch=1,
            grid=(t // tm,),
            in_specs=[pl.BlockSpec(memory_space=pl.ANY),
                      pl.BlockSpec((tm, LANES), lambda i, dest: (i, 0)),
                      pl.BlockSpec((tm, d), lambda i, dest: (i, 0)),
                      pl.BlockSpec((1, d), lambda i, dest: (0, 0)),
                      pl.BlockSpec((None, 1, d), lambda i, dest: (i // per_b, 0, 0))],
            out_specs=pl.BlockSpec((tm, d), lambda i, dest: (i, 0)),
            scratch_shapes=[pltpu.VMEM((TOP_K, tm, d), F32), pltpu.SemaphoreType.DMA],
        ),
        compiler_params=_cparams("arbitrary"),
        name="combine",
    )(dest, ys, packed, x1, gpost, gate_f)


def _pad_heads(w, n_heads, width):
    k = w.shape[0]
    w = w.reshape(k, n_heads, width)
    return jnp.pad(w, ((0, 0), (0, 0), (0, LANES - width))).reshape(k, n_heads * LANES)


def _layer(x, c, positions, w_ada, b_ada, g_pre_mix, g_post_mix, g_pre_ffn, g_post_ffn,
           w_in, q_a_norm, w_q_b, kv_a_norm, w_kv_b, w_mla_o, ret_decay_fwd, ret_decay_bwd,
           ret_gn, w_ret_o, w_out, w_router, b_router, w_gate_up, b_gate_up, w_down, b_down):
    batch, seq, d = x.shape
    t = batch * seq
    q_rank = q_a_norm.shape[0]
    kv_rank = kv_a_norm.shape[0]
    hd = RET_HEADS * 2 * LANES
    x2 = x.reshape(t, d)

    c_pad = jnp.pad(c, ((0, 8 - batch), (0, 0)))
    ada = _ada(c_pad, w_ada, b_ada[None, :])[:batch]
    shift_m, scale_m, gate_m, shift_f, scale_f, gate_f = [
        ada[:, i * d:(i + 1) * d].reshape(batch, 1, d) for i in range(N_MOD)]

    o_q, o_kv = 0, q_rank
    o_r = q_rank + kv_rank + MLA_ROPE
    w_wide = w_in[:, o_r:]
    w_main = jnp.concatenate([w_wide, w_in[:, o_q:o_q + q_rank], w_in[:, o_kv:o_kv + kv_rank]],
                             axis=1).astype(BF16)
    w_kr = jnp.pad(w_in[:, o_kv + kv_rank:o_r], ((0, 0), (0, LANES - MLA_ROPE))).astype(BF16)
    n_wide = w_wide.shape[1]
    proj, kr_raw = _in_proj(x2, g_pre_mix[None, :], scale_m, shift_m, w_main, w_kr, seq,
                            tm=min(1024, seq), tn=1024)
    blk_w = 2 * LANES
    blk = {"rq": 0, "rk": hd // blk_w, "rv": 2 * hd // blk_w, "rg": 3 * hd // blk_w}
    ga_blk, gb_blk = (4 * hd) // d, (4 * hd + d) // d
    ql_blk, ckv_blk = n_wide // q_rank, (n_wide + q_rank) // kv_rank

    cos_r, sin_r, cm, sm = _rope_tables(positions.reshape(t, 1), tm=min(512, seq))

    wq = w_q_b.reshape(q_rank, MLA_HEADS, MLA_QK)
    wqn = wq[:, :, :MLA_NOPE].reshape(q_rank, MLA_HEADS * MLA_NOPE).astype(BF16)
    wqr = _pad_heads(wq[:, :, MLA_NOPE:].reshape(q_rank, MLA_HEADS * MLA_ROPE), MLA_HEADS,
                     MLA_ROPE).astype(BF16)
    wkv = w_kv_b.reshape(kv_rank, MLA_HEADS, MLA_NOPE + MLA_V)
    wk = wkv[:, :, :MLA_NOPE].reshape(kv_rank, MLA_HEADS * MLA_NOPE).astype(BF16)
    wv = wkv[:, :, MLA_NOPE:].reshape(kv_rank, MLA_HEADS * MLA_V).astype(BF16)
    q, k, v = _mla_prep(proj, kr_raw, cm, sm, q_a_norm[None, :], kv_a_norm[None, :],
                        wqn, wqr, wk, wv, ql_blk, ckv_blk, seq, tm=min(512, seq))
    o_mla = _attention(q, k, v, batch, seq, tq=min(2048, seq), rq=128, tk=min(512, seq))

    dec_f = jnp.broadcast_to(ret_decay_fwd.astype(F32)[:, None, None], (RET_HEADS, 8, LANES))
    dec_b = jnp.broadcast_to(ret_decay_bwd.astype(F32)[:, None, None], (RET_HEADS, 8, LANES))
    o_ret = _retention(proj, cos_r, sin_r, dec_f, dec_b, ret_gn[None, :], batch, seq, blk,
                       ch=min(256, seq))

    m = _mix_gate(o_mla, o_ret, proj, w_mla_o.astype(BF16), w_ret_o.astype(BF16), ga_blk, gb_blk,
                  tm=min(512, seq))
    wr32 = jnp.pad(w_router, ((0, 0), (0, LANES - N_EXPERTS)))
    wr_hi = wr32.astype(BF16)
    wr = jnp.concatenate([wr_hi, (wr32 - wr_hi.astype(F32)).astype(BF16)], axis=1)
    br = jnp.pad(b_router, (0, LANES - N_EXPERTS))[None, :]
    x1, h2, packed, cnt = _mix_out(m, x2, w_out.astype(BF16), g_post_mix[None, :], gate_m,
                              g_pre_ffn[None, :], scale_f, shift_f, wr, br, seq, tm=min(256, seq))

    e_idx = packed[:, 0:TOP_K].astype(jnp.int32)
    rank = packed[:, TOP_K:2 * TOP_K].astype(jnp.int32)
    counts = cnt[0, :N_EXPERTS].astype(jnp.int32)
    a = t * TOP_K
    rs = min(EXPERT_SUB_ROWS, a // N_EXPERTS)
    rmax = GROUP_SUB * rs
    n_sub = (a + N_EXPERTS * (rs - 1) + rs - 1) // rs
    n_groups = N_EXPERTS + a // rmax
    padded = ((counts + rs - 1) // rs) * rs
    pend = jnp.cumsum(padded)
    pstart = pend - padded
    dest = (pstart[e_idx] + rank).reshape(a).astype(jnp.int32)
    used_sub = (pend[-1] // rs).astype(jnp.int32).reshape(1)
    tail = jnp.concatenate([used_sub, jnp.full((1,), n_sub, jnp.int32)])
    e_sub = padded // rs
    e_grp = (e_sub + GROUP_SUB - 1) // GROUP_SUB
    g_end = jnp.cumsum(e_grp)
    slot = jnp.arange(n_groups, dtype=jnp.int32)
    g_e = jnp.minimum(jnp.sum((g_end[None, :] <= slot[:, None]).astype(jnp.int32), axis=1),
                      N_EXPERTS - 1)
    g_k = slot - (g_end - e_grp)[g_e]
    live = slot < g_end[-1]
    last_e = g_e[jnp.maximum(g_end[-1] - 1, 0)]
    g_e = jnp.where(live, g_e, last_e).astype(jnp.int32)
    e_chunk = (e_sub + jnp.maximum(e_grp, 1) - 1) // jnp.maximum(e_grp, 1)
    g_s = jnp.where(live, pstart[g_e] + g_k * e_chunk[g_e] * rs, 0).astype(jnp.int32)
    g_n = jnp.where(live, jnp.clip(e_sub[g_e] - e_chunk[g_e] * g_k, 0, e_chunk[g_e]), 0).astype(jnp.int32)

    xs = _dispatch(dest, (pstart + counts).astype(jnp.int32), (padded - counts).astype(jnp.int32),
                   tail, h2, n_sub * rs, tm=min(512, seq), tm_e=rs)
    fdim = w_down.shape[1]
    ys = _experts(g_e, g_s, g_n, tail, xs, w_gate_up, b_gate_up[:, None, :], w_down,
                  b_down[:, None, :], rs=rs, pc=EXPERT_PIECE_SUB, rmax=rmax, tf=min(256, fdim),
                  td=min(512, d))
    out = _combine(dest, ys, packed, x1, g_post_ffn[None, :], gate_f, seq, tm=min(512, seq))
    return out.reshape(batch, seq, d)


def kernel(x, c, positions, w_ada, b_ada, g_pre_mix, g_post_mix, g_pre_ffn, g_post_ffn, w_in,
           q_a_norm, w_q_b, kv_a_norm, w_kv_b, w_mla_o, ret_decay_fwd, ret_decay_bwd, ret_gn,
           w_ret_o, w_out, w_router, b_router, w_gate_up, b_gate_up, w_down, b_down):
    params = (w_ada, b_ada, g_pre_mix, g_post_mix, g_pre_ffn, g_post_ffn, w_in, q_a_norm, w_q_b,
              kv_a_norm, w_kv_b, w_mla_o, ret_decay_fwd, ret_decay_bwd, ret_gn, w_ret_o, w_out,
              w_router, b_router, w_gate_up, b_gate_up, w_down, b_down)
    for l in range(w_ada.shape[0]):
        x = _layer(x, c, positions, *[p[l] for p in params])
    return x
```
